```python
import math
import jax, jax.numpy as jnp
from jax import lax
import numpy as np

D_MODEL = 1024
BATCH = 16
SEQ = 2048
DEPTH = 2
DEC_BATCH = 128
DEC_SEQ = 1
PAST_LEN = 16384
PAGE_SIZE = 128

N_MIXERS = 2
N_ATTN_LAYERS = (DEPTH + 1) // 2
N_RET_LAYERS = DEPTH // 2
ATT_HEADS = 32
ATT_KV_HEADS = 4
ATT_GROUP = ATT_HEADS // ATT_KV_HEADS
ATT_HEAD_DIM = 64
ATT_WIDTH = ATT_HEADS * ATT_HEAD_DIM
ATT_KV_WIDTH = ATT_KV_HEADS * ATT_HEAD_DIM
ATT_IN = ATT_WIDTH + 2 * ATT_KV_WIDTH + ATT_WIDTH
WINDOW = 128
NUM_BUCKETS = 32
MAX_DISTANCE = 128
RET_HEADS = 4
RET_QK_DIM = 256
RET_V_DIM = 512
RET_QK_WIDTH = RET_HEADS * RET_QK_DIM
RET_WIDTH = RET_HEADS * RET_V_DIM
RET_IN = 2 * RET_QK_WIDTH + 2 * RET_WIDTH
RET_CHUNK = 128
ROPE_BASE = 10000.0
EPS = 1e-6
NEG = -1e30

kernel_name = 'hybrid_swa_sink_retention_step'


def _rmsnorm(x, g):
    xf = x.astype(jnp.float32)
    y = xf * lax.rsqrt(jnp.mean(xf * xf, axis=-1, keepdims=True) + EPS) * g.astype(jnp.float32)
    return y.astype(x.dtype)


def _t5_bucket(rel):
    n = jnp.maximum(rel, 0)
    max_exact = NUM_BUCKETS // 2
    nf = jnp.maximum(n, 1).astype(jnp.float32)
    large = max_exact + (jnp.log(nf / max_exact) / math.log(MAX_DISTANCE / max_exact)
                         * (NUM_BUCKETS - max_exact)).astype(jnp.int32)
    large = jnp.minimum(large, NUM_BUCKETS - 1)
    return jnp.where(n < max_exact, n, large)


def _gated_out(o, g, w_out):
    y = jax.nn.silu(g.astype(jnp.float32)) * o.astype(jnp.float32)
    return y.astype(w_out.dtype) @ w_out


def _attn_project(h, w_in):
    B, T, _ = h.shape
    proj = h @ w_in
    q, k, v, g = jnp.split(proj, [ATT_WIDTH, ATT_WIDTH + ATT_KV_WIDTH, ATT_WIDTH + 2 * ATT_KV_WIDTH], axis=-1)
    q = q.reshape(B, T, ATT_KV_HEADS, ATT_GROUP, ATT_HEAD_DIM)
    k = k.reshape(B, T, ATT_KV_HEADS, ATT_HEAD_DIM)
    v = v.reshape(B, T, ATT_KV_HEADS, ATT_HEAD_DIM)
    return q, k, v, g


def _swa_attend(q, k, v, q_pos, k_pos, rel_bias, sink):
    Tq, Tk = q.shape[1], k.shape[1]
    logits = jnp.einsum('bqkgd,bskd->bkgqs', q.astype(jnp.float32), k.astype(jnp.float32)) * (ATT_HEAD_DIM ** -0.5)
    rel = q_pos[:, None] - k_pos[None, :]
    bias = jnp.transpose(rel_bias.astype(jnp.float32)[_t5_bucket(rel)], (2, 0, 1))
    bias = bias.reshape(ATT_KV_HEADS, ATT_GROUP, Tq, Tk)
    valid = (k_pos[None, :] >= 0) & (rel >= 0) & (rel < WINDOW)
    logits = jnp.where(valid, logits + bias, NEG)
    s = sink.astype(jnp.float32).reshape(ATT_KV_HEADS, ATT_GROUP, 1, 1)
    m = jnp.maximum(jnp.max(logits, axis=-1, keepdims=True), s)
    p = jnp.exp(logits - m)
    w = p / (jnp.sum(p, axis=-1, keepdims=True) + jnp.exp(s - m))
    return jnp.einsum('bkgqs,bskd->bqkgd', w, v.astype(jnp.float32))


def _attn_prompt(h, w_in, sink, w_out, rel_bias):
    B, S, _ = h.shape
    q, k, v, g = _attn_project(h, w_in)
    nblk = S // WINDOW
    pad = jnp.zeros((B, WINDOW, ATT_KV_HEADS, ATT_HEAD_DIM), k.dtype)
    k_ext = jnp.concatenate([pad, k], axis=1).reshape(B, nblk + 1, WINDOW, ATT_KV_HEADS, ATT_HEAD_DIM)
    v_ext = jnp.concatenate([pad, v], axis=1).reshape(B, nblk + 1, WINDOW, ATT_KV_HEADS, ATT_HEAD_DIM)
    k_prev, k_cur = jnp.moveaxis(k_ext[:, :-1], 1, 0), jnp.moveaxis(k_ext[:, 1:], 1, 0)
    v_prev, v_cur = jnp.moveaxis(v_ext[:, :-1], 1, 0), jnp.moveaxis(v_ext[:, 1:], 1, 0)
    q_blk = jnp.moveaxis(q.reshape(B, nblk, WINDOW, ATT_KV_HEADS, ATT_GROUP, ATT_HEAD_DIM), 1, 0)
    starts = jnp.arange(nblk, dtype=jnp.int32) * WINDOW

    def one_block(args):
        qb, kp, kc, vp, vc, s0 = args
        kb = jnp.concatenate([kp, kc], axis=1)
        vb = jnp.concatenate([vp, vc], axis=1)
        q_pos = s0 + jnp.arange(WINDOW, dtype=jnp.int32)
        k_pos = s0 - WINDOW + jnp.arange(2 * WINDOW, dtype=jnp.int32)
        return _swa_attend(qb, kb, vb, q_pos, k_pos, rel_bias, sink)

    o = lax.map(one_block, (q_blk, k_prev, k_cur, v_prev, v_cur, starts))
    o = jnp.moveaxis(o, 0, 1).reshape(B, S, ATT_WIDTH)
    y = _gated_out(o, g, w_out)
    return y, k[:, S - WINDOW:], v[:, S - WINDOW:]


def _attn_sample(h, ck, cv, w_in, sink, w_out, rel_bias):
    B, T, _ = h.shape
    q, k, v, g = _attn_project(h, w_in)
    kb = jnp.concatenate([ck.astype(k.dtype), k], axis=1)
    vb = jnp.concatenate([cv.astype(v.dtype), v], axis=1)
    q_pos = PAST_LEN + jnp.arange(T, dtype=jnp.int32)
    k_pos = PAST_LEN - WINDOW + jnp.arange(WINDOW + T, dtype=jnp.int32)
    o = _swa_attend(q, kb, vb, q_pos, k_pos, rel_bias, sink).reshape(B, T, ATT_WIDTH)
    y = _gated_out(o, g, w_out)
    return y, kb[:, T:].astype(ck.dtype), vb[:, T:].astype(cv.dtype)


def _log_gamma():
    return jnp.log1p(-jnp.exp2(-5.0 - jnp.arange(RET_HEADS, dtype=jnp.float32)))


def _rotary(x, pos):
    half = RET_QK_DIM // 2
    inv = ROPE_BASE ** (-jnp.arange(half, dtype=jnp.float32) / half)
    ang = pos.astype(jnp.float32)[:, None] * inv[None, :]
    cos = jnp.cos(ang)[None, :, None, :]
    sin = jnp.sin(ang)[None, :, None, :]
    x1, x2 = x[..., :half], x[..., half:]
    return jnp.concatenate([x1 * cos - x2 * sin, x1 * sin + x2 * cos], axis=-1)


def _ret_project(h, w_in, pos):
    B, T, _ = h.shape
    proj = h @ w_in
    q, k, v, g = jnp.split(proj, [RET_QK_WIDTH, 2 * RET_QK_WIDTH, 2 * RET_QK_WIDTH + RET_WIDTH], axis=-1)
    q = _rotary(q.reshape(B, T, RET_HEADS, RET_QK_DIM).astype(jnp.float32), pos)
    k = _rotary(k.reshape(B, T, RET_HEADS, RET_QK_DIM).astype(jnp.float32), pos) * (RET_QK_DIM ** -0.5)
    v = v.reshape(B, T, RET_HEADS, RET_V_DIM).astype(jnp.float32)
    return q, k, v, g


def _retention_chunk(q, k, v, R, log_gamma):
    C = q.shape[1]
    idx = jnp.arange(C, dtype=jnp.float32)
    diff = idx[:, None] - idx[None, :]
    decay = jnp.where(diff >= 0, jnp.exp(jnp.maximum(diff, 0.0)[None] * log_gamma[:, None, None]), 0.0)
    scores = jnp.einsum('bqhd,bshd->bhqs', q, k) * decay
    o = jnp.einsum('bhqs,bshv->bqhv', scores, v)
    cross = jnp.exp((idx[:, None] + 1.0) * log_gamma[None, :])
    o = o + jnp.einsum('bqhd,bhdv->bqhv', q, R) * cross[None, :, :, None]
    k_dec = k * jnp.exp((C - 1.0 - idx)[:, None] * log_gamma[None, :])[None, :, :, None]
    R_new = jnp.exp(C * log_gamma)[None, :, None, None] * R + jnp.einsum('bshd,bshv->bhdv', k_dec, v)
    return o, R_new


def _ret_output(o, g, w_out):
    B, T = o.shape[0], o.shape[1]
    mu = jnp.mean(o, axis=-1, keepdims=True)
    var = jnp.mean(jnp.square(o - mu), axis=-1, keepdims=True)
    o = ((o - mu) * lax.rsqrt(var + EPS)).reshape(B, T, RET_WIDTH)
    return _gated_out(o, g, w_out)


def _ret_prompt(h, w_in, w_out):
    B, S, _ = h.shape
    lg = _log_gamma()
    q, k, v, g = _ret_project(h, w_in, jnp.arange(S, dtype=jnp.int32))
    nch = S // RET_CHUNK

    def to_chunks(t):
        return jnp.moveaxis(t.reshape(B, nch, RET_CHUNK, *t.shape[2:]), 1, 0)

    def step(R, xs):
        qc, kc, vc = xs
        o, R_new = _retention_chunk(qc, kc, vc, R, lg)
        return R_new, o

    R0 = jnp.zeros((B, RET_HEADS, RET_QK_DIM, RET_V_DIM), jnp.float32)
    R_fin, o = lax.scan(step, R0, (to_chunks(q), to_chunks(k), to_chunks(v)))
    o = jnp.moveaxis(o, 0, 1).reshape(B, S, RET_HEADS, RET_V_DIM)
    return _ret_output(o, g, w_out), R_fin.astype(h.dtype)


def _ret_sample(h, state, w_in, w_out):
    T = h.shape[1]
    lg = _log_gamma()
    q, k, v, g = _ret_project(h, w_in, PAST_LEN + jnp.arange(T, dtype=jnp.int32))
    o, R_new = _retention_chunk(q, k, v, state.astype(jnp.float32), lg)
    return _ret_output(o, g, w_out), R_new.astype(state.dtype)


def setup_inputs(seed: int = 0) -> dict:
    key = jax.random.key(seed)
    ks = jax.random.split(key, 14)
    f32 = jnp.float32
    x_prompt = jax.random.normal(ks[0], (BATCH, SEQ, D_MODEL), f32)
    x_sample = jax.random.normal(ks[1], (DEC_BATCH, DEC_SEQ, D_MODEL), f32)
    cache_swa_k = jax.random.normal(ks[2], (N_ATTN_LAYERS, DEC_BATCH, WINDOW, ATT_KV_HEADS, ATT_HEAD_DIM), f32)
    cache_swa_v = jax.random.normal(ks[3], (N_ATTN_LAYERS, DEC_BATCH, WINDOW, ATT_KV_HEADS, ATT_HEAD_DIM), f32)
    state_ret = 0.5 * jax.random.normal(ks[4], (N_RET_LAYERS, DEC_BATCH, RET_HEADS, RET_QK_DIM, RET_V_DIM), f32)
    norm_g = 1.0 + 0.02 * jax.random.normal(ks[5], (DEPTH, D_MODEL), f32)
    final_norm_g = 1.0 + 0.02 * jax.random.normal(ks[6], (D_MODEL,), f32)
    rel_bias = 0.5 * jax.random.normal(ks[7], (NUM_BUCKETS, ATT_HEADS), f32)
    w_in_attn = jax.random.normal(ks[8], (N_ATTN_LAYERS, D_MODEL, ATT_IN), f32) * D_MODEL ** -0.5
    attn_sinks = 0.5 * jax.random.normal(ks[9], (N_ATTN_LAYERS, ATT_HEADS), f32)
    w_out_attn = jax.random.normal(ks[10], (N_ATTN_LAYERS, ATT_WIDTH, D_MODEL), f32) * ATT_WIDTH ** -0.5
    w_in_ret = jax.random.normal(ks[11], (N_RET_LAYERS, D_MODEL, RET_IN), f32) * D_MODEL ** -0.5
    w_out_ret = jax.random.normal(ks[12], (N_RET_LAYERS, RET_WIDTH, D_MODEL), f32) * RET_WIDTH ** -0.5
    return {'x_prompt': x_prompt, 'x_sample': x_sample,
            'cache_swa_k': cache_swa_k, 'cache_swa_v': cache_swa_v, 'state_ret': state_ret,
            'norm_g': norm_g, 'final_norm_g': final_norm_g, 'rel_bias': rel_bias,
            'w_in_attn': w_in_attn, 'attn_sinks': attn_sinks, 'w_out_attn': w_out_attn,
            'w_in_ret': w_in_ret, 'w_out_ret': w_out_ret}


def reference(x_prompt, x_sample, cache_swa_k, cache_swa_v, state_ret, norm_g, final_norm_g, rel_bias,
              w_in_attn, attn_sinks, w_out_attn, w_in_ret, w_out_ret):
    xp, xs = x_prompt, x_sample
    kp_l, vp_l, rp_l, ks_l, vs_l, rs_l = [], [], [], [], [], []
    for i in range(DEPTH):
        hp = _rmsnorm(xp, norm_g[i])
        hs = _rmsnorm(xs, norm_g[i])
        j = i // N_MIXERS
        if i % N_MIXERS == 0:
            yp, kp, vp = _attn_prompt(hp, w_in_attn[j], attn_sinks[j], w_out_attn[j], rel_bias)
            ys, kn, vn = _attn_sample(hs, cache_swa_k[j], cache_swa_v[j], w_in_attn[j], attn_sinks[j],
                                      w_out_attn[j], rel_bias)
            kp_l.append(kp); vp_l.append(vp); ks_l.append(kn); vs_l.append(vn)
        else:
            yp, rp = _ret_prompt(hp, w_in_ret[j], w_out_ret[j])
            ys, rn = _ret_sample(hs, state_ret[j], w_in_ret[j], w_out_ret[j])
            rp_l.append(rp); rs_l.append(rn)
        xp = xp + yp.astype(xp.dtype)
        xs = xs + ys.astype(xs.dtype)
    y_prompt = _rmsnorm(xp, final_norm_g)
    y_sample = _rmsnorm(xs, final_norm_g)
    return (y_prompt, y_sample, jnp.stack(kp_l), jnp.stack(vp_l), jnp.stack(rp_l),
            jnp.stack(ks_l), jnp.stack(vs_l), jnp.stack(rs_l))
```

```python
import functools
import math

import numpy as np
import jax
import jax.numpy as jnp
from jax import lax
from jax.experimental import pallas as pl
from jax.experimental.pallas import tpu as pltpu

F32 = jnp.float32
BF16 = jnp.bfloat16

D_MODEL = 1024
PAST_LEN = 16384
ATT_HEADS = 32
ATT_KV_HEADS = 4
ATT_GROUP = ATT_HEADS // ATT_KV_HEADS
ATT_HEAD_DIM = 64
ATT_WIDTH = ATT_HEADS * ATT_HEAD_DIM
ATT_KV_WIDTH = ATT_KV_HEADS * ATT_HEAD_DIM
ATT_IN = 2 * ATT_WIDTH + 2 * ATT_KV_WIDTH
WINDOW = 128
NUM_BUCKETS = 32
MAX_DISTANCE = 128
RET_HEADS = 4
RET_QK_DIM = 256
RET_V_DIM = 512
RET_QK_WIDTH = RET_HEADS * RET_QK_DIM
RET_WIDTH = RET_HEADS * RET_V_DIM
RET_IN = 2 * RET_QK_WIDTH + 2 * RET_WIDTH
RET_CHUNK = 128
ROPE_BASE = 10000.0
EPS = 1e-6
NEG = -1e30

LANES = 128
PROJ_ROWS = 512
PROJ_COLS = 512
ATT_ROWS = 256
SAMPLE_ATT_BATCH = 8
SAMPLE_RET_BATCH = 2
VMEM_LIMIT = 56 * 1024 * 1024


def _params(n_axes):
    return pltpu.CompilerParams(
        dimension_semantics=("arbitrary",) * n_axes,
        vmem_limit_bytes=VMEM_LIMIT)


def _silu(x):
    return x * (1.0 / (1.0 + jnp.exp(-x)))


def _rmsnorm(x, g):
    return x * lax.rsqrt(jnp.mean(x * x, axis=-1, keepdims=True) + EPS) * g


def _t5_bucket_np(rel):
    n = np.maximum(rel, 0)
    max_exact = NUM_BUCKETS // 2
    nf = np.maximum(n, 1).astype(np.float64)
    large = max_exact + (np.log(nf / max_exact) / math.log(MAX_DISTANCE / max_exact)
                         * (NUM_BUCKETS - max_exact)).astype(np.int32)
    large = np.minimum(large, NUM_BUCKETS - 1)
    return np.where(n < max_exact, n, large).astype(np.int32)


def _bucket_map_np():
    i = np.arange(WINDOW)[:, None]
    j = np.arange(2 * WINDOW)[None, :]
    rel = i - j + WINDOW
    valid = (rel >= 0) & (rel < WINDOW)
    return np.where(valid, _t5_bucket_np(rel), -1).astype(np.int32)


def _gammas():
    return [1.0 - 2.0 ** (-5 - h) for h in range(RET_HEADS)]


def _retention_tables_np():
    lg = np.log1p(-np.exp2(-5.0 - np.arange(RET_HEADS, dtype=np.float64)))
    idx = np.arange(RET_CHUNK, dtype=np.float64)
    diff = idx[:, None] - idx[None, :]
    decay = np.where(diff >= 0, np.exp(np.maximum(diff, 0.0)[None] * lg[:, None, None]), 0.0)
    cross = np.exp((idx[None, :] + 1.0) * lg[:, None])
    kdec = np.exp((RET_CHUNK - 1.0 - idx)[None, :] * lg[:, None])
    rep = lambda t: np.repeat(t[:, :, None], LANES, axis=2)
    return (decay.astype(np.float32), rep(cross).astype(np.float32),
            rep(kdec).astype(np.float32), np.exp(RET_CHUNK * lg).astype(np.float32))


def _rope_tables_np(pos):
    half = RET_QK_DIM // 2
    inv = ROPE_BASE ** (-np.arange(half, dtype=np.float64) / half)
    ang = np.asarray(pos, np.float64)[:, None] * inv[None, :]
    return np.cos(ang).astype(np.float32), np.sin(ang).astype(np.float32)


def _bias_table_kernel(rb_ref, bucket_ref, out_ref):
    h = pl.program_id(0)
    bk = bucket_ref[...]
    acc = jnp.full(bk.shape, NEG, F32)
    for b in range(NUM_BUCKETS):
        acc = jnp.where(bk == b, rb_ref[b, h], acc)
    col = lax.broadcasted_iota(jnp.int32, bk.shape, 1)
    out_ref[1, 0] = acc
    out_ref[0, 0] = jnp.where(col < WINDOW, NEG, acc)


def _bias_table(rel_bias):
    bucket = jnp.asarray(_bucket_map_np())
    return pl.pallas_call(
        _bias_table_kernel,
        grid=(ATT_HEADS,),
        in_specs=[pl.BlockSpec(memory_space=pltpu.SMEM),
                  pl.BlockSpec((WINDOW, 2 * WINDOW), lambda h: (0, 0))],
        out_specs=pl.BlockSpec((2, 1, WINDOW, 2 * WINDOW), lambda h: (0, h, 0, 0)),
        out_shape=jax.ShapeDtypeStruct((2, ATT_HEADS, WINDOW, 2 * WINDOW), F32),
        compiler_params=_params(1),
        name="bias_table",
    )(rel_bias, bucket)


def _attn_inproj_kernel(x_ref, g_ref, w_ref, q_ref, kv_ref, gate_ref):
    h = _rmsnorm(x_ref[...], g_ref[...]).astype(BF16)
    for c in range(0, ATT_IN, PROJ_COLS):
        r = jnp.dot(h, w_ref[:, c:c + PROJ_COLS], preferred_element_type=F32)
        if c < ATT_WIDTH:
            q_ref[:, c:c + PROJ_COLS] = (r * (ATT_HEAD_DIM ** -0.5)).astype(q_ref.dtype)
        elif c < ATT_WIDTH + 2 * ATT_KV_WIDTH:
            kv_ref[...] = r
        else:
            o = c - ATT_WIDTH - 2 * ATT_KV_WIDTH
            gate_ref[:, o:o + PROJ_COLS] = r.astype(gate_ref.dtype)


def _attn_inproj(x, g, w, out_dtype):
    m = x.shape[0]
    tm = min(PROJ_ROWS, m)
    row = lambda i: (i, 0)
    return pl.pallas_call(
        _attn_inproj_kernel,
        grid=(m // tm,),
        in_specs=[pl.BlockSpec((tm, D_MODEL), row),
                  pl.BlockSpec((1, D_MODEL), lambda i: (0, 0)),
                  pl.BlockSpec((D_MODEL, ATT_IN), lambda i: (0, 0))],
        out_specs=[pl.BlockSpec((tm, ATT_WIDTH), row),
                   pl.BlockSpec((tm, 2 * ATT_KV_WIDTH), row),
                   pl.BlockSpec((tm, ATT_WIDTH), row)],
        out_shape=[jax.ShapeDtypeStruct((m, ATT_WIDTH), out_dtype),
                   jax.ShapeDtypeStruct((m, 2 * ATT_KV_WIDTH), F32),
                   jax.ShapeDtypeStruct((m, ATT_WIDTH), out_dtype)],
        compiler_params=_params(1),
        name="attn_inproj",
    )(x, g, w)


def _ret_inproj_kernel(x_ref, g_ref, w_ref, cos_ref, sin_ref, qk_ref, v_ref, gate_ref):
    h = _rmsnorm(x_ref[...], g_ref[...]).astype(BF16)
    cos = cos_ref[...]
    sin = sin_ref[...]
    half = RET_QK_DIM // 2
    for c in range(0, RET_IN, PROJ_COLS):
        r = jnp.dot(h, w_ref[:, c:c + PROJ_COLS], preferred_element_type=F32)
        if c < 2 * RET_QK_WIDTH:
            scale = 1.0 if c < RET_QK_WIDTH else RET_QK_DIM ** -0.5
            for o in range(0, PROJ_COLS, RET_QK_DIM):
                x1 = r[:, o:o + half]
                x2 = r[:, o + half:o + RET_QK_DIM]
                qk_ref[:, c + o:c + o + half] = ((x1 * cos - x2 * sin) * scale).astype(qk_ref.dtype)
                qk_ref[:, c + o + half:c + o + RET_QK_DIM] = (
                    (x1 * sin + x2 * cos) * scale).astype(qk_ref.dtype)
        elif c < 2 * RET_QK_WIDTH + RET_WIDTH:
            o = c - 2 * RET_QK_WIDTH
            v_ref[:, o:o + PROJ_COLS] = r.astype(v_ref.dtype)
        else:
            o = c - 2 * RET_QK_WIDTH - RET_WIDTH
            gate_ref[:, o:o + PROJ_COLS] = r.astype(gate_ref.dtype)


def _ret_inproj(x, g, w, cos, sin, out_dtype):
    m = x.shape[0]
    tm = min(PROJ_ROWS, m)
    n_pos = cos.shape[0] // tm
    row = lambda i: (i, 0)
    pos = lambda i: (i % n_pos, 0)
    half = RET_QK_DIM // 2
    return pl.pallas_call(
        _ret_inproj_kernel,
        grid=(m // tm,),
        in_specs=[pl.BlockSpec((tm, D_MODEL), row),
                  pl.BlockSpec((1, D_MODEL), lambda i: (0, 0)),
                  pl.BlockSpec((D_MODEL, RET_IN), lambda i: (0, 0)),
                  pl.BlockSpec((tm, half), pos),
                  pl.BlockSpec((tm, half), pos)],
        out_specs=[pl.BlockSpec((tm, 2 * RET_QK_WIDTH), row),
                   pl.BlockSpec((tm, RET_WIDTH), row),
                   pl.BlockSpec((tm, RET_WIDTH), row)],
        out_shape=[jax.ShapeDtypeStruct((m, 2 * RET_QK_WIDTH), out_dtype),
                   jax.ShapeDtypeStruct((m, RET_WIDTH), out_dtype),
                   jax.ShapeDtypeStruct((m, RET_WIDTH), out_dtype)],
        compiler_params=_params(1),
        name="ret_inproj",
    )(x, g, w, cos, sin)


def _outproj_kernel(o_ref, w_ref, x_ref, y_ref):
    y_ref[...] = x_ref[...] + jnp.dot(o_ref[...].astype(BF16), w_ref[...],
                                      preferred_element_type=F32)


def _outproj_norm_kernel(o_ref, w_ref, x_ref, g_ref, y_ref):
    y = x_ref[...] + jnp.dot(o_ref[...].astype(BF16), w_ref[...], preferred_element_type=F32)
    y_ref[...] = _rmsnorm(y, g_ref[...])


def _outproj(o, w, x, final_g=None):
    m, width = o.shape
    tm = min(PROJ_ROWS, m)
    row = lambda i: (i, 0)
    in_specs = [pl.BlockSpec((tm, width), row),
                pl.BlockSpec((width, D_MODEL), lambda i: (0, 0)),
                pl.BlockSpec((tm, D_MODEL), row)]
    args = [o, w, x]
    body = _outproj_kernel
    if final_g is not None:
        in_specs.append(pl.BlockSpec((1, D_MODEL), lambda i: (0, 0)))
        args.append(final_g)
        body = _outproj_norm_kernel
    return pl.pallas_call(
        body,
        grid=(m // tm,),
        in_specs=in_specs,
        out_specs=pl.BlockSpec((tm, D_MODEL), row),
        out_shape=jax.ShapeDtypeStruct((m, D_MODEL), F32),
        compiler_params=_params(1),
        name="outproj",
    )(*args)


def _pad_pair(x, x_rolled, lo, e):
    if e == 0:
        return jnp.where(lo, x, 0.0), jnp.where(lo, 0.0, x_rolled)
    return jnp.where(lo, x_rolled, 0.0), jnp.where(lo, 0.0, x)


def _swa_prompt_kernel(sink_ref, q_ref, kvc_ref, kvp_ref, gate_ref, bias_ref, out_ref):
    j = pl.program_id(1)
    half = ATT_HEAD_DIM
    pairs_per_kv = ATT_GROUP // 2
    kv_group_cols = ATT_GROUP * ATT_HEAD_DIM
    contract_last = (((1,), (1,)), ((), ()))
    lo2 = lax.broadcasted_iota(jnp.int32, (2 * WINDOW, LANES), 1) < half
    lo1 = lax.broadcasted_iota(jnp.int32, (WINDOW, LANES), 1) < half
    for blk in range(ATT_ROWS // WINDOW):
        r0 = blk * WINDOW
        rows = slice(r0, r0 + WINDOW)
        if blk == 0:
            kv_prev = kvp_ref[...]
            sel = jnp.where(j == 0, 0, 1)
        else:
            kv_prev = kvc_ref[r0 - WINDOW:r0, :]
            sel = 1
        kvcat = jnp.concatenate([kv_prev, kvc_ref[rows, :]], axis=0)
        for pair in range(ATT_KV_HEADS // 2):
            kk = kvcat[:, pair * LANES:(pair + 1) * LANES]
            vv = kvcat[:, ATT_KV_WIDTH + pair * LANES:ATT_KV_WIDTH + (pair + 1) * LANES]
            kk_r = pltpu.roll(kk, half, axis=1)
            vv_r = pltpu.roll(vv, half, axis=1)
            for e in range(2):
                kvh = 2 * pair + e
                k_a, k_b = _pad_pair(kk, kk_r, lo2, e)
                v_a, v_b = _pad_pair(vv, vv_r, lo2, e)
                c0 = kvh * kv_group_cols
                qs = jnp.concatenate(
                    [q_ref[rows, c0 + p * LANES:c0 + (p + 1) * LANES] for p in range(pairs_per_kv)],
                    axis=0)
                logits = (lax.dot_general(qs, k_a.astype(BF16), contract_last,
                                          preferred_element_type=F32),
                          lax.dot_general(qs, k_b.astype(BF16), contract_last,
                                          preferred_element_type=F32))
                probs = ([], [])
                inv = ([], [])
                for p in range(pairs_per_kv):
                    for st in range(2):
                        hd = kvh * ATT_GROUP + 2 * p + st
                        l = logits[st][p * WINDOW:(p + 1) * WINDOW, :] + bias_ref[sel, hd]
                        s = sink_ref[hd]
                        m = jnp.maximum(jnp.max(l, axis=-1, keepdims=True), s)
                        pe = jnp.exp(l - m)
                        den = jnp.sum(pe, axis=-1, keepdims=True) + jnp.exp(s - m)
                        probs[st].append(pe.astype(BF16))
                        inv[st].append(1.0 / den)
                o = (jnp.dot(jnp.concatenate(probs[0], axis=0), v_a.astype(BF16),
                             preferred_element_type=F32)
                     + jnp.dot(jnp.concatenate(probs[1], axis=0), v_b.astype(BF16),
                               preferred_element_type=F32))
                for p in range(pairs_per_kv):
                    cols = slice(c0 + p * LANES, c0 + (p + 1) * LANES)
                    scale = jnp.where(lo1, inv[0][p], inv[1][p])
                    gt = gate_ref[rows, cols].astype(F32)
                    og = o[p * WINDOW:(p + 1) * WINDOW, :] * scale * _silu(gt)
                    out_ref[rows, cols] = og.astype(out_ref.dtype)


def _swa_prompt(sinks, q, kv, gate, bias, batch, seq):
    m = batch * seq
    nt = seq // ATT_ROWS
    blocks_per_tile = ATT_ROWS // WINDOW
    blocks_per_seq = seq // WINDOW
    tile = lambda b, j: (b * nt + j, 0)
    prev = lambda b, j: (b * blocks_per_seq + jnp.maximum(j * blocks_per_tile - 1, 0), 0)
    return pl.pallas_call(
        _swa_prompt_kernel,
        grid=(batch, nt),
        in_specs=[pl.BlockSpec(memory_space=pltpu.SMEM),
                  pl.BlockSpec((ATT_ROWS, ATT_WIDTH), tile),
                  pl.BlockSpec((ATT_ROWS, 2 * ATT_KV_WIDTH), tile),
                  pl.BlockSpec((WINDOW, 2 * ATT_KV_WIDTH), prev),
                  pl.BlockSpec((ATT_ROWS, ATT_WIDTH), tile),
                  pl.BlockSpec((2, ATT_HEADS, WINDOW, 2 * WINDOW), lambda b, j: (0, 0, 0, 0))],
        out_specs=pl.BlockSpec((ATT_ROWS, ATT_WIDTH), tile),
        out_shape=jax.ShapeDtypeStruct((m, ATT_WIDTH), BF16),
        compiler_params=_params(2),
        name="swa_prompt",
    )(sinks, q, kv, kv, gate, bias)


def _swa_sample_kernel(q_ref, kv_ref, ck_ref, cv_ref, gate_ref, bias_ref, sink_ref,
                       o_ref, cko_ref, cvo_ref):
    half = ATT_HEAD_DIM
    heads_per_pair = 2 * ATT_GROUP
    contract_last = (((1,), (1,)), ((), ()))
    last_row = lax.broadcasted_iota(jnp.int32, (WINDOW, ATT_KV_WIDTH), 0) == WINDOW - 1
    row = lax.broadcasted_iota(jnp.int32, (heads_per_pair, LANES), 0)
    lane = lax.broadcasted_iota(jnp.int32, (heads_per_pair, LANES), 1)
    diag = (row < ATT_GROUP) == (lane < half)
    first_group = lax.broadcasted_iota(jnp.int32, (heads_per_pair, half), 0) < ATT_GROUP
    for i in range(SAMPLE_ATT_BATCH):
        k_new = jnp.broadcast_to(kv_ref[i:i + 1, 0:ATT_KV_WIDTH], (WINDOW, ATT_KV_WIDTH))
        v_new = jnp.broadcast_to(kv_ref[i:i + 1, ATT_KV_WIDTH:2 * ATT_KV_WIDTH],
                                 (WINDOW, ATT_KV_WIDTH))
        kb = jnp.where(last_row, k_new, pltpu.roll(ck_ref[i], WINDOW - 1, axis=0))
        vb = jnp.where(last_row, v_new, pltpu.roll(cv_ref[i], WINDOW - 1, axis=0))
        cko_ref[i] = kb
        cvo_ref[i] = vb
        qb = q_ref[i]
        gb = gate_ref[i]
        for pair in range(ATT_KV_HEADS // 2):
            hs = slice(pair * heads_per_pair, (pair + 1) * heads_per_pair)
            q16 = qb[hs, :]
            qbd = jnp.where(diag, jnp.concatenate([q16, q16], axis=1), 0.0)
            kp = kb[:, pair * LANES:(pair + 1) * LANES]
            vp = vb[:, pair * LANES:(pair + 1) * LANES]
            l = lax.dot_general(qbd, kp, contract_last, preferred_element_type=F32) + bias_ref[hs, :]
            s = sink_ref[hs, 0:1]
            m = jnp.maximum(jnp.max(l, axis=-1, keepdims=True), s)
            pe = jnp.exp(l - m)
            den = jnp.sum(pe, axis=-1, keepdims=True) + jnp.exp(s - m)
            o2 = jnp.dot(pe, vp, preferred_element_type=F32) / den
            o16 = jnp.where(first_group, o2[:, :half], o2[:, half:])
            o_ref[i, hs, :] = o16 * _silu(gb[hs, :])


def _swa_sample(q, kv, ck, cv, gate, bias_s, sinks_rep):
    n = q.shape[0]
    bb = SAMPLE_ATT_BATCH
    blk3 = lambda i: (i, 0, 0)
    return pl.pallas_call(
        _swa_sample_kernel,
        grid=(n // bb,),
        in_specs=[pl.BlockSpec((bb, ATT_HEADS, ATT_HEAD_DIM), blk3),
                  pl.BlockSpec((bb, 2 * ATT_KV_WIDTH), lambda i: (i, 0)),
                  pl.BlockSpec((bb, WINDOW, ATT_KV_WIDTH), blk3),
                  pl.BlockSpec((bb, WINDOW, ATT_KV_WIDTH), blk3),
                  pl.BlockSpec((bb, ATT_HEADS, ATT_HEAD_DIM), blk3),
                  pl.BlockSpec((ATT_HEADS, WINDOW), lambda i: (0, 0)),
                  pl.BlockSpec((ATT_HEADS, LANES), lambda i: (0, 0))],
        out_specs=[pl.BlockSpec((bb, ATT_HEADS, ATT_HEAD_DIM), blk3),
                   pl.BlockSpec((bb, WINDOW, ATT_KV_WIDTH), blk3),
                   pl.BlockSpec((bb, WINDOW, ATT_KV_WIDTH), blk3)],
        out_shape=[jax.ShapeDtypeStruct((n, ATT_HEADS, ATT_HEAD_DIM), F32),
                   jax.ShapeDtypeStruct((n, WINDOW, ATT_KV_WIDTH), F32),
                   jax.ShapeDtypeStruct((n, WINDOW, ATT_KV_WIDTH), F32)],
        compiler_params=_params(1),
        name="swa_sample",
    )(q, kv, ck, cv, gate, bias_s, sinks_rep)


def _group_norm_gate(o, gate):
    mu = jnp.mean(o, axis=-1, keepdims=True)
    d = o - mu
    var = jnp.mean(d * d, axis=-1, keepdims=True)
    return d * lax.rsqrt(var + EPS) * _silu(gate)


def _ret_prompt_kernel(g_chunk_ref, q_ref, k_ref, v_ref, gate_ref, decay_ref, cross_ref, kdec_ref,
                       og_ref, r_ref):
    c = pl.program_id(1)
    contract_last = (((1,), (1,)), ((), ()))

    @pl.when(c == 0)
    def _():
        r_ref[...] = jnp.zeros_like(r_ref)

    for h in range(RET_HEADS):
        qh = q_ref[:, h * RET_QK_DIM:(h + 1) * RET_QK_DIM]
        kh = k_ref[:, h * RET_QK_DIM:(h + 1) * RET_QK_DIM]
        vcols = slice(h * RET_V_DIM, (h + 1) * RET_V_DIM)
        vh = v_ref[:, vcols]
        state = r_ref[0, h]
        scores = lax.dot_general(qh, kh, contract_last, preferred_element_type=F32) * decay_ref[h]
        cross = jnp.concatenate([cross_ref[h]] * (RET_V_DIM // LANES), axis=1)
        o = (jnp.dot(scores.astype(BF16), vh, preferred_element_type=F32)
             + jnp.dot(qh, state.astype(BF16), preferred_element_type=F32) * cross)
        kdec = jnp.concatenate([kdec_ref[h]] * (RET_QK_DIM // LANES), axis=1)
        kd_t = (kh.astype(F32) * kdec).T.astype(BF16)
        r_ref[0, h] = g_chunk_ref[h] * state + jnp.dot(kd_t, vh, preferred_element_type=F32)
        og_ref[:, vcols] = _group_norm_gate(o, gate_ref[:, vcols].astype(F32)).astype(og_ref.dtype)


def _ret_prompt(qk, v, gate, batch, seq):
    m = batch * seq
    nch = seq // RET_CHUNK
    decay, cross, kdec, g_chunk = (jnp.asarray(t) for t in _retention_tables_np())
    table = pl.BlockSpec((RET_HEADS, RET_CHUNK, LANES), lambda b, c: (0, 0, 0))
    return pl.pallas_call(
        _ret_prompt_kernel,
        grid=(batch, nch),
        in_specs=[pl.BlockSpec(memory_space=pltpu.SMEM),
                  pl.BlockSpec((RET_CHUNK, RET_QK_WIDTH), lambda b, c: (b * nch + c, 0)),
                  pl.BlockSpec((RET_CHUNK, RET_QK_WIDTH), lambda b, c: (b * nch + c, 1)),
                  pl.BlockSpec((RET_CHUNK, RET_WIDTH), lambda b, c: (b * nch + c, 0)),
                  pl.BlockSpec((RET_CHUNK, RET_WIDTH), lambda b, c: (b * nch + c, 0)),
                  table, table, table],
        out_specs=[pl.BlockSpec((RET_CHUNK, RET_WIDTH), lambda b, c: (b * nch + c, 0)),
                   pl.BlockSpec((1, RET_HEADS, RET_QK_DIM, RET_V_DIM), lambda b, c: (b, 0, 0, 0))],
        out_shape=[jax.ShapeDtypeStruct((m, RET_WIDTH), BF16),
                   jax.ShapeDtypeStruct((batch, RET_HEADS, RET_QK_DIM, RET_V_DIM), F32)],
        compiler_params=_params(2),
        name="ret_prompt",
    )(g_chunk, qk, qk, v, gate, decay, cross, kdec)


def _ret_sample_kernel(qk_ref, v_ref, gate_ref, st_ref, og_ref, sto_ref, qkt_ref):
    step = pl.program_id(0)

    @pl.when(step == 0)
    def _():
        qkt_ref[...] = qk_ref[...].T

    lane = lax.broadcasted_iota(jnp.int32, (RET_QK_DIM, LANES), 1)
    gammas = _gammas()
    for i in range(SAMPLE_RET_BATCH):
        b = step * SAMPLE_RET_BATCH + i
        pick = lane == b
        for h in range(RET_HEADS):
            qrows = slice(h * RET_QK_DIM, (h + 1) * RET_QK_DIM)
            krows = slice(RET_QK_WIDTH + h * RET_QK_DIM, RET_QK_WIDTH + (h + 1) * RET_QK_DIM)
            qc = jnp.sum(jnp.where(pick, qkt_ref[qrows, :], 0.0), axis=1, keepdims=True)
            kc = jnp.sum(jnp.where(pick, qkt_ref[krows, :], 0.0), axis=1, keepdims=True)
            vcols = slice(h * RET_V_DIM, (h + 1) * RET_V_DIM)
            vrow = v_ref[pl.ds(b, 1), vcols]
            state = st_ref[i, h]
            o = (gammas[h] * jnp.sum(qc * state, axis=0, keepdims=True)
                 + jnp.sum(qc * kc, axis=0, keepdims=True) * vrow)
            sto_ref[i, h] = gammas[h] * state + kc * vrow
            og_ref[pl.ds(b, 1), vcols] = _group_norm_gate(o, gate_ref[pl.ds(b, 1), vcols])


def _ret_sample(qk, v, gate, state):
    n = qk.shape[0]
    bb = SAMPLE_RET_BATCH
    whole = lambda i: (0, 0)
    st_spec = pl.BlockSpec((bb, RET_HEADS, RET_QK_DIM, RET_V_DIM), lambda i: (i, 0, 0, 0))
    return pl.pallas_call(
        _ret_sample_kernel,
        grid=(n // bb,),
        in_specs=[pl.BlockSpec((n, 2 * RET_QK_WIDTH), whole),
                  pl.BlockSpec((n, RET_WIDTH), whole),
                  pl.BlockSpec((n, RET_WIDTH), whole),
                  st_spec],
        out_specs=[pl.BlockSpec((n, RET_WIDTH), whole), st_spec],
        out_shape=[jax.ShapeDtypeStruct((n, RET_WIDTH), F32),
                   jax.ShapeDtypeStruct(state.shape, F32)],
        scratch_shapes=[pltpu.VMEM((2 * RET_QK_WIDTH, n), F32)],
        compiler_params=_params(1),
        name="ret_sample",
    )(qk, v, gate, state)


def kernel(x_prompt, x_sample, cache_swa_k, cache_swa_v, state_ret, norm_g, final_norm_g, rel_bias,
           w_in_attn, attn_sinks, w_out_attn, w_in_ret, w_out_ret):
    batch, seq, _ = x_prompt.shape
    n_s = x_sample.shape[0]
    xp = x_prompt.reshape(batch * seq, D_MODEL)
    xs = x_sample.reshape(n_s, D_MODEL)
    g0 = norm_g[0].reshape(1, D_MODEL)
    g1 = norm_g[1].reshape(1, D_MODEL)
    gf = final_norm_g.reshape(1, D_MODEL)
    w_in_a = w_in_attn[0].astype(BF16)
    w_out_a = w_out_attn[0].astype(BF16)
    w_in_r = w_in_ret[0].astype(BF16)
    w_out_r = w_out_ret[0].astype(BF16)
    sinks = attn_sinks[0]

    bias = _bias_table(rel_bias)

    q, kv, gate = _attn_inproj(xp, g0, w_in_a, BF16)
    og = _swa_prompt(sinks, q, kv, gate, bias, batch, seq)
    xp1 = _outproj(og, w_out_a, xp)
    kv_last = kv.reshape(batch, seq, 2 * ATT_KV_WIDTH)[:, seq - WINDOW:]
    k_prompt = kv_last[..., :ATT_KV_WIDTH].reshape(1, batch, WINDOW, ATT_KV_HEADS, ATT_HEAD_DIM)
    v_prompt = kv_last[..., ATT_KV_WIDTH:].reshape(1, batch, WINDOW, ATT_KV_HEADS, ATT_HEAD_DIM)

    qs, kvs, gates = _attn_inproj(xs, g0, w_in_a, F32)
    bias_s = bias[1, :, WINDOW - 1, WINDOW:]
    sinks_rep = jnp.broadcast_to(sinks[:, None], (ATT_HEADS, LANES))
    ogs, ck_new, cv_new = _swa_sample(
        qs.reshape(n_s, ATT_HEADS, ATT_HEAD_DIM), kvs,
        cache_swa_k[0].reshape(n_s, WINDOW, ATT_KV_WIDTH),
        cache_swa_v[0].reshape(n_s, WINDOW, ATT_KV_WIDTH),
        gates.reshape(n_s, ATT_HEADS, ATT_HEAD_DIM), bias_s, sinks_rep)
    xs1 = _outproj(ogs.reshape(n_s, ATT_WIDTH), w_out_a, xs)
    k_sample = ck_new.reshape(1, n_s, WINDOW, ATT_KV_HEADS, ATT_HEAD_DIM)
    v_sample = cv_new.reshape(1, n_s, WINDOW, ATT_KV_HEADS, ATT_HEAD_DIM)

    cos_p, sin_p = _rope_tables_np(np.arange(seq))
    qk, v, gate = _ret_inproj(xp1, g1, w_in_r, jnp.asarray(cos_p), jnp.asarray(sin_p), BF16)
    og, r_prompt = _ret_prompt(qk, v, gate, batch, seq)
    y_prompt = _outproj(og, w_out_r, xp1, gf).reshape(batch, seq, D_MODEL)

    cos_s, sin_s = _rope_tables_np(np.full((n_s,), PAST_LEN))
    qk_s, v_s, gate_s = _ret_inproj(xs1, g1, w_in_r, jnp.asarray(cos_s), jnp.asarray(sin_s), F32)
    og_s, r_sample = _ret_sample(qk_s, v_s, gate_s, state_ret[0])
    y_sample = _outproj(og_s, w_out_r, xs1, gf).reshape(n_s, 1, D_MODEL)

    return (y_prompt, y_sample, k_prompt, v_prompt, r_prompt[None],
            k_sample, v_sample, r_sample[None])
```

```python
import functools
import math

import numpy as np
import jax
import jax.numpy as jnp
from jax import lax
from jax.experimental import pallas as pl
from jax.experimental.pallas import tpu as pltpu

F32 = jnp.float32
BF16 = jnp.bfloat16

D_MODEL = 1024
PAST_LEN = 16384
ATT_HEADS = 32
ATT_KV_HEADS = 4
ATT_GROUP = ATT_HEADS // ATT_KV_HEADS
ATT_HEAD_DIM = 64
ATT_WIDTH = ATT_HEADS * ATT_HEAD_DIM
ATT_KV_WIDTH = ATT_KV_HEADS * ATT_HEAD_DIM
ATT_IN = 2 * ATT_WIDTH + 2 * ATT_KV_WIDTH
WINDOW = 128
NUM_BUCKETS = 32
MAX_DISTANCE = 128
RET_HEADS = 4
RET_QK_DIM = 256
RET_V_DIM = 512
RET_QK_WIDTH = RET_HEADS * RET_QK_DIM
RET_WIDTH = RET_HEADS * RET_V_DIM
RET_IN = 2 * RET_QK_WIDTH + 2 * RET_WIDTH
ROPE_BASE = 10000.0
EPS = 1e-6
NEG = -1e30
LOG2E = math.log2(math.e)

LANES = 128
PROJ_ROWS = 512
PROJ_COLS = 512
ATT_ROWS = 512
RET_BLOCK = 256
SAMPLE_ATT_BATCH = 8
SAMPLE_RET_BATCH = 2
VMEM_LIMIT = 56 * 1024 * 1024


def _params(n_axes):
    return pltpu.CompilerParams(
        dimension_semantics=("arbitrary",) * n_axes,
        vmem_limit_bytes=VMEM_LIMIT)


def _silu(x):
    hx = 0.5 * x
    return hx + hx * jnp.tanh(hx)


def _rmsnorm(x, g):
    return x * lax.rsqrt(jnp.mean(x * x, axis=-1, keepdims=True) + EPS) * g


def _t5_bucket_np(rel):
    n = np.maximum(rel, 0)
    max_exact = NUM_BUCKETS // 2
    nf = np.maximum(n, 1).astype(np.float64)
    large = max_exact + (np.log(nf / max_exact) / math.log(MAX_DISTANCE / max_exact)
                         * (NUM_BUCKETS - max_exact)).astype(np.int32)
    large = np.minimum(large, NUM_BUCKETS - 1)
    return np.where(n < max_exact, n, large).astype(np.int32)


def _bucket_map_np():
    i = np.arange(WINDOW)[:, None]
    j = np.arange(WINDOW)[None, :]
    rel = np.where(j <= i, i - j, i - j + WINDOW)
    return _t5_bucket_np(rel).astype(np.int32)


def _gammas():
    return [1.0 - 2.0 ** (-5 - h) for h in range(RET_HEADS)]


def _decay_scales_np(rows, block):
    lg = np.log1p(-np.exp2(-5.0 - np.arange(RET_HEADS, dtype=np.float64)))
    i1 = (np.arange(rows) % block + 1.0)[None, :] * lg[:, None]
    q_scale = np.repeat(np.exp(i1)[:, :, None], LANES, axis=2)
    k_scale = np.repeat((np.exp(-i1) * RET_QK_DIM ** -0.5)[:, None, :], LANES, axis=1)
    return (q_scale.astype(np.float32), k_scale.astype(np.float32),
            np.exp(block * lg).astype(np.float32))


def _rope_tables_np(pos):
    half = RET_QK_DIM // 2
    inv = ROPE_BASE ** (-np.arange(half, dtype=np.float64) / half)
    ang = np.asarray(pos, np.float64)[:, None] * inv[None, :]
    return np.cos(ang).astype(np.float32), np.sin(ang).astype(np.float32)


def _bias_table_kernel(rb_ref, bucket_ref, out_ref):
    h = pl.program_id(0)
    bk = bucket_ref[...]
    acc = jnp.zeros(bk.shape, F32)
    for b in range(NUM_BUCKETS):
        acc = jnp.where(bk == b, rb_ref[b, h] * LOG2E, acc)
    row = lax.broadcasted_iota(jnp.int32, bk.shape, 0)
    col = lax.broadcasted_iota(jnp.int32, bk.shape, 1)
    out_ref[1, 0] = acc
    out_ref[0, 0] = jnp.where(col > row, NEG, acc)


def _bias_table(rel_bias):
    bucket = jnp.asarray(_bucket_map_np())
    return pl.pallas_call(
        _bias_table_kernel,
        grid=(ATT_HEADS,),
        in_specs=[pl.BlockSpec(memory_space=pltpu.SMEM),
                  pl.BlockSpec((WINDOW, WINDOW), lambda h: (0, 0))],
        out_specs=pl.BlockSpec((2, 1, WINDOW, WINDOW), lambda h: (0, h, 0, 0)),
        out_shape=jax.ShapeDtypeStruct((2, ATT_HEADS, WINDOW, WINDOW), F32),
        compiler_params=_params(1),
        name="bias_table",
    )(rel_bias, bucket)


def _attn_inproj_kernel(x_ref, g_ref, w_ref, q_ref, kv_ref, gate_ref):
    h = _rmsnorm(x_ref[...], g_ref[...]).astype(BF16)
    for c in range(0, ATT_IN, PROJ_COLS):
        r = jnp.dot(h, w_ref[:, c:c + PROJ_COLS], preferred_element_type=F32)
        if c < ATT_WIDTH:
            q_ref[:, c:c + PROJ_COLS] = (r * (ATT_HEAD_DIM ** -0.5 * LOG2E)).astype(q_ref.dtype)
        elif c < ATT_WIDTH + 2 * ATT_KV_WIDTH:
            kv_ref[...] = r
        else:
            o = c - ATT_WIDTH - 2 * ATT_KV_WIDTH
            gate_ref[:, o:o + PROJ_COLS] = r.astype(gate_ref.dtype)


def _attn_inproj(x, g, w, out_dtype):
    m = x.shape[0]
    tm = min(PROJ_ROWS, m)
    row = lambda i: (i, 0)
    return pl.pallas_call(
        _attn_inproj_kernel,
        grid=(m // tm,),
        in_specs=[pl.BlockSpec((tm, D_MODEL), row),
                  pl.BlockSpec((1, D_MODEL), lambda i: (0, 0)),
                  pl.BlockSpec((D_MODEL, ATT_IN), lambda i: (0, 0))],
        out_specs=[pl.BlockSpec((tm, ATT_WIDTH), row),
                   pl.BlockSpec((tm, 2 * ATT_KV_WIDTH), row),
                   pl.BlockSpec((tm, ATT_WIDTH), row)],
        out_shape=[jax.ShapeDtypeStruct((m, ATT_WIDTH), out_dtype),
                   jax.ShapeDtypeStruct((m, 2 * ATT_KV_WIDTH), F32),
                   jax.ShapeDtypeStruct((m, ATT_WIDTH), out_dtype)],
        compiler_params=_params(1),
        name="attn_inproj",
    )(x, g, w)


def _ret_inproj_kernel(x_ref, g_ref, w_ref, wkt_ref, cos_ref, sin_ref, cost_ref, sint_ref,
                       qsc_ref, ksc_ref, q_ref, kt_ref, v_ref, gate_ref):
    h = _rmsnorm(x_ref[...], g_ref[...]).astype(BF16)
    half = RET_QK_DIM // 2
    contract_last = (((1,), (1,)), ((), ()))
    cos = cos_ref[...]
    sin = sin_ref[...]
    for c in range(0, RET_QK_WIDTH, PROJ_COLS):
        r = jnp.dot(h, w_ref[:, c:c + PROJ_COLS], preferred_element_type=F32)
        for o in range(0, PROJ_COLS, RET_QK_DIM):
            scale = qsc_ref[(c + o) // RET_QK_DIM]
            x1 = r[:, o:o + half]
            x2 = r[:, o + half:o + RET_QK_DIM]
            q_ref[:, c + o:c + o + half] = ((x1 * cos - x2 * sin) * scale).astype(q_ref.dtype)
            q_ref[:, c + o + half:c + o + RET_QK_DIM] = (
                (x1 * sin + x2 * cos) * scale).astype(q_ref.dtype)
    cos = cost_ref[...]
    sin = sint_ref[...]
    for c in range(0, RET_QK_WIDTH, PROJ_COLS):
        r = lax.dot_general(wkt_ref[c:c + PROJ_COLS, :], h, contract_last,
                            preferred_element_type=F32)
        for o in range(0, PROJ_COLS, RET_QK_DIM):
            scale = ksc_ref[(c + o) // RET_QK_DIM]
            x1 = r[o:o + half, :]
            x2 = r[o + half:o + RET_QK_DIM, :]
            kt_ref[c + o:c + o + half, :] = ((x1 * cos - x2 * sin) * scale).astype(kt_ref.dtype)
            kt_ref[c + o + half:c + o + RET_QK_DIM, :] = (
                (x1 * sin + x2 * cos) * scale).astype(kt_ref.dtype)
    for c in range(0, RET_WIDTH, PROJ_COLS):
        v_ref[:, c:c + PROJ_COLS] = jnp.dot(
            h, w_ref[:, 2 * RET_QK_WIDTH + c:2 * RET_QK_WIDTH + c + PROJ_COLS],
            preferred_element_type=F32).astype(v_ref.dtype)
    for c in range(0, RET_WIDTH, PROJ_COLS):
        w0 = 2 * RET_QK_WIDTH + RET_WIDTH + c
        gate_ref[:, c:c + PROJ_COLS] = jnp.dot(
            h, w_ref[:, w0:w0 + PROJ_COLS], preferred_element_type=F32).astype(gate_ref.dtype)


def _ret_inproj(x, g, w, wkt, pos, block, out_dtype):
    m = x.shape[0]
    tm = min(PROJ_ROWS, m)
    n_pos = len(pos) // tm
    half = RET_QK_DIM // 2
    cos, sin = _rope_tables_np(pos)
    q_scale, k_scale, _ = _decay_scales_np(tm, block)
    row = lambda i: (i, 0)
    const2 = lambda i: (0, 0)
    const3 = lambda i: (0, 0, 0)
    return pl.pallas_call(
        _ret_inproj_kernel,
        grid=(m // tm,),
        in_specs=[pl.BlockSpec((tm, D_MODEL), row),
                  pl.BlockSpec((1, D_MODEL), const2),
                  pl.BlockSpec((D_MODEL, RET_IN), const2),
                  pl.BlockSpec((RET_QK_WIDTH, D_MODEL), const2),
                  pl.BlockSpec((tm, half), lambda i: (i % n_pos, 0)),
                  pl.BlockSpec((tm, half), lambda i: (i % n_pos, 0)),
                  pl.BlockSpec((half, tm), lambda i: (0, i % n_pos)),
                  pl.BlockSpec((half, tm), lambda i: (0, i % n_pos)),
                  pl.BlockSpec((RET_HEADS, tm, LANES), const3),
                  pl.BlockSpec((RET_HEADS, LANES, tm), const3)],
        out_specs=[pl.BlockSpec((tm, RET_QK_WIDTH), row),
                   pl.BlockSpec((RET_QK_WIDTH, tm), lambda i: (0, i)),
                   pl.BlockSpec((tm, RET_WIDTH), row),
                   pl.BlockSpec((tm, RET_WIDTH), row)],
        out_shape=[jax.ShapeDtypeStruct((m, RET_QK_WIDTH), out_dtype),
                   jax.ShapeDtypeStruct((RET_QK_WIDTH, m), out_dtype),
                   jax.ShapeDtypeStruct((m, RET_WIDTH), out_dtype),
                   jax.ShapeDtypeStruct((m, RET_WIDTH), out_dtype)],
        compiler_params=_params(1),
        name="ret_inproj",
    )(x, g, w, wkt, jnp.asarray(cos), jnp.asarray(sin), jnp.asarray(cos.T), jnp.asarray(sin.T),
      jnp.asarray(q_scale), jnp.asarray(k_scale))


def _outproj_kernel(o_ref, w_ref, x_ref, y_ref):
    y_ref[...] = x_ref[...] + jnp.dot(o_ref[...].astype(BF16), w_ref[...],
                                      preferred_element_type=F32)


def _outproj_norm_kernel(o_ref, w_ref, x_ref, g_ref, y_ref):
    y = x_ref[...] + jnp.dot(o_ref[...].astype(BF16), w_ref[...], preferred_element_type=F32)
    y_ref[...] = _rmsnorm(y, g_ref[...])


def _outproj(o, w, x, final_g=None):
    m, width = o.shape
    tm = min(PROJ_ROWS, m)
    row = lambda i: (i, 0)
    in_specs = [pl.BlockSpec((tm, width), row),
                pl.BlockSpec((width, D_MODEL), lambda i: (0, 0)),
                pl.BlockSpec((tm, D_MODEL), row)]
    args = [o, w, x]
    body = _outproj_kernel
    if final_g is not None:
        in_specs.append(pl.BlockSpec((1, D_MODEL), lambda i: (0, 0)))
        args.append(final_g)
        body = _outproj_norm_kernel
    return pl.pallas_call(
        body,
        grid=(m // tm,),
        in_specs=in_specs,
        out_specs=pl.BlockSpec((tm, D_MODEL), row),
        out_shape=jax.ShapeDtypeStruct((m, D_MODEL), F32),
        compiler_params=_params(1),
        name="outproj",
    )(*args)


def _pad_pair(x, x_rolled, lo, e):
    if e == 0:
        return jnp.where(lo, x, 0.0), jnp.where(lo, 0.0, x_rolled)
    return jnp.where(lo, x_rolled, 0.0), jnp.where(lo, 0.0, x)


def _swa_prompt_kernel(sink_ref, q_ref, kvc_ref, kvp_ref, gate_ref, bias_ref, out_ref):
    j = pl.program_id(1)
    half = ATT_HEAD_DIM
    pairs_per_kv = ATT_GROUP // 2
    kv_group_cols = ATT_GROUP * ATT_HEAD_DIM
    contract_last = (((1,), (1,)), ((), ()))
    lo2 = lax.broadcasted_iota(jnp.int32, (2 * WINDOW, LANES), 1) < half
    lo1 = lax.broadcasted_iota(jnp.int32, (WINDOW, LANES), 1) < half
    own = (lax.broadcasted_iota(jnp.int32, (WINDOW, WINDOW), 1)
           <= lax.broadcasted_iota(jnp.int32, (WINDOW, WINDOW), 0))
    for blk in range(ATT_ROWS // WINDOW):
        r0 = blk * WINDOW
        rows = slice(r0, r0 + WINDOW)
        if blk == 0:
            kv_prev = kvp_ref[...]
            sel = jnp.where(j == 0, 0, 1)
        else:
            kv_prev = kvc_ref[r0 - WINDOW:r0, :]
            sel = 1
        kvcat = jnp.concatenate([kv_prev, kvc_ref[rows, :]], axis=0)
        for pair in range(ATT_KV_HEADS // 2):
            kk = kvcat[:, pair * LANES:(pair + 1) * LANES]
            vv = kvcat[:, ATT_KV_WIDTH + pair * LANES:ATT_KV_WIDTH + (pair + 1) * LANES]
            kk_r = pltpu.roll(kk, half, axis=1)
            vv_r = pltpu.roll(vv, half, axis=1)
            for e in range(2):
                kvh = 2 * pair + e
                k_a, k_b = _pad_pair(kk, kk_r, lo2, e)
                v_a, v_b = _pad_pair(vv, vv_r, lo2, e)
                c0 = kvh * kv_group_cols
                qs = jnp.concatenate(
                    [q_ref[rows, c0 + p * LANES:c0 + (p + 1) * LANES] for p in range(pairs_per_kv)],
                    axis=0)
                logits = (lax.dot_general(qs, k_a.astype(BF16), contract_last,
                                          preferred_element_type=F32),
                          lax.dot_general(qs, k_b.astype(BF16), contract_last,
                                          preferred_element_type=F32))
                probs = ([], [])
                inv = ([], [])
                for p in range(pairs_per_kv):
                    for st in range(2):
                        hd = kvh * ATT_GROUP + 2 * p + st
                        lg = logits[st][p * WINDOW:(p + 1) * WINDOW, :]
                        l = jnp.where(own, lg[:, WINDOW:], lg[:, :WINDOW]) + bias_ref[sel, hd]
                        s = sink_ref[hd]
                        m = jnp.maximum(jnp.max(l, axis=-1, keepdims=True), s)
                        pe = jnp.exp2(l - m)
                        den = jnp.sum(pe, axis=-1, keepdims=True) + jnp.exp2(s - m)
                        probs[st].append(jnp.concatenate(
                            [jnp.where(own, 0.0, pe).astype(BF16),
                             jnp.where(own, pe, 0.0).astype(BF16)], axis=1))
                        inv[st].append(1.0 / den)
                o = (jnp.dot(jnp.concatenate(probs[0], axis=0), v_a.astype(BF16),
                             preferred_element_type=F32)
                     + jnp.dot(jnp.concatenate(probs[1], axis=0), v_b.astype(BF16),
                               preferred_element_type=F32))
                for p in range(pairs_per_kv):
                    cols = slice(c0 + p * LANES, c0 + (p + 1) * LANES)
                    scale = jnp.where(lo1, inv[0][p], inv[1][p])
                    gt = gate_ref[rows, cols].astype(F32)
                    og = o[p * WINDOW:(p + 1) * WINDOW, :] * scale * _silu(gt)
                    out_ref[rows, cols] = og.astype(out_ref.dtype)


def _swa_prompt(sinks, q, kv, gate, bias, batch, seq):
    m = batch * seq
    nt = seq // ATT_ROWS
    blocks_per_tile = ATT_ROWS // WINDOW
    blocks_per_seq = seq // WINDOW
    tile = lambda b, j: (b * nt + j, 0)
    prev = lambda b, j: (b * blocks_per_seq + jnp.maximum(j * blocks_per_tile - 1, 0), 0)
    return pl.pallas_call(
        _swa_prompt_kernel,
        grid=(batch, nt),
        in_specs=[pl.BlockSpec(memory_space=pltpu.SMEM),
                  pl.BlockSpec((ATT_ROWS, ATT_WIDTH), tile),
                  pl.BlockSpec((ATT_ROWS, 2 * ATT_KV_WIDTH), tile),
                  pl.BlockSpec((WINDOW, 2 * ATT_KV_WIDTH), prev),
                  pl.BlockSpec((ATT_ROWS, ATT_WIDTH), tile),
                  pl.BlockSpec((2, ATT_HEADS, WINDOW, WINDOW), lambda b, j: (0, 0, 0, 0))],
        out_specs=pl.BlockSpec((ATT_ROWS, ATT_WIDTH), tile),
        out_shape=jax.ShapeDtypeStruct((m, ATT_WIDTH), BF16),
        compiler_params=_params(2),
        name="swa_prompt",
    )(sinks, q, kv, kv, gate, bias)


def _swa_sample_kernel(q_ref, kv_ref, ck_ref, cv_ref, gate_ref, bias_ref, sink_ref,
                       o_ref, cko_ref, cvo_ref):
    half = ATT_HEAD_DIM
    heads_per_pair = 2 * ATT_GROUP
    contract_last = (((1,), (1,)), ((), ()))
    last_row = lax.broadcasted_iota(jnp.int32, (WINDOW, ATT_KV_WIDTH), 0) == WINDOW - 1
    row = lax.broadcasted_iota(jnp.int32, (heads_per_pair, LANES), 0)
    lane = lax.broadcasted_iota(jnp.int32, (heads_per_pair, LANES), 1)
    diag = (row < ATT_GROUP) == (lane < half)
    first_group = lax.broadcasted_iota(jnp.int32, (heads_per_pair, half), 0) < ATT_GROUP
    for i in range(SAMPLE_ATT_BATCH):
        k_new = jnp.broadcast_to(kv_ref[i:i + 1, 0:ATT_KV_WIDTH], (WINDOW, ATT_KV_WIDTH))
        v_new = jnp.broadcast_to(kv_ref[i:i + 1, ATT_KV_WIDTH:2 * ATT_KV_WIDTH],
                                 (WINDOW, ATT_KV_WIDTH))
        kb = jnp.where(last_row, k_new, pltpu.roll(ck_ref[i], WINDOW - 1, axis=0))
        vb = jnp.where(last_row, v_new, pltpu.roll(cv_ref[i], WINDOW - 1, axis=0))
        cko_ref[i] = kb
        cvo_ref[i] = vb
        qb = q_ref[i]
        gb = gate_ref[i]
        for pair in range(ATT_KV_HEADS // 2):
            hs = slice(pair * heads_per_pair, (pair + 1) * heads_per_pair)
            q16 = qb[hs, :]
            qbd = jnp.where(diag, jnp.concatenate([q16, q16], axis=1), 0.0)
            kp = kb[:, pair * LANES:(pair + 1) * LANES]
            vp = vb[:, pair * LANES:(pair + 1) * LANES]
            l = lax.dot_general(qbd, kp, contract_last, preferred_element_type=F32) + bias_ref[hs, :]
            s = sink_ref[hs, 0:1]
            m = jnp.maximum(jnp.max(l, axis=-1, keepdims=True), s)
            pe = jnp.exp2(l - m)
            den = jnp.sum(pe, axis=-1, keepdims=True) + jnp.exp2(s - m)
            o2 = jnp.dot(pe, vp, preferred_element_type=F32) / den
            o16 = jnp.where(first_group, o2[:, :half], o2[:, half:])
            o_ref[i, hs, :] = o16 * _silu(gb[hs, :])


def _swa_sample(q, kv, ck, cv, gate, bias_s, sinks_rep):
    n = q.shape[0]
    bb = SAMPLE_ATT_BATCH
    blk3 = lambda i: (i, 0, 0)
    return pl.pallas_call(
        _swa_sample_kernel,
        grid=(n // bb,),
        in_specs=[pl.BlockSpec((bb, ATT_HEADS, ATT_HEAD_DIM), blk3),
                  pl.BlockSpec((bb, 2 * ATT_KV_WIDTH), lambda i: (i, 0)),
                  pl.BlockSpec((bb, WINDOW, ATT_KV_WIDTH), blk3),
                  pl.BlockSpec((bb, WINDOW, ATT_KV_WIDTH), blk3),
                  pl.BlockSpec((bb, ATT_HEADS, ATT_HEAD_DIM), blk3),
                  pl.BlockSpec((ATT_HEADS, WINDOW), lambda i: (0, 0)),
                  pl.BlockSpec((ATT_HEADS, LANES), lambda i: (0, 0))],
        out_specs=[pl.BlockSpec((bb, ATT_HEADS, ATT_HEAD_DIM), blk3),
                   pl.BlockSpec((bb, WINDOW, ATT_KV_WIDTH), blk3),
                   pl.BlockSpec((bb, WINDOW, ATT_KV_WIDTH), blk3)],
        out_shape=[jax.ShapeDtypeStruct((n, ATT_HEADS, ATT_HEAD_DIM), F32),
                   jax.ShapeDtypeStruct((n, WINDOW, ATT_KV_WIDTH), F32),
                   jax.ShapeDtypeStruct((n, WINDOW, ATT_KV_WIDTH), F32)],
        compiler_params=_params(1),
        name="swa_sample",
    )(q, kv, ck, cv, gate, bias_s, sinks_rep)


def _group_norm_gate(o, gate):
    mu = jnp.mean(o, axis=-1, keepdims=True)
    d = o - mu
    var = jnp.mean(d * d, axis=-1, keepdims=True)
    return d * lax.rsqrt(var + EPS) * _silu(gate)


def _ret_prompt_kernel(g_block_ref, q_ref, kt_ref, v_ref, gate_ref, og_ref, r_ref):
    c = pl.program_id(1)

    @pl.when(c == 0)
    def _():
        r_ref[...] = jnp.zeros_like(r_ref)

    causal = (lax.broadcasted_iota(jnp.int32, (RET_BLOCK, RET_BLOCK), 1)
              <= lax.broadcasted_iota(jnp.int32, (RET_BLOCK, RET_BLOCK), 0))
    for h in range(RET_HEADS):
        qh = q_ref[:, h * RET_QK_DIM:(h + 1) * RET_QK_DIM]
        kt = kt_ref[h * RET_QK_DIM:(h + 1) * RET_QK_DIM, :]
        vcols = slice(h * RET_V_DIM, (h + 1) * RET_V_DIM)
        vh = v_ref[:, vcols]
        state = r_ref[0, h]
        scores = jnp.where(causal, jnp.dot(qh, kt, preferred_element_type=F32), 0.0)
        o = (jnp.dot(scores.astype(BF16), vh, preferred_element_type=F32)
             + jnp.dot(qh, state.astype(BF16), preferred_element_type=F32))
        r_ref[0, h] = g_block_ref[h] * (state + jnp.dot(kt, vh, preferred_element_type=F32))
        og_ref[:, vcols] = _group_norm_gate(o, gate_ref[:, vcols].astype(F32)).astype(og_ref.dtype)


def _ret_prompt(q, kt, v, gate, batch, seq):
    m = batch * seq
    nb = seq // RET_BLOCK
    g_block = jnp.asarray(_decay_scales_np(RET_BLOCK, RET_BLOCK)[2])
    rows = lambda b, c: (b * nb + c, 0)
    return pl.pallas_call(
        _ret_prompt_kernel,
        grid=(batch, nb),
        in_specs=[pl.BlockSpec(memory_space=pltpu.SMEM),
                  pl.BlockSpec((RET_BLOCK, RET_QK_WIDTH), rows),
                  pl.BlockSpec((RET_QK_WIDTH, RET_BLOCK), lambda b, c: (0, b * nb + c)),
                  pl.BlockSpec((RET_BLOCK, RET_WIDTH), rows),
                  pl.BlockSpec((RET_BLOCK, RET_WIDTH), rows)],
        out_specs=[pl.BlockSpec((RET_BLOCK, RET_WIDTH), rows),
                   pl.BlockSpec((1, RET_HEADS, RET_QK_DIM, RET_V_DIM), lambda b, c: (b, 0, 0, 0))],
        out_shape=[jax.ShapeDtypeStruct((m, RET_WIDTH), BF16),
                   jax.ShapeDtypeStruct((batch, RET_HEADS, RET_QK_DIM, RET_V_DIM), F32)],
        compiler_params=_params(2),
        name="ret_prompt",
    )(g_block, q, kt, v, gate)


def _ret_sample_kernel(q_ref, kt_ref, v_ref, gate_ref, st_ref, og_ref, sto_ref, qt_ref):
    step = pl.program_id(0)

    @pl.when(step == 0)
    def _():
        qt_ref[...] = q_ref[...].T

    lane = lax.broadcasted_iota(jnp.int32, (RET_QK_DIM, LANES), 1)
    gammas = _gammas()
    for i in range(SAMPLE_RET_BATCH):
        b = step * SAMPLE_RET_BATCH + i
        pick = lane == b
        for h in range(RET_HEADS):
            rows = slice(h * RET_QK_DIM, (h + 1) * RET_QK_DIM)
            qc = jnp.sum(jnp.where(pick, qt_ref[rows, :], 0.0), axis=1, keepdims=True)
            kc = jnp.sum(jnp.where(pick, kt_ref[rows, :], 0.0), axis=1, keepdims=True)
            vcols = slice(h * RET_V_DIM, (h + 1) * RET_V_DIM)
            vrow = v_ref[pl.ds(b, 1), vcols]
            state = st_ref[i, h]
            o = (jnp.sum(qc * state, axis=0, keepdims=True)
                 + jnp.sum(qc * kc, axis=0, keepdims=True) * vrow)
            sto_ref[i, h] = gammas[h] * (state + kc * vrow)
            og_ref[pl.ds(b, 1), vcols] = _group_norm_gate(o, gate_ref[pl.ds(b, 1), vcols])


def _ret_sample(q, kt, v, gate, state):
    n = q.shape[0]
    bb = SAMPLE_RET_BATCH
    whole = lambda i: (0, 0)
    st_spec = pl.BlockSpec((bb, RET_HEADS, RET_QK_DIM, RET_V_DIM), lambda i: (i, 0, 0, 0))
    return pl.pallas_call(
        _ret_sample_kernel,
        grid=(n // bb,),
        in_specs=[pl.BlockSpec((n, RET_QK_WIDTH), whole),
                  pl.BlockSpec((RET_QK_WIDTH, n), whole),
                  pl.BlockSpec((n, RET_WIDTH), whole),
                  pl.BlockSpec((n, RET_WIDTH), whole),
                  st_spec],
        out_specs=[pl.BlockSpec((n, RET_WIDTH), whole), st_spec],
        out_shape=[jax.ShapeDtypeStruct((n, RET_WIDTH), F32),
                   jax.ShapeDtypeStruct(state.shape, F32)],
        scratch_shapes=[pltpu.VMEM((RET_QK_WIDTH, n), F32)],
        compiler_params=_params(1),
        name="ret_sample",
    )(q, kt, v, gate, state)


def kernel(x_prompt, x_sample, cache_swa_k, cache_swa_v, state_ret, norm_g, final_norm_g, rel_bias,
           w_in_attn, attn_sinks, w_out_attn, w_in_ret, w_out_ret):
    batch, seq, _ = x_prompt.shape
    n_s = x_sample.shape[0]
    xp = x_prompt.reshape(batch * seq, D_MODEL)
    xs = x_sample.reshape(n_s, D_MODEL)
    g0 = norm_g[0].reshape(1, D_MODEL)
    g1 = norm_g[1].reshape(1, D_MODEL)
    gf = final_norm_g.reshape(1, D_MODEL)
    w_in_a = w_in_attn[0].astype(BF16)
    w_out_a = w_out_attn[0].astype(BF16)
    w_in_r = w_in_ret[0].astype(BF16)
    w_out_r = w_out_ret[0].astype(BF16)
    sinks = attn_sinks[0] * LOG2E

    bias = _bias_table(rel_bias)

    q, kv, gate = _attn_inproj(xp, g0, w_in_a, BF16)
    og = _swa_prompt(sinks, q, kv, gate, bias, batch, seq)
    xp1 = _outproj(og, w_out_a, xp)
    kv_last = kv.reshape(batch, seq, 2 * ATT_KV_WIDTH)[:, seq - WINDOW:]
    k_prompt = kv_last[..., :ATT_KV_WIDTH].reshape(1, batch, WINDOW, ATT_KV_HEADS, ATT_HEAD_DIM)
    v_prompt = kv_last[..., ATT_KV_WIDTH:].reshape(1, batch, WINDOW, ATT_KV_HEADS, ATT_HEAD_DIM)

    qs, kvs, gates = _attn_inproj(xs, g0, w_in_a, F32)
    bias_s = bias[1, :, WINDOW - 1, :]
    sinks_rep = jnp.broadcast_to(sinks[:, None], (ATT_HEADS, LANES))
    ogs, ck_new, cv_new = _swa_sample(
        qs.reshape(n_s, ATT_HEADS, ATT_HEAD_DIM), kvs,
        cache_swa_k[0].reshape(n_s, WINDOW, ATT_KV_WIDTH),
        cache_swa_v[0].reshape(n_s, WINDOW, ATT_KV_WIDTH),
        gates.reshape(n_s, ATT_HEADS, ATT_HEAD_DIM), bias_s, sinks_rep)
    xs1 = _outproj(ogs.reshape(n_s, ATT_WIDTH), w_out_a, xs)
    k_sample = ck_new.reshape(1, n_s, WINDOW, ATT_KV_HEADS, ATT_HEAD_DIM)
    v_sample = cv_new.reshape(1, n_s, WINDOW, ATT_KV_HEADS, ATT_HEAD_DIM)

    w_kt = w_in_ret[0][:, RET_QK_WIDTH:2 * RET_QK_WIDTH].T.astype(BF16)
    q, kt, v, gate = _ret_inproj(xp1, g1, w_in_r, w_kt, np.arange(seq), RET_BLOCK, BF16)
    og, r_prompt = _ret_prompt(q, kt, v, gate, batch, seq)
    y_prompt = _outproj(og, w_out_r, xp1, gf).reshape(batch, seq, D_MODEL)

    q_s, kt_s, v_s, gate_s = _ret_inproj(xs1, g1, w_in_r, w_kt, np.full((n_s,), PAST_LEN), 1, F32)
    og_s, r_sample = _ret_sample(q_s, kt_s, v_s, gate_s, state_ret[0])
    y_sample = _outproj(og_s, w_out_r, xs1, gf).reshape(n_s, 1, D_MODEL)

    return (y_prompt, y_sample, k_prompt, v_prompt, r_prompt[None],
            k_sample, v_sample, r_sample[None])
```

```python
import functools
import math

import numpy as np
import jax
import jax.numpy as jnp
from jax import lax
from jax.experimental import pallas as pl
from jax.experimental.pallas import tpu as pltpu

F32 = jnp.float32
BF16 = jnp.bfloat16

D_MODEL = 1024
PAST_LEN = 16384
ATT_HEADS = 32
ATT_KV_HEADS = 4
ATT_GROUP = ATT_HEADS // ATT_KV_HEADS
ATT_HEAD_DIM = 64
ATT_WIDTH = ATT_HEADS * ATT_HEAD_DIM
ATT_KV_WIDTH = ATT_KV_HEADS * ATT_HEAD_DIM
ATT_IN = 2 * ATT_WIDTH + 2 * ATT_KV_WIDTH
WINDOW = 128
NUM_BUCKETS = 32
MAX_DISTANCE = 128
RET_HEADS = 4
RET_QK_DIM = 256
RET_V_DIM = 512
RET_QK_WIDTH = RET_HEADS * RET_QK_DIM
RET_WIDTH = RET_HEADS * RET_V_DIM
RET_IN = 2 * RET_QK_WIDTH + 2 * RET_WIDTH
ROPE_BASE = 10000.0
EPS = 1e-6
NEG = -1e30
LOG2E = math.log2(math.e)

LANES = 128
PROJ_ROWS = 512
PROJ_COLS = 512
OUT_ROWS = 1024
ATT_ROWS = 512
RET_BLOCK = 256
RET_ROWS = 512
SAMPLE_ATT_BATCH = 8
SAMPLE_RET_BATCH = 2
VMEM_LIMIT = 56 * 1024 * 1024


def _params(n_axes):
    return pltpu.CompilerParams(
        dimension_semantics=("arbitrary",) * n_axes,
        vmem_limit_bytes=VMEM_LIMIT)


def _silu(x):
    hx = 0.5 * x
    return hx + hx * jnp.tanh(hx)


def _rmsnorm(x, g):
    return x * lax.rsqrt(jnp.mean(x * x, axis=-1, keepdims=True) + EPS) * g


def _t5_bucket_np(rel):
    n = np.maximum(rel, 0)
    max_exact = NUM_BUCKETS // 2
    nf = np.maximum(n, 1).astype(np.float64)
    large = max_exact + (np.log(nf / max_exact) / math.log(MAX_DISTANCE / max_exact)
                         * (NUM_BUCKETS - max_exact)).astype(np.int32)
    large = np.minimum(large, NUM_BUCKETS - 1)
    return np.where(n < max_exact, n, large).astype(np.int32)


def _bucket_map_np():
    i = np.arange(WINDOW)[:, None]
    j = np.arange(WINDOW)[None, :]
    rel = np.where(j <= i, i - j, i - j + WINDOW)
    return _t5_bucket_np(rel).astype(np.int32)


def _gammas():
    return [1.0 - 2.0 ** (-5 - h) for h in range(RET_HEADS)]


def _decay_scales_np(rows, block):
    lg = np.log1p(-np.exp2(-5.0 - np.arange(RET_HEADS, dtype=np.float64)))
    i1 = (np.arange(rows) % block + 1.0)[None, :] * lg[:, None]
    q_scale = np.repeat(np.exp(i1)[:, :, None], LANES, axis=2)
    k_scale = np.repeat((np.exp(-i1) * RET_QK_DIM ** -0.5)[:, None, :], LANES, axis=1)
    return (q_scale.astype(np.float32), k_scale.astype(np.float32),
            np.exp(block * lg).astype(np.float32))


def _rope_tables_np(pos):
    half = RET_QK_DIM // 2
    inv = ROPE_BASE ** (-np.arange(half, dtype=np.float64) / half)
    ang = np.asarray(pos, np.float64)[:, None] * inv[None, :]
    return np.cos(ang).astype(np.float32), np.sin(ang).astype(np.float32)


def _bias_table_kernel(rb_ref, bucket_ref, out_ref):
    h = pl.program_id(0)
    bk = bucket_ref[...]
    acc = jnp.zeros(bk.shape, F32)
    for b in range(NUM_BUCKETS):
        acc = jnp.where(bk == b, rb_ref[b, h] * LOG2E, acc)
    row = lax.broadcasted_iota(jnp.int32, bk.shape, 0)
    col = lax.broadcasted_iota(jnp.int32, bk.shape, 1)
    out_ref[1, 0] = acc
    out_ref[0, 0] = jnp.where(col > row, NEG, acc)


def _bias_table(rel_bias):
    bucket = jnp.asarray(_bucket_map_np())
    return pl.pallas_call(
        _bias_table_kernel,
        grid=(ATT_HEADS,),
        in_specs=[pl.BlockSpec(memory_space=pltpu.SMEM),
                  pl.BlockSpec((WINDOW, WINDOW), lambda h: (0, 0))],
        out_specs=pl.BlockSpec((2, 1, WINDOW, WINDOW), lambda h: (0, h, 0, 0)),
        out_shape=jax.ShapeDtypeStruct((2, ATT_HEADS, WINDOW, WINDOW), F32),
        compiler_params=_params(1),
        name="bias_table",
    )(rel_bias, bucket)


def _transpose_cast_kernel(w_ref, o_ref):
    o_ref[...] = w_ref[...].T.astype(o_ref.dtype)


def _transposed_columns(w, col0, ncols, dtype):
    rows = w.shape[0]
    blk = 2 * LANES
    return pl.pallas_call(
        _transpose_cast_kernel,
        grid=(ncols // blk,),
        in_specs=[pl.BlockSpec((rows, blk), lambda i: (0, col0 // blk + i))],
        out_specs=pl.BlockSpec((blk, rows), lambda i: (i, 0)),
        out_shape=jax.ShapeDtypeStruct((ncols, rows), dtype),
        compiler_params=_params(1),
        name="transpose_cast",
    )(w)


def _attn_inproj_kernel(x_ref, g_ref, w_ref, q_ref, kv_ref, gate_ref):
    h = _rmsnorm(x_ref[...], g_ref[...]).astype(BF16)
    for c in range(0, ATT_IN, PROJ_COLS):
        r = jnp.dot(h, w_ref[:, c:c + PROJ_COLS], preferred_element_type=F32)
        if c < ATT_WIDTH:
            q_ref[:, c:c + PROJ_COLS] = (r * (ATT_HEAD_DIM ** -0.5 * LOG2E)).astype(q_ref.dtype)
        elif c < ATT_WIDTH + 2 * ATT_KV_WIDTH:
            kv_ref[...] = r
        else:
            o = c - ATT_WIDTH - 2 * ATT_KV_WIDTH
            gate_ref[:, o:o + PROJ_COLS] = r.astype(gate_ref.dtype)


def _attn_inproj(x, g, w, out_dtype):
    m = x.shape[0]
    tm = min(PROJ_ROWS, m)
    row = lambda i: (i, 0)
    return pl.pallas_call(
        _attn_inproj_kernel,
        grid=(m // tm,),
        in_specs=[pl.BlockSpec((tm, D_MODEL), row),
                  pl.BlockSpec((1, D_MODEL), lambda i: (0, 0)),
                  pl.BlockSpec((D_MODEL, ATT_IN), lambda i: (0, 0))],
        out_specs=[pl.BlockSpec((tm, ATT_WIDTH), row),
                   pl.BlockSpec((tm, 2 * ATT_KV_WIDTH), row),
                   pl.BlockSpec((tm, ATT_WIDTH), row)],
        out_shape=[jax.ShapeDtypeStruct((m, ATT_WIDTH), out_dtype),
                   jax.ShapeDtypeStruct((m, 2 * ATT_KV_WIDTH), F32),
                   jax.ShapeDtypeStruct((m, ATT_WIDTH), out_dtype)],
        compiler_params=_params(1),
        name="attn_inproj",
    )(x, g, w)


def _ret_inproj_kernel(x_ref, g_ref, w_ref, wkt_ref, cos_ref, sin_ref, cost_ref, sint_ref,
                       qsc_ref, ksc_ref, q_ref, kt_ref, v_ref, gate_ref):
    h = _rmsnorm(x_ref[...], g_ref[...]).astype(BF16)
    half = RET_QK_DIM // 2
    contract_last = (((1,), (1,)), ((), ()))
    cos = cos_ref[...]
    sin = sin_ref[...]
    for c in range(0, RET_QK_WIDTH, PROJ_COLS):
        r = jnp.dot(h, w_ref[:, c:c + PROJ_COLS], preferred_element_type=F32)
        for o in range(0, PROJ_COLS, RET_QK_DIM):
            scale = qsc_ref[(c + o) // RET_QK_DIM]
            x1 = r[:, o:o + half]
            x2 = r[:, o + half:o + RET_QK_DIM]
            q_ref[:, c + o:c + o + half] = ((x1 * cos - x2 * sin) * scale).astype(q_ref.dtype)
            q_ref[:, c + o + half:c + o + RET_QK_DIM] = (
                (x1 * sin + x2 * cos) * scale).astype(q_ref.dtype)
    cos = cost_ref[...]
    sin = sint_ref[...]
    for c in range(0, RET_QK_WIDTH, PROJ_COLS):
        r = lax.dot_general(wkt_ref[c:c + PROJ_COLS, :], h, contract_last,
                            preferred_element_type=F32)
        for o in range(0, PROJ_COLS, RET_QK_DIM):
            scale = ksc_ref[(c + o) // RET_QK_DIM]
            x1 = r[o:o + half, :]
            x2 = r[o + half:o + RET_QK_DIM, :]
            kt_ref[c + o:c + o + half, :] = ((x1 * cos - x2 * sin) * scale).astype(kt_ref.dtype)
            kt_ref[c + o + half:c + o + RET_QK_DIM, :] = (
                (x1 * sin + x2 * cos) * scale).astype(kt_ref.dtype)
    for c in range(0, RET_WIDTH, PROJ_COLS):
        v_ref[:, c:c + PROJ_COLS] = jnp.dot(
            h, w_ref[:, 2 * RET_QK_WIDTH + c:2 * RET_QK_WIDTH + c + PROJ_COLS],
            preferred_element_type=F32).astype(v_ref.dtype)
    for c in range(0, RET_WIDTH, PROJ_COLS):
        w0 = 2 * RET_QK_WIDTH + RET_WIDTH + c
        gate_ref[:, c:c + PROJ_COLS] = jnp.dot(
            h, w_ref[:, w0:w0 + PROJ_COLS], preferred_element_type=F32).astype(gate_ref.dtype)


def _ret_inproj(x, g, w, wkt, pos, block, out_dtype):
    m = x.shape[0]
    tm = min(PROJ_ROWS, m)
    n_pos = len(pos) // tm
    half = RET_QK_DIM // 2
    cos, sin = _rope_tables_np(pos)
    q_scale, k_scale, _ = _decay_scales_np(tm, block)
    row = lambda i: (i, 0)
    const2 = lambda i: (0, 0)
    const3 = lambda i: (0, 0, 0)
    return pl.pallas_call(
        _ret_inproj_kernel,
        grid=(m // tm,),
        in_specs=[pl.BlockSpec((tm, D_MODEL), row),
                  pl.BlockSpec((1, D_MODEL), const2),
                  pl.BlockSpec((D_MODEL, RET_IN), const2),
                  pl.BlockSpec((RET_QK_WIDTH, D_MODEL), const2),
                  pl.BlockSpec((tm, half), lambda i: (i % n_pos, 0)),
                  pl.BlockSpec((tm, half), lambda i: (i % n_pos, 0)),
                  pl.BlockSpec((half, tm), lambda i: (0, i % n_pos)),
                  pl.BlockSpec((half, tm), lambda i: (0, i % n_pos)),
                  pl.BlockSpec((RET_HEADS, tm, LANES), const3),
                  pl.BlockSpec((RET_HEADS, LANES, tm), const3)],
        out_specs=[pl.BlockSpec((tm, RET_QK_WIDTH), row),
                   pl.BlockSpec((RET_QK_WIDTH, tm), lambda i: (0, i)),
                   pl.BlockSpec((tm, RET_WIDTH), row),
                   pl.BlockSpec((tm, RET_WIDTH), row)],
        out_shape=[jax.ShapeDtypeStruct((m, RET_QK_WIDTH), out_dtype),
                   jax.ShapeDtypeStruct((RET_QK_WIDTH, m), out_dtype),
                   jax.ShapeDtypeStruct((m, RET_WIDTH), out_dtype),
                   jax.ShapeDtypeStruct((m, RET_WIDTH), out_dtype)],
        compiler_params=_params(1),
        name="ret_inproj",
    )(x, g, w, wkt, jnp.asarray(cos), jnp.asarray(sin), jnp.asarray(cos.T), jnp.asarray(sin.T),
      jnp.asarray(q_scale), jnp.asarray(k_scale))


def _outproj_kernel(o_ref, w_ref, x_ref, y_ref):
    y_ref[...] = x_ref[...] + jnp.dot(o_ref[...].astype(BF16), w_ref[...],
                                      preferred_element_type=F32)


def _outproj_norm_kernel(o_ref, w_ref, x_ref, g_ref, y_ref):
    y = x_ref[...] + jnp.dot(o_ref[...].astype(BF16), w_ref[...], preferred_element_type=F32)
    y_ref[...] = _rmsnorm(y, g_ref[...])


def _outproj(o, w, x, final_g=None):
    m, width = o.shape
    tm = min(OUT_ROWS, m)
    row = lambda i: (i, 0)
    in_specs = [pl.BlockSpec((tm, width), row),
                pl.BlockSpec((width, D_MODEL), lambda i: (0, 0)),
                pl.BlockSpec((tm, D_MODEL), row)]
    args = [o, w, x]
    body = _outproj_kernel
    if final_g is not None:
        in_specs.append(pl.BlockSpec((1, D_MODEL), lambda i: (0, 0)))
        args.append(final_g)
        body = _outproj_norm_kernel
    return pl.pallas_call(
        body,
        grid=(m // tm,),
        in_specs=in_specs,
        out_specs=pl.BlockSpec((tm, D_MODEL), row),
        out_shape=jax.ShapeDtypeStruct((m, D_MODEL), F32),
        compiler_params=_params(1),
        name="outproj",
    )(*args)


def _pad_pair(x, x_rolled, lo, e):
    if e == 0:
        return jnp.where(lo, x, 0.0), jnp.where(lo, 0.0, x_rolled)
    return jnp.where(lo, x_rolled, 0.0), jnp.where(lo, 0.0, x)


def _swa_prompt_kernel(sink_ref, q_ref, kvc_ref, kvp_ref, gate_ref, bias_ref, out_ref):
    j = pl.program_id(1)
    half = ATT_HEAD_DIM
    pairs_per_kv = ATT_GROUP // 2
    kv_group_cols = ATT_GROUP * ATT_HEAD_DIM
    contract_last = (((1,), (1,)), ((), ()))
    lo2 = lax.broadcasted_iota(jnp.int32, (2 * WINDOW, LANES), 1) < half
    lo1 = lax.broadcasted_iota(jnp.int32, (WINDOW, LANES), 1) < half
    own = (lax.broadcasted_iota(jnp.int32, (WINDOW, WINDOW), 1)
           <= lax.broadcasted_iota(jnp.int32, (WINDOW, WINDOW), 0))
    for blk in range(ATT_ROWS // WINDOW):
        r0 = blk * WINDOW
        rows = slice(r0, r0 + WINDOW)
        if blk == 0:
            kv_prev = kvp_ref[...]
            sel = jnp.where(j == 0, 0, 1)
        else:
            kv_prev = kvc_ref[r0 - WINDOW:r0, :]
            sel = 1
        kvcat = jnp.concatenate([kv_prev, kvc_ref[rows, :]], axis=0)
        for pair in range(ATT_KV_HEADS // 2):
            kk = kvcat[:, pair * LANES:(pair + 1) * LANES]
            vv = kvcat[:, ATT_KV_WIDTH + pair * LANES:ATT_KV_WIDTH + (pair + 1) * LANES]
            kk_r = pltpu.roll(kk, half, axis=1)
            vv_r = pltpu.roll(vv, half, axis=1)
            for e in range(2):
                kvh = 2 * pair + e
                k_a, k_b = _pad_pair(kk, kk_r, lo2, e)
                v_a, v_b = _pad_pair(vv, vv_r, lo2, e)
                c0 = kvh * kv_group_cols
                qs = jnp.concatenate(
                    [q_ref[rows, c0 + p * LANES:c0 + (p + 1) * LANES] for p in range(pairs_per_kv)],
                    axis=0)
                logits = (lax.dot_general(qs, k_a.astype(BF16), contract_last,
                                          preferred_element_type=F32),
                          lax.dot_general(qs, k_b.astype(BF16), contract_last,
                                          preferred_element_type=F32))
                probs = ([], [])
                inv = ([], [])
                for p in range(pairs_per_kv):
                    for st in range(2):
                        hd = kvh * ATT_GROUP + 2 * p + st
                        lg = logits[st][p * WINDOW:(p + 1) * WINDOW, :]
                        l = jnp.where(own, lg[:, WINDOW:], lg[:, :WINDOW]) + bias_ref[sel, hd]
                        s = sink_ref[hd]
                        m = jnp.maximum(jnp.max(l, axis=-1, keepdims=True), s)
                        pe = jnp.exp2(l - m)
                        den = jnp.sum(pe, axis=-1, keepdims=True) + jnp.exp2(s - m)
                        probs[st].append(jnp.concatenate(
                            [jnp.where(own, 0.0, pe).astype(BF16),
                             jnp.where(own, pe, 0.0).astype(BF16)], axis=1))
                        inv[st].append(1.0 / den)
                o = (jnp.dot(jnp.concatenate(probs[0], axis=0), v_a.astype(BF16),
                             preferred_element_type=F32)
                     + jnp.dot(jnp.concatenate(probs[1], axis=0), v_b.astype(BF16),
                               preferred_element_type=F32))
                for p in range(pairs_per_kv):
                    cols = slice(c0 + p * LANES, c0 + (p + 1) * LANES)
                    scale = jnp.where(lo1, inv[0][p], inv[1][p])
                    gt = gate_ref[rows, cols].astype(F32)
                    og = o[p * WINDOW:(p + 1) * WINDOW, :] * scale * _silu(gt)
                    out_ref[rows, cols] = og.astype(out_ref.dtype)


def _swa_prompt(sinks, q, kv, gate, bias, batch, seq):
    m = batch * seq
    nt = seq // ATT_ROWS
    blocks_per_tile = ATT_ROWS // WINDOW
    blocks_per_seq = seq // WINDOW
    tile = lambda b, j: (b * nt + j, 0)
    prev = lambda b, j: (b * blocks_per_seq + jnp.maximum(j * blocks_per_tile - 1, 0), 0)
    return pl.pallas_call(
        _swa_prompt_kernel,
        grid=(batch, nt),
        in_specs=[pl.BlockSpec(memory_space=pltpu.SMEM),
                  pl.BlockSpec((ATT_ROWS, ATT_WIDTH), tile),
                  pl.BlockSpec((ATT_ROWS, 2 * ATT_KV_WIDTH), tile),
                  pl.BlockSpec((WINDOW, 2 * ATT_KV_WIDTH), prev),
                  pl.BlockSpec((ATT_ROWS, ATT_WIDTH), tile),
                  pl.BlockSpec((2, ATT_HEADS, WINDOW, WINDOW), lambda b, j: (0, 0, 0, 0))],
        out_specs=pl.BlockSpec((ATT_ROWS, ATT_WIDTH), tile),
        out_shape=jax.ShapeDtypeStruct((m, ATT_WIDTH), BF16),
        compiler_params=_params(2),
        name="swa_prompt",
    )(sinks, q, kv, kv, gate, bias)


def _sample_head_order():
    order = []
    for pair in range(ATT_KV_HEADS // 2):
        base = pair * 2 * ATT_GROUP
        for kv in range(2):
            for half in range(2):
                order += [base + kv * ATT_GROUP + 2 * g + half for g in range(ATT_GROUP // 2)]
    return np.asarray(order, np.int32)


def _swa_sample_kernel(q_ref, kv_ref, ckt_ref, cvt_ref, gate_ref, bias_ref, sink_ref,
                       o_ref, cko_ref, cvo_ref, kvt_ref):
    step = pl.program_id(0)

    @pl.when(step == 0)
    def _():
        kvt_ref[...] = kv_ref[...].T

    half = ATT_HEAD_DIM
    quarter = ATT_GROUP // 2
    contract_last = (((1,), (1,)), ((), ()))
    newest = lax.broadcasted_iota(jnp.int32, (ATT_KV_WIDTH, WINDOW), 1) == WINDOW - 1
    seq_lane = lax.broadcasted_iota(jnp.int32, (2 * ATT_KV_WIDTH, LANES), 1)
    lo4 = lax.broadcasted_iota(jnp.int32, (quarter, LANES), 1) < half
    for i in range(SAMPLE_ATT_BATCH):
        b = step * SAMPLE_ATT_BATCH + i
        new_col = jnp.sum(jnp.where(seq_lane == b, kvt_ref[...], 0.0), axis=1, keepdims=True)
        kb = jnp.where(newest, new_col[:ATT_KV_WIDTH], pltpu.roll(ckt_ref[i], WINDOW - 1, axis=1))
        vb = jnp.where(newest, new_col[ATT_KV_WIDTH:], pltpu.roll(cvt_ref[i], WINDOW - 1, axis=1))
        cko_ref[i] = kb
        cvo_ref[i] = vb
        for pair in range(ATT_KV_HEADS // 2):
            groups = slice(pair * ATT_GROUP, (pair + 1) * ATT_GROUP)
            hs = slice(pair * 2 * ATT_GROUP, (pair + 1) * 2 * ATT_GROUP)
            feat = slice(pair * LANES, (pair + 1) * LANES)
            g8 = q_ref[i, groups, :]
            g8r = pltpu.roll(g8, half, axis=1)
            qbd = jnp.concatenate(
                [jnp.where(lo4, g8[:quarter], 0.0), jnp.where(lo4, g8r[:quarter], 0.0),
                 jnp.where(lo4, 0.0, g8r[quarter:]), jnp.where(lo4, 0.0, g8[quarter:])],
                axis=0)
            l = jnp.dot(qbd, kb[feat, :], preferred_element_type=F32) + bias_ref[hs, :]
            s = sink_ref[hs, 0:1]
            m = jnp.maximum(jnp.max(l, axis=-1, keepdims=True), s)
            pe = jnp.exp2(l - m)
            den = jnp.sum(pe, axis=-1, keepdims=True) + jnp.exp2(s - m)
            o2 = lax.dot_general(pe, vb[feat, :], contract_last,
                                 preferred_element_type=F32) / den
            o2r = pltpu.roll(o2, half, axis=1)
            og = jnp.concatenate(
                [jnp.where(lo4, o2[:quarter], o2r[quarter:2 * quarter]),
                 jnp.where(lo4, o2r[2 * quarter:3 * quarter], o2[3 * quarter:])], axis=0)
            o_ref[i, groups, :] = og * _silu(gate_ref[i, groups, :])


def _swa_sample(q, kv, ckt, cvt, gate, bias_s, sinks_rep):
    n = q.shape[0]
    bb = SAMPLE_ATT_BATCH
    n_groups = ATT_WIDTH // LANES
    blk3 = lambda i: (i, 0, 0)
    whole = lambda i: (0, 0)
    return pl.pallas_call(
        _swa_sample_kernel,
        grid=(n // bb,),
        in_specs=[pl.BlockSpec((bb, n_groups, LANES), blk3),
                  pl.BlockSpec((n, 2 * ATT_KV_WIDTH), whole),
                  pl.BlockSpec((bb, ATT_KV_WIDTH, WINDOW), blk3),
                  pl.BlockSpec((bb, ATT_KV_WIDTH, WINDOW), blk3),
                  pl.BlockSpec((bb, n_groups, LANES), blk3),
                  pl.BlockSpec((ATT_HEADS, WINDOW), whole),
                  pl.BlockSpec((ATT_HEADS, LANES), whole)],
        out_specs=[pl.BlockSpec((bb, n_groups, LANES), blk3),
                   pl.BlockSpec((bb, ATT_KV_WIDTH, WINDOW), blk3),
                   pl.BlockSpec((bb, ATT_KV_WIDTH, WINDOW), blk3)],
        out_shape=[jax.ShapeDtypeStruct((n, n_groups, LANES), F32),
                   jax.ShapeDtypeStruct((n, ATT_KV_WIDTH, WINDOW), F32),
                   jax.ShapeDtypeStruct((n, ATT_KV_WIDTH, WINDOW), F32)],
        scratch_shapes=[pltpu.VMEM((2 * ATT_KV_WIDTH, n), F32)],
        compiler_params=_params(1),
        name="swa_sample",
    )(q, kv, ckt, cvt, gate, bias_s, sinks_rep)


def _group_norm_gate(o, gate):
    mu = jnp.mean(o, axis=-1, keepdims=True)
    d = o - mu
    var = jnp.mean(d * d, axis=-1, keepdims=True)
    return d * lax.rsqrt(var + EPS) * _silu(gate)


def _ret_prompt_kernel(g_block_ref, q_ref, kt_ref, v_ref, gate_ref, x_ref, wout_ref, gf_ref,
                       y_ref, r_ref, og_ref):
    c = pl.program_id(1)

    @pl.when(c == 0)
    def _():
        r_ref[...] = jnp.zeros_like(r_ref)

    causal = (lax.broadcasted_iota(jnp.int32, (RET_BLOCK, RET_BLOCK), 1)
              <= lax.broadcasted_iota(jnp.int32, (RET_BLOCK, RET_BLOCK), 0))
    for h in range(RET_HEADS):
        qcols = slice(h * RET_QK_DIM, (h + 1) * RET_QK_DIM)
        vcols = slice(h * RET_V_DIM, (h + 1) * RET_V_DIM)
        for blk in range(RET_ROWS // RET_BLOCK):
            rows = slice(blk * RET_BLOCK, (blk + 1) * RET_BLOCK)
            qh = q_ref[rows, qcols]
            kt = kt_ref[qcols, rows]
            vh = v_ref[rows, vcols]
            state = r_ref[0, h]
            scores = jnp.where(causal, jnp.dot(qh, kt, preferred_element_type=F32), 0.0)
            o = (jnp.dot(scores.astype(BF16), vh, preferred_element_type=F32)
                 + jnp.dot(qh, state.astype(BF16), preferred_element_type=F32))
            r_ref[0, h] = g_block_ref[h] * (state + jnp.dot(kt, vh, preferred_element_type=F32))
            og_ref[rows, :] = _group_norm_gate(
                o, gate_ref[rows, vcols].astype(F32)).astype(og_ref.dtype)
        part = jnp.dot(og_ref[...], wout_ref[vcols, :], preferred_element_type=F32)
        if h == 0:
            y_ref[...] = x_ref[...] + part
        elif h < RET_HEADS - 1:
            y_ref[...] += part
        else:
            y_ref[...] = _rmsnorm(y_ref[...] + part, gf_ref[...])


def _ret_prompt(q, kt, v, gate, x, w_out, final_g, batch, seq):
    m = batch * seq
    nt = seq // RET_ROWS
    g_block = jnp.asarray(_decay_scales_np(RET_BLOCK, RET_BLOCK)[2])
    rows = lambda b, c: (b * nt + c, 0)
    return pl.pallas_call(
        _ret_prompt_kernel,
        grid=(batch, nt),
        in_specs=[pl.BlockSpec(memory_space=pltpu.SMEM),
                  pl.BlockSpec((RET_ROWS, RET_QK_WIDTH), rows),
                  pl.BlockSpec((RET_QK_WIDTH, RET_ROWS), lambda b, c: (0, b * nt + c)),
                  pl.BlockSpec((RET_ROWS, RET_WIDTH), rows),
                  pl.BlockSpec((RET_ROWS, RET_WIDTH), rows),
                  pl.BlockSpec((RET_ROWS, D_MODEL), rows),
                  pl.BlockSpec((RET_WIDTH, D_MODEL), lambda b, c: (0, 0)),
                  pl.BlockSpec((1, D_MODEL), lambda b, c: (0, 0))],
        out_specs=[pl.BlockSpec((RET_ROWS, D_MODEL), rows),
                   pl.BlockSpec((1, RET_HEADS, RET_QK_DIM, RET_V_DIM), lambda b, c: (b, 0, 0, 0))],
        out_shape=[jax.ShapeDtypeStruct((m, D_MODEL), F32),
                   jax.ShapeDtypeStruct((batch, RET_HEADS, RET_QK_DIM, RET_V_DIM), F32)],
        scratch_shapes=[pltpu.VMEM((RET_ROWS, RET_V_DIM), BF16)],
        compiler_params=_params(2),
        name="ret_prompt",
    )(g_block, q, kt, v, gate, x, w_out, final_g)


def _ret_sample_kernel(q_ref, kt_ref, v_ref, gate_ref, st_ref, og_ref, sto_ref, qt_ref):
    step = pl.program_id(0)

    @pl.when(step == 0)
    def _():
        qt_ref[...] = q_ref[...].T

    lane = lax.broadcasted_iota(jnp.int32, (RET_QK_DIM, LANES), 1)
    gammas = _gammas()
    for i in range(SAMPLE_RET_BATCH):
        b = step * SAMPLE_RET_BATCH + i
        pick = lane == b
        for h in range(RET_HEADS):
            rows = slice(h * RET_QK_DIM, (h + 1) * RET_QK_DIM)
            qc = jnp.sum(jnp.where(pick, qt_ref[rows, :], 0.0), axis=1, keepdims=True)
            kc = jnp.sum(jnp.where(pick, kt_ref[rows, :], 0.0), axis=1, keepdims=True)
            vcols = slice(h * RET_V_DIM, (h + 1) * RET_V_DIM)
            vrow = v_ref[pl.ds(b, 1), vcols]
            state = st_ref[i, h]
            o = (jnp.sum(qc * state, axis=0, keepdims=True)
                 + jnp.sum(qc * kc, axis=0, keepdims=True) * vrow)
            sto_ref[i, h] = gammas[h] * (state + kc * vrow)
            og_ref[pl.ds(b, 1), vcols] = _group_norm_gate(o, gate_ref[pl.ds(b, 1), vcols])


def _ret_sample(q, kt, v, gate, state):
    n = q.shape[0]
    bb = SAMPLE_RET_BATCH
    whole = lambda i: (0, 0)
    st_spec = pl.BlockSpec((bb, RET_HEADS, RET_QK_DIM, RET_V_DIM), lambda i: (i, 0, 0, 0))
    return pl.pallas_call(
        _ret_sample_kernel,
        grid=(n // bb,),
        in_specs=[pl.BlockSpec((n, RET_QK_WIDTH), whole),
                  pl.BlockSpec((RET_QK_WIDTH, n), whole),
                  pl.BlockSpec((n, RET_WIDTH), whole),
                  pl.BlockSpec((n, RET_WIDTH), whole),
                  st_spec],
        out_specs=[pl.BlockSpec((n, RET_WIDTH), whole), st_spec],
        out_shape=[jax.ShapeDtypeStruct((n, RET_WIDTH), F32),
                   jax.ShapeDtypeStruct(state.shape, F32)],
        scratch_shapes=[pltpu.VMEM((RET_QK_WIDTH, n), F32)],
        compiler_params=_params(1),
        name="ret_sample",
    )(q, kt, v, gate, state)


def kernel(x_prompt, x_sample, cache_swa_k, cache_swa_v, state_ret, norm_g, final_norm_g, rel_bias,
           w_in_attn, attn_sinks, w_out_attn, w_in_ret, w_out_ret):
    batch, seq, _ = x_prompt.shape
    n_s = x_sample.shape[0]
    xp = x_prompt.reshape(batch * seq, D_MODEL)
    xs = x_sample.reshape(n_s, D_MODEL)
    g0 = norm_g[0].reshape(1, D_MODEL)
    g1 = norm_g[1].reshape(1, D_MODEL)
    gf = final_norm_g.reshape(1, D_MODEL)
    w_in_a = w_in_attn[0].astype(BF16)
    w_out_a = w_out_attn[0].astype(BF16)
    w_in_r = w_in_ret[0].astype(BF16)
    w_out_r = w_out_ret[0].astype(BF16)
    sinks = attn_sinks[0] * LOG2E

    bias = _bias_table(rel_bias)

    q, kv, gate = _attn_inproj(xp, g0, w_in_a, BF16)
    og = _swa_prompt(sinks, q, kv, gate, bias, batch, seq)
    xp1 = _outproj(og, w_out_a, xp)
    kv_last = kv.reshape(batch, seq, 2 * ATT_KV_WIDTH)[:, seq - WINDOW:]
    k_prompt = kv_last[..., :ATT_KV_WIDTH].reshape(1, batch, WINDOW, ATT_KV_HEADS, ATT_HEAD_DIM)
    v_prompt = kv_last[..., ATT_KV_WIDTH:].reshape(1, batch, WINDOW, ATT_KV_HEADS, ATT_HEAD_DIM)

    qs, kvs, gates = _attn_inproj(xs, g0, w_in_a, F32)
    order = _sample_head_order()
    bias_s = bias[1, :, WINDOW - 1, :][order]
    sinks_rep = jnp.broadcast_to(sinks[order][:, None], (ATT_HEADS, LANES))
    to_feature_major = lambda c: jnp.transpose(c, (0, 2, 3, 1)).reshape(n_s, ATT_KV_WIDTH, WINDOW)
    to_window_major = lambda c: jnp.transpose(
        c.reshape(n_s, ATT_KV_HEADS, ATT_HEAD_DIM, WINDOW), (0, 3, 1, 2))[None]
    n_groups = ATT_WIDTH // LANES
    ogs, ckt_new, cvt_new = _swa_sample(
        qs.reshape(n_s, n_groups, LANES), kvs,
        to_feature_major(cache_swa_k[0]), to_feature_major(cache_swa_v[0]),
        gates.reshape(n_s, n_groups, LANES), bias_s, sinks_rep)
    xs1 = _outproj(ogs.reshape(n_s, ATT_WIDTH), w_out_a, xs)
    k_sample = to_window_major(ckt_new)
    v_sample = to_window_major(cvt_new)

    w_kt = _transposed_columns(w_in_ret[0], RET_QK_WIDTH, RET_QK_WIDTH, BF16)
    q, kt, v, gate = _ret_inproj(xp1, g1, w_in_r, w_kt, np.arange(seq), RET_BLOCK, BF16)
    y_prompt, r_prompt = _ret_prompt(q, kt, v, gate, xp1, w_out_r, gf, batch, seq)
    y_prompt = y_prompt.reshape(batch, seq, D_MODEL)

    q_s, kt_s, v_s, gate_s = _ret_inproj(xs1, g1, w_in_r, w_kt, np.full((n_s,), PAST_LEN), 1, F32)
    og_s, r_sample = _ret_sample(q_s, kt_s, v_s, gate_s, state_ret[0])
    y_sample = _outproj(og_s, w_out_r, xs1, gf).reshape(n_s, 1, D_MODEL)

    return (y_prompt, y_sample, k_prompt, v_prompt, r_prompt[None],
            k_sample, v_sample, r_sample[None])
```

```python
import functools
import math

import numpy as np
import jax
import jax.numpy as jnp
from jax import lax
from jax.experimental import pallas as pl
from jax.experimental.pallas import tpu as pltpu

F32 = jnp.float32
BF16 = jnp.bfloat16

D_MODEL = 1024
PAST_LEN = 16384
ATT_HEADS = 32
ATT_KV_HEADS = 4
ATT_GROUP = ATT_HEADS // ATT_KV_HEADS
ATT_HEAD_DIM = 64
ATT_WIDTH = ATT_HEADS * ATT_HEAD_DIM
ATT_KV_WIDTH = ATT_KV_HEADS * ATT_HEAD_DIM
ATT_IN = 2 * ATT_WIDTH + 2 * ATT_KV_WIDTH
PADDED_KV_VARIANTS = 4
PADDED_KV_WIDTH = ATT_KV_HEADS * PADDED_KV_VARIANTS * 128
WINDOW = 128
NUM_BUCKETS = 32
MAX_DISTANCE = 128
RET_HEADS = 4
RET_QK_DIM = 256
RET_V_DIM = 512
RET_QK_WIDTH = RET_HEADS * RET_QK_DIM
RET_WIDTH = RET_HEADS * RET_V_DIM
RET_IN = 2 * RET_QK_WIDTH + 2 * RET_WIDTH
ROPE_BASE = 10000.0
EPS = 1e-6
NEG = -1e30
LOG2E = math.log2(math.e)

LANES = 128
PROJ_ROWS = 512
PROJ_COLS = 512
OUT_ROWS = 1024
ATT_ROWS = 512
RET_BLOCK = 256
RET_ROWS = 512
SAMPLE_ATT_BATCH = 8
SAMPLE_RET_BATCH = 2
VMEM_LIMIT = 56 * 1024 * 1024


def _params(n_axes):
    return pltpu.CompilerParams(
        dimension_semantics=("arbitrary",) * n_axes,
        vmem_limit_bytes=VMEM_LIMIT)


def _silu(x):
    hx = 0.5 * x
    return hx + hx * jnp.tanh(hx)


def _rmsnorm(x, g):
    return x * lax.rsqrt(jnp.mean(x * x, axis=-1, keepdims=True) + EPS) * g


def _t5_bucket_np(rel):
    n = np.maximum(rel, 0)
    max_exact = NUM_BUCKETS // 2
    nf = np.maximum(n, 1).astype(np.float64)
    large = max_exact + (np.log(nf / max_exact) / math.log(MAX_DISTANCE / max_exact)
                         * (NUM_BUCKETS - max_exact)).astype(np.int32)
    large = np.minimum(large, NUM_BUCKETS - 1)
    return np.where(n < max_exact, n, large).astype(np.int32)


def _bucket_map_np():
    i = np.arange(WINDOW)[:, None]
    j = np.arange(WINDOW)[None, :]
    rel = np.where(j <= i, i - j, i - j + WINDOW)
    return _t5_bucket_np(rel).astype(np.int32)


def _gammas():
    return [1.0 - 2.0 ** (-5 - h) for h in range(RET_HEADS)]


def _decay_scales_np(rows, block):
    lg = np.log1p(-np.exp2(-5.0 - np.arange(RET_HEADS, dtype=np.float64)))
    i1 = (np.arange(rows) % block + 1.0)[None, :] * lg[:, None]
    q_scale = np.repeat(np.exp(i1)[:, :, None], LANES, axis=2)
    k_scale = np.repeat((np.exp(-i1) * RET_QK_DIM ** -0.5)[:, None, :], LANES, axis=1)
    return (q_scale.astype(np.float32), k_scale.astype(np.float32),
            np.exp(block * lg).astype(np.float32))


def _rope_tables_np(pos):
    half = RET_QK_DIM // 2
    inv = ROPE_BASE ** (-np.arange(half, dtype=np.float64) / half)
    ang = np.asarray(pos, np.float64)[:, None] * inv[None, :]
    return np.cos(ang).astype(np.float32), np.sin(ang).astype(np.float32)


def _bias_table_kernel(rb_ref, bucket_ref, out_ref):
    h = pl.program_id(0)
    bk = bucket_ref[...]
    acc = jnp.zeros(bk.shape, F32)
    for b in range(NUM_BUCKETS):
        acc = jnp.where(bk == b, rb_ref[b, h] * LOG2E, acc)
    row = lax.broadcasted_iota(jnp.int32, bk.shape, 0)
    col = lax.broadcasted_iota(jnp.int32, bk.shape, 1)
    out_ref[1, 0] = acc
    out_ref[0, 0] = jnp.where(col > row, NEG, acc)


def _bias_table(rel_bias):
    bucket = jnp.asarray(_bucket_map_np())
    return pl.pallas_call(
        _bias_table_kernel,
        grid=(ATT_HEADS,),
        in_specs=[pl.BlockSpec(memory_space=pltpu.SMEM),
                  pl.BlockSpec((WINDOW, WINDOW), lambda h: (0, 0))],
        out_specs=pl.BlockSpec((2, 1, WINDOW, WINDOW), lambda h: (0, h, 0, 0)),
        out_shape=jax.ShapeDtypeStruct((2, ATT_HEADS, WINDOW, WINDOW), F32),
        compiler_params=_params(1),
        name="bias_table",
    )(rel_bias, bucket)


def _transpose_cast_kernel(w_ref, o_ref):
    o_ref[...] = w_ref[...].T.astype(o_ref.dtype)


def _transposed_columns(w, col0, ncols, dtype):
    rows = w.shape[0]
    blk = 2 * LANES
    return pl.pallas_call(
        _transpose_cast_kernel,
        grid=(ncols // blk,),
        in_specs=[pl.BlockSpec((rows, blk), lambda i: (0, col0 // blk + i))],
        out_specs=pl.BlockSpec((blk, rows), lambda i: (i, 0)),
        out_shape=jax.ShapeDtypeStruct((ncols, rows), dtype),
        compiler_params=_params(1),
        name="transpose_cast",
    )(w)


def _pad_pair(x, x_rolled, lo, e):
    if e == 0:
        return jnp.where(lo, x, 0.0), jnp.where(lo, 0.0, x_rolled)
    return jnp.where(lo, x_rolled, 0.0), jnp.where(lo, 0.0, x)


def _attn_inproj_kernel(x_ref, g_ref, w_ref, q_ref, kv_ref, gate_ref, kvpad_ref=None):
    h = _rmsnorm(x_ref[...], g_ref[...]).astype(BF16)
    for c in range(0, ATT_IN, PROJ_COLS):
        r = jnp.dot(h, w_ref[:, c:c + PROJ_COLS], preferred_element_type=F32)
        if c < ATT_WIDTH:
            q_ref[:, c:c + PROJ_COLS] = (r * (ATT_HEAD_DIM ** -0.5 * LOG2E)).astype(q_ref.dtype)
        elif c < ATT_WIDTH + 2 * ATT_KV_WIDTH:
            kv_ref[...] = r
            if kvpad_ref is not None:
                lo = lax.broadcasted_iota(jnp.int32, (r.shape[0], LANES), 1) < ATT_HEAD_DIM
                for pair in range(ATT_KV_HEADS // 2):
                    for is_v in range(2):
                        c1 = is_v * ATT_KV_WIDTH + pair * LANES
                        x2 = r[:, c1:c1 + LANES]
                        x2r = pltpu.roll(x2, ATT_HEAD_DIM, axis=1)
                        for e in range(2):
                            g0 = ((2 * pair + e) * PADDED_KV_VARIANTS + 2 * is_v) * LANES
                            lo_half, hi_half = _pad_pair(x2, x2r, lo, e)
                            kvpad_ref[:, g0:g0 + LANES] = lo_half.astype(kvpad_ref.dtype)
                            kvpad_ref[:, g0 + LANES:g0 + 2 * LANES] = hi_half.astype(kvpad_ref.dtype)
        else:
            o = c - ATT_WIDTH - 2 * ATT_KV_WIDTH
            gate_ref[:, o:o + PROJ_COLS] = r.astype(gate_ref.dtype)


def _attn_inproj(x, g, w, out_dtype, with_padded_kv):
    m = x.shape[0]
    tm = min(PROJ_ROWS, m)
    row = lambda i: (i, 0)
    out_specs = [pl.BlockSpec((tm, ATT_WIDTH), row),
                 pl.BlockSpec((tm, 2 * ATT_KV_WIDTH), row),
                 pl.BlockSpec((tm, ATT_WIDTH), row)]
    out_shape = [jax.ShapeDtypeStruct((m, ATT_WIDTH), out_dtype),
                 jax.ShapeDtypeStruct((m, 2 * ATT_KV_WIDTH), F32),
                 jax.ShapeDtypeStruct((m, ATT_WIDTH), out_dtype)]
    if with_padded_kv:
        out_specs.append(pl.BlockSpec((tm, PADDED_KV_WIDTH), row))
        out_shape.append(jax.ShapeDtypeStruct((m, PADDED_KV_WIDTH), BF16))
    return pl.pallas_call(
        _attn_inproj_kernel,
        grid=(m // tm,),
        in_specs=[pl.BlockSpec((tm, D_MODEL), row),
                  pl.BlockSpec((1, D_MODEL), lambda i: (0, 0)),
                  pl.BlockSpec((D_MODEL, ATT_IN), lambda i: (0, 0))],
        out_specs=out_specs,
        out_shape=out_shape,
        compiler_params=_params(1),
        name="attn_inproj",
    )(x, g, w)


def _ret_inproj_kernel(x_ref, g_ref, w_ref, wkt_ref, cos_ref, sin_ref, cost_ref, sint_ref,
                       qsc_ref, ksc_ref, q_ref, kt_ref, v_ref, gate_ref):
    h = _rmsnorm(x_ref[...], g_ref[...]).astype(BF16)
    half = RET_QK_DIM // 2
    contract_last = (((1,), (1,)), ((), ()))
    cos = cos_ref[...]
    sin = sin_ref[...]
    for c in range(0, RET_QK_WIDTH, PROJ_COLS):
        r = jnp.dot(h, w_ref[:, c:c + PROJ_COLS], preferred_element_type=F32)
        for o in range(0, PROJ_COLS, RET_QK_DIM):
            scale = qsc_ref[(c + o) // RET_QK_DIM]
            x1 = r[:, o:o + half]
            x2 = r[:, o + half:o + RET_QK_DIM]
            q_ref[:, c + o:c + o + half] = ((x1 * cos - x2 * sin) * scale).astype(q_ref.dtype)
            q_ref[:, c + o + half:c + o + RET_QK_DIM] = (
                (x1 * sin + x2 * cos) * scale).astype(q_ref.dtype)
    cos = cost_ref[...]
    sin = sint_ref[...]
    for c in range(0, RET_QK_WIDTH, PROJ_COLS):
        r = lax.dot_general(wkt_ref[c:c + PROJ_COLS, :], h, contract_last,
                            preferred_element_type=F32)
        for o in range(0, PROJ_COLS, RET_QK_DIM):
            scale = ksc_ref[(c + o) // RET_QK_DIM]
            x1 = r[o:o + half, :]
            x2 = r[o + half:o + RET_QK_DIM, :]
            kt_ref[c + o:c + o + half, :] = ((x1 * cos - x2 * sin) * scale).astype(kt_ref.dtype)
            kt_ref[c + o + half:c + o + RET_QK_DIM, :] = (
                (x1 * sin + x2 * cos) * scale).astype(kt_ref.dtype)
    for c in range(0, RET_WIDTH, PROJ_COLS):
        v_ref[:, c:c + PROJ_COLS] = jnp.dot(
            h, w_ref[:, 2 * RET_QK_WIDTH + c:2 * RET_QK_WIDTH + c + PROJ_COLS],
            preferred_element_type=F32).astype(v_ref.dtype)
    for c in range(0, RET_WIDTH, PROJ_COLS):
        w0 = 2 * RET_QK_WIDTH + RET_WIDTH + c
        gate_ref[:, c:c + PROJ_COLS] = jnp.dot(
            h, w_ref[:, w0:w0 + PROJ_COLS], preferred_element_type=F32).astype(gate_ref.dtype)


def _ret_inproj(x, g, w, wkt, pos, block, out_dtype):
    m = x.shape[0]
    tm = min(PROJ_ROWS, m)
    n_pos = len(pos) // tm
    half = RET_QK_DIM // 2
    cos, sin = _rope_tables_np(pos)
    q_scale, k_scale, _ = _decay_scales_np(tm, block)
    row = lambda i: (i, 0)
    const2 = lambda i: (0, 0)
    const3 = lambda i: (0, 0, 0)
    return pl.pallas_call(
        _ret_inproj_kernel,
        grid=(m // tm,),
        in_specs=[pl.BlockSpec((tm, D_MODEL), row),
                  pl.BlockSpec((1, D_MODEL), const2),
                  pl.BlockSpec((D_MODEL, RET_IN), const2),
                  pl.BlockSpec((RET_QK_WIDTH, D_MODEL), const2),
                  pl.BlockSpec((tm, half), lambda i: (i % n_pos, 0)),
                  pl.BlockSpec((tm, half), lambda i: (i % n_pos, 0)),
                  pl.BlockSpec((half, tm), lambda i: (0, i % n_pos)),
                  pl.BlockSpec((half, tm), lambda i: (0, i % n_pos)),
                  pl.BlockSpec((RET_HEADS, tm, LANES), const3),
                  pl.BlockSpec((RET_HEADS, LANES, tm), const3)],
        out_specs=[pl.BlockSpec((tm, RET_QK_WIDTH), row),
                   pl.BlockSpec((RET_QK_WIDTH, tm), lambda i: (0, i)),
                   pl.BlockSpec((tm, RET_WIDTH), row),
                   pl.BlockSpec((tm, RET_WIDTH), row)],
        out_shape=[jax.ShapeDtypeStruct((m, RET_QK_WIDTH), out_dtype),
                   jax.ShapeDtypeStruct((RET_QK_WIDTH, m), out_dtype),
                   jax.ShapeDtypeStruct((m, RET_WIDTH), out_dtype),
                   jax.ShapeDtypeStruct((m, RET_WIDTH), out_dtype)],
        compiler_params=_params(1),
        name="ret_inproj",
    )(x, g, w, wkt, jnp.asarray(cos), jnp.asarray(sin), jnp.asarray(cos.T), jnp.asarray(sin.T),
      jnp.asarray(q_scale), jnp.asarray(k_scale))


def _outproj_kernel(o_ref, w_ref, x_ref, y_ref):
    y_ref[...] = x_ref[...] + jnp.dot(o_ref[...].astype(BF16), w_ref[...],
                                      preferred_element_type=F32)


def _outproj_norm_kernel(o_ref, w_ref, x_ref, g_ref, y_ref):
    y = x_ref[...] + jnp.dot(o_ref[...].astype(BF16), w_ref[...], preferred_element_type=F32)
    y_ref[...] = _rmsnorm(y, g_ref[...])


def _outproj(o, w, x, final_g=None):
    m, width = o.shape
    tm = min(OUT_ROWS, m)
    row = lambda i: (i, 0)
    in_specs = [pl.BlockSpec((tm, width), row),
                pl.BlockSpec((width, D_MODEL), lambda i: (0, 0)),
                pl.BlockSpec((tm, D_MODEL), row)]
    args = [o, w, x]
    body = _outproj_kernel
    if final_g is not None:
        in_specs.append(pl.BlockSpec((1, D_MODEL), lambda i: (0, 0)))
        args.append(final_g)
        body = _outproj_norm_kernel
    return pl.pallas_call(
        body,
        grid=(m // tm,),
        in_specs=in_specs,
        out_specs=pl.BlockSpec((tm, D_MODEL), row),
        out_shape=jax.ShapeDtypeStruct((m, D_MODEL), F32),
        compiler_params=_params(1),
        name="outproj",
    )(*args)


def _swa_prompt_kernel(sink_ref, q_ref, kvc_ref, kvp_ref, gate_ref, bias_ref, out_ref):
    j = pl.program_id(1)
    half = ATT_HEAD_DIM
    pairs_per_kv = ATT_GROUP // 2
    kv_group_cols = ATT_GROUP * ATT_HEAD_DIM
    contract_last = (((1,), (1,)), ((), ()))
    lo1 = lax.broadcasted_iota(jnp.int32, (WINDOW, LANES), 1) < half
    own = (lax.broadcasted_iota(jnp.int32, (WINDOW, WINDOW), 1)
           <= lax.broadcasted_iota(jnp.int32, (WINDOW, WINDOW), 0))
    units =[(blk, kvh) for blk in range(ATT_ROWS // WINDOW) for kvh in range(ATT_KV_HEADS)]

    def keys_values(blk, kvh):
        r0 = blk * WINDOW
        out = []
        for variant in range(PADDED_KV_VARIANTS):
            c = (kvh * PADDED_KV_VARIANTS + variant) * LANES
            prev = kvp_ref[:, c:c + LANES] if blk == 0 else kvc_ref[r0 - WINDOW:r0, c:c + LANES]
            out.append(jnp.concatenate([prev, kvc_ref[r0:r0 + WINDOW, c:c + LANES]], axis=0))
        return out

    def scores(blk, kvh):
        rows = slice(blk * WINDOW, (blk + 1) * WINDOW)
        c0 = kvh * kv_group_cols
        k_a, k_b, v_a, v_b = keys_values(blk, kvh)
        qs = jnp.concatenate(
            [q_ref[rows, c0 + p * LANES:c0 + (p + 1) * LANES] for p in range(pairs_per_kv)],
            axis=0)
        logits = (lax.dot_general(qs, k_a, contract_last, preferred_element_type=F32),
                  lax.dot_general(qs, k_b, contract_last, preferred_element_type=F32))
        return logits, v_a, v_b

    def softmax(blk, kvh, logits):
        sel = jnp.where(j == 0, 0, 1) if blk == 0 else 1
        probs = ([], [])
        inv = ([], [])
        for p in range(pairs_per_kv):
            for st in range(2):
                hd = kvh * ATT_GROUP + 2 * p + st
                lg = logits[st][p * WINDOW:(p + 1) * WINDOW, :]
                l = jnp.where(own, lg[:, WINDOW:], lg[:, :WINDOW]) + bias_ref[sel, hd]
                s = sink_ref[hd]
                m = jnp.maximum(jnp.max(l, axis=-1, keepdims=True), s)
                pe = jnp.exp2(l - m)
                den = jnp.sum(pe, axis=-1, keepdims=True) + jnp.exp2(s - m)
                probs[st].append(jnp.concatenate(
                    [jnp.where(own, 0.0, pe).astype(BF16),
                     jnp.where(own, pe, 0.0).astype(BF16)], axis=1))
                inv[st].append(1.0 / den)
        return jnp.concatenate(probs[0], axis=0), jnp.concatenate(probs[1], axis=0), inv

    def output(blk, kvh, p_a, p_b, inv, v_a, v_b):
        rows = slice(blk * WINDOW, (blk + 1) * WINDOW)
        c0 = kvh * kv_group_cols
        o = (jnp.dot(p_a, v_a, preferred_element_type=F32)
             + jnp.dot(p_b, v_b, preferred_element_type=F32))
        for p in range(pairs_per_kv):
            cols = slice(c0 + p * LANES, c0 + (p + 1) * LANES)
            scale = jnp.where(lo1, inv[0][p], inv[1][p])
            gt = gate_ref[rows, cols].astype(F32)
            og = o[p * WINDOW:(p + 1) * WINDOW, :] * scale * _silu(gt)
            out_ref[rows, cols] = og.astype(out_ref.dtype)

    for unit in units:
        logits, v_a, v_b = scores(*unit)
        p_a, p_b, inv = softmax(*unit, logits)
        output(*unit, p_a, p_b, inv, v_a, v_b)


def _swa_prompt(sinks, q, kvpad, gate, bias, batch, seq):
    m = batch * seq
    nt = seq // ATT_ROWS
    blocks_per_tile = ATT_ROWS // WINDOW
    blocks_per_seq = seq // WINDOW
    tile = lambda b, j: (b * nt + j, 0)
    prev = lambda b, j: (b * blocks_per_seq + jnp.maximum(j * blocks_per_tile - 1, 0), 0)
    return pl.pallas_call(
        _swa_prompt_kernel,
        grid=(batch, nt),
        in_specs=[pl.BlockSpec(memory_space=pltpu.SMEM),
                  pl.BlockSpec((ATT_ROWS, ATT_WIDTH), tile),
                  pl.BlockSpec((ATT_ROWS, PADDED_KV_WIDTH), tile),
                  pl.BlockSpec((WINDOW, PADDED_KV_WIDTH), prev),
                  pl.BlockSpec((ATT_ROWS, ATT_WIDTH), tile),
                  pl.BlockSpec((2, ATT_HEADS, WINDOW, WINDOW), lambda b, j: (0, 0, 0, 0))],
        out_specs=pl.BlockSpec((ATT_ROWS, ATT_WIDTH), tile),
        out_shape=jax.ShapeDtypeStruct((m, ATT_WIDTH), BF16),
        compiler_params=_params(2),
        name="swa_prompt",
    )(sinks, q, kvpad, kvpad, gate, bias)


def _sample_head_order():
    order = []
    for pair in range(ATT_KV_HEADS // 2):
        base = pair * 2 * ATT_GROUP
        for kv in range(2):
            for half in range(2):
                order += [base + kv * ATT_GROUP + 2 * g + half for g in range(ATT_GROUP // 2)]
    return np.asarray(order, np.int32)


def _swa_sample_kernel(q_ref, kv_ref, ckt_ref, cvt_ref, gate_ref, bias_ref, sink_ref,
                       o_ref, cko_ref, cvo_ref, kvt_ref):
    step = pl.program_id(0)

    @pl.when(step == 0)
    def _():
        kvt_ref[...] = kv_ref[...].T

    half = ATT_HEAD_DIM
    quarter = ATT_GROUP // 2
    contract_last = (((1,), (1,)), ((), ()))
    newest = lax.broadcasted_iota(jnp.int32, (ATT_KV_WIDTH, WINDOW), 1) == WINDOW - 1
    seq_lane = lax.broadcasted_iota(jnp.int32, (2 * ATT_KV_WIDTH, LANES), 1)
    lo4 = lax.broadcasted_iota(jnp.int32, (quarter, LANES), 1) < half
    for i in range(SAMPLE_ATT_BATCH):
        b = step * SAMPLE_ATT_BATCH + i
        new_col = jnp.sum(jnp.where(seq_lane == b, kvt_ref[...], 0.0), axis=1, keepdims=True)
        kb = jnp.where(newest, new_col[:ATT_KV_WIDTH], pltpu.roll(ckt_ref[i], WINDOW - 1, axis=1))
        vb = jnp.where(newest, new_col[ATT_KV_WIDTH:], pltpu.roll(cvt_ref[i], WINDOW - 1, axis=1))
        cko_ref[i] = kb
        cvo_ref[i] = vb
        for pair in range(ATT_KV_HEADS // 2):
            groups = slice(pair * ATT_GROUP, (pair + 1) * ATT_GROUP)
            hs = slice(pair * 2 * ATT_GROUP, (pair + 1) * 2 * ATT_GROUP)
            feat = slice(pair * LANES, (pair + 1) * LANES)
            g8 = q_ref[i, groups, :]
            g8r = pltpu.roll(g8, half, axis=1)
            qbd = jnp.concatenate(
                [jnp.where(lo4, g8[:quarter], 0.0), jnp.where(lo4, g8r[:quarter], 0.0),
                 jnp.where(lo4, 0.0, g8r[quarter:]), jnp.where(lo4, 0.0, g8[quarter:])],
                axis=0)
            l = jnp.dot(qbd, kb[feat, :], preferred_element_type=F32) + bias_ref[hs, :]
            s = sink_ref[hs, 0:1]
            m = jnp.maximum(jnp.max(l, axis=-1, keepdims=True), s)
            pe = jnp.exp2(l - m)
            den = jnp.sum(pe, axis=-1, keepdims=True) + jnp.exp2(s - m)
            o2 = lax.dot_general(pe, vb[feat, :], contract_last,
                                 preferred_element_type=F32) / den
            o2r = pltpu.roll(o2, half, axis=1)
            og = jnp.concatenate(
                [jnp.where(lo4, o2[:quarter], o2r[quarter:2 * quarter]),
                 jnp.where(lo4, o2r[2 * quarter:3 * quarter], o2[3 * quarter:])], axis=0)
            o_ref[i, groups, :] = og * _silu(gate_ref[i, groups, :])


def _swa_sample(q, kv, ckt, cvt, gate, bias_s, sinks_rep):
    n = q.shape[0]
    bb = SAMPLE_ATT_BATCH
    n_groups = ATT_WIDTH // LANES
    blk3 = lambda i: (i, 0, 0)
    whole = lambda i: (0, 0)
    return pl.pallas_call(
        _swa_sample_kernel,
        grid=(n // bb,),
        in_specs=[pl.BlockSpec((bb, n_groups, LANES), blk3),
                  pl.BlockSpec((n, 2 * ATT_KV_WIDTH), whole),
                  pl.BlockSpec((bb, ATT_KV_WIDTH, WINDOW), blk3),
                  pl.BlockSpec((bb, ATT_KV_WIDTH, WINDOW), blk3),
                  pl.BlockSpec((bb, n_groups, LANES), blk3),
                  pl.BlockSpec((ATT_HEADS, WINDOW), whole),
                  pl.BlockSpec((ATT_HEADS, LANES), whole)],
        out_specs=[pl.BlockSpec((bb, n_groups, LANES), blk3),
                   pl.BlockSpec((bb, ATT_KV_WIDTH, WINDOW), blk3),
                   pl.BlockSpec((bb, ATT_KV_WIDTH, WINDOW), blk3)],
        out_shape=[jax.ShapeDtypeStruct((n, n_groups, LANES), F32),
                   jax.ShapeDtypeStruct((n, ATT_KV_WIDTH, WINDOW), F32),
                   jax.ShapeDtypeStruct((n, ATT_KV_WIDTH, WINDOW), F32)],
        scratch_shapes=[pltpu.VMEM((2 * ATT_KV_WIDTH, n), F32)],
        compiler_params=_params(1),
        name="swa_sample",
    )(q, kv, ckt, cvt, gate, bias_s, sinks_rep)


def _group_norm_gate(o, gate):
    mu = jnp.mean(o, axis=-1, keepdims=True)
    d = o - mu
    var = jnp.mean(d * d, axis=-1, keepdims=True)
    return d * lax.rsqrt(var + EPS) * _silu(gate)


def _ret_prompt_kernel(g_block_ref, q_ref, kt_ref, v_ref, gate_ref, x_ref, wout_ref, gf_ref,
                       y_ref, r_ref, og_ref):
    c = pl.program_id(1)

    @pl.when(c == 0)
    def _():
        r_ref[...] = jnp.zeros_like(r_ref)

    causal = (lax.broadcasted_iota(jnp.int32, (RET_BLOCK, RET_BLOCK), 1)
              <= lax.broadcasted_iota(jnp.int32, (RET_BLOCK, RET_BLOCK), 0))
    for h in range(RET_HEADS):
        qcols = slice(h * RET_QK_DIM, (h + 1) * RET_QK_DIM)
        vcols = slice(h * RET_V_DIM, (h + 1) * RET_V_DIM)
        for blk in range(RET_ROWS // RET_BLOCK):
            rows = slice(blk * RET_BLOCK, (blk + 1) * RET_BLOCK)
            qh = q_ref[rows, qcols]
            kt = kt_ref[qcols, rows]
            vh = v_ref[rows, vcols]
            state = r_ref[0, h]
            scores = jnp.where(causal, jnp.dot(qh, kt, preferred_element_type=F32), 0.0)
            o = (jnp.dot(scores.astype(BF16), vh, preferred_element_type=F32)
                 + jnp.dot(qh, state.astype(BF16), preferred_element_type=F32))
            r_ref[0, h] = g_block_ref[h] * (state + jnp.dot(kt, vh, preferred_element_type=F32))
            og_ref[rows, :] = _group_norm_gate(
                o, gate_ref[rows, vcols].astype(F32)).astype(og_ref.dtype)
        part = jnp.dot(og_ref[...], wout_ref[vcols, :], preferred_element_type=F32)
        if h == 0:
            y_ref[...] = x_ref[...] + part
        elif h < RET_HEADS - 1:
            y_ref[...] += part
        else:
            y_ref[...] = _rmsnorm(y_ref[...] + part, gf_ref[...])


def _ret_prompt(q, kt, v, gate, x, w_out, final_g, batch, seq):
    m = batch * seq
    nt = seq // RET_ROWS
    g_block = jnp.asarray(_decay_scales_np(RET_BLOCK, RET_BLOCK)[2])
    rows = lambda b, c: (b * nt + c, 0)
    return pl.pallas_call(
        _ret_prompt_kernel,
        grid=(batch, nt),
        in_specs=[pl.BlockSpec(memory_space=pltpu.SMEM),
                  pl.BlockSpec((RET_ROWS, RET_QK_WIDTH), rows),
                  pl.BlockSpec((RET_QK_WIDTH, RET_ROWS), lambda b, c: (0, b * nt + c)),
                  pl.BlockSpec((RET_ROWS, RET_WIDTH), rows),
                  pl.BlockSpec((RET_ROWS, RET_WIDTH), rows),
                  pl.BlockSpec((RET_ROWS, D_MODEL), rows),
                  pl.BlockSpec((RET_WIDTH, D_MODEL), lambda b, c: (0, 0)),
                  pl.BlockSpec((1, D_MODEL), lambda b, c: (0, 0))],
        out_specs=[pl.BlockSpec((RET_ROWS, D_MODEL), rows),
                   pl.BlockSpec((1, RET_HEADS, RET_QK_DIM, RET_V_DIM), lambda b, c: (b, 0, 0, 0))],
        out_shape=[jax.ShapeDtypeStruct((m, D_MODEL), F32),
                   jax.ShapeDtypeStruct((batch, RET_HEADS, RET_QK_DIM, RET_V_DIM), F32)],
        scratch_shapes=[pltpu.VMEM((RET_ROWS, RET_V_DIM), BF16)],
        compiler_params=_params(2),
        name="ret_prompt",
    )(g_block, q, kt, v, gate, x, w_out, final_g)


def _ret_sample_kernel(q_ref, kt_ref, v_ref, gate_ref, st_ref, og_ref, sto_ref, qt_ref):
    step = pl.program_id(0)

    @pl.when(step == 0)
    def _():
        qt_ref[...] = q_ref[...].T

    lane = lax.broadcasted_iota(jnp.int32, (RET_QK_DIM, LANES), 1)
    gammas = _gammas()
    for i in range(SAMPLE_RET_BATCH):
        b = step * SAMPLE_RET_BATCH + i
        pick = lane == b
        for h in range(RET_HEADS):
            rows = slice(h * RET_QK_DIM, (h + 1) * RET_QK_DIM)
            qc = jnp.sum(jnp.where(pick, qt_ref[rows, :], 0.0), axis=1, keepdims=True)
            kc = jnp.sum(jnp.where(pick, kt_ref[rows, :], 0.0), axis=1, keepdims=True)
            vcols = slice(h * RET_V_DIM, (h + 1) * RET_V_DIM)
            vrow = v_ref[pl.ds(b, 1), vcols]
            state = st_ref[i, h]
            o = (jnp.sum(qc * state, axis=0, keepdims=True)
                 + jnp.sum(qc * kc, axis=0, keepdims=True) * vrow)
            sto_ref[i, h] = gammas[h] * (state + kc * vrow)
            og_ref[pl.ds(b, 1), vcols] = _group_norm_gate(o, gate_ref[pl.ds(b, 1), vcols])


def _ret_sample(q, kt, v, gate, state):
    n = q.shape[0]
    bb = SAMPLE_RET_BATCH
    whole = lambda i: (0, 0)
    st_spec = pl.BlockSpec((bb, RET_HEADS, RET_QK_DIM, RET_V_DIM), lambda i: (i, 0, 0, 0))
    return pl.pallas_call(
        _ret_sample_kernel,
        grid=(n // bb,),
        in_specs=[pl.BlockSpec((n, RET_QK_WIDTH), whole),
                  pl.BlockSpec((RET_QK_WIDTH, n), whole),
                  pl.BlockSpec((n, RET_WIDTH), whole),
                  pl.BlockSpec((n, RET_WIDTH), whole),
                  st_spec],
        out_specs=[pl.BlockSpec((n, RET_WIDTH), whole), st_spec],
        out_shape=[jax.ShapeDtypeStruct((n, RET_WIDTH), F32),
                   jax.ShapeDtypeStruct(state.shape, F32)],
        scratch_shapes=[pltpu.VMEM((RET_QK_WIDTH, n), F32)],
        compiler_params=_params(1),
        name="ret_sample",
    )(q, kt, v, gate, state)


def kernel(x_prompt, x_sample, cache_swa_k, cache_swa_v, state_ret, norm_g, final_norm_g, rel_bias,
           w_in_attn, attn_sinks, w_out_attn, w_in_ret, w_out_ret):
    batch, seq, _ = x_prompt.shape
    n_s = x_sample.shape[0]
    xp = x_prompt.reshape(batch * seq, D_MODEL)
    xs = x_sample.reshape(n_s, D_MODEL)
    g0 = norm_g[0].reshape(1, D_MODEL)
    g1 = norm_g[1].reshape(1, D_MODEL)
    gf = final_norm_g.reshape(1, D_MODEL)
    w_in_a = w_in_attn[0].astype(BF16)
    w_out_a = w_out_attn[0].astype(BF16)
    w_in_r = w_in_ret[0].astype(BF16)
    w_out_r = w_out_ret[0].astype(BF16)
    sinks = attn_sinks[0] * LOG2E

    bias = _bias_table(rel_bias)

    q, kv, gate, kvpad = _attn_inproj(xp, g0, w_in_a, BF16, with_padded_kv=True)
    og = _swa_prompt(sinks, q, kvpad, gate, bias, batch, seq)
    xp1 = _outproj(og, w_out_a, xp)
    kv_last = kv.reshape(batch, seq, 2 * ATT_KV_WIDTH)[:, seq - WINDOW:]
    k_prompt = kv_last[..., :ATT_KV_WIDTH].reshape(1, batch, WINDOW, ATT_KV_HEADS, ATT_HEAD_DIM)
    v_prompt = kv_last[..., ATT_KV_WIDTH:].reshape(1, batch, WINDOW, ATT_KV_HEADS, ATT_HEAD_DIM)

    qs, kvs, gates = _attn_inproj(xs, g0, w_in_a, F32, with_padded_kv=False)
    order = _sample_head_order()
    bias_s = bias[1, :, WINDOW - 1, :][order]
    sinks_rep = jnp.broadcast_to(sinks[order][:, None], (ATT_HEADS, LANES))
    to_feature_major = lambda c: jnp.transpose(c, (0, 2, 3, 1)).reshape(n_s, ATT_KV_WIDTH, WINDOW)
    to_window_major = lambda c: jnp.transpose(
        c.reshape(n_s, ATT_KV_HEADS, ATT_HEAD_DIM, WINDOW), (0, 3, 1, 2))[None]
    n_groups = ATT_WIDTH // LANES
    ogs, ckt_new, cvt_new = _swa_sample(
        qs.reshape(n_s, n_groups, LANES), kvs,
        to_feature_major(cache_swa_k[0]), to_feature_major(cache_swa_v[0]),
        gates.reshape(n_s, n_groups, LANES), bias_s, sinks_rep)
    xs1 = _outproj(ogs.reshape(n_s, ATT_WIDTH), w_out_a, xs)
    k_sample = to_window_major(ckt_new)
    v_sample = to_window_major(cvt_new)

    w_kt = _transposed_columns(w_in_ret[0], RET_QK_WIDTH, RET_QK_WIDTH, BF16)
    q, kt, v, gate = _ret_inproj(xp1, g1, w_in_r, w_kt, np.arange(seq), RET_BLOCK, BF16)
    y_prompt, r_prompt = _ret_prompt(q, kt, v, gate, xp1, w_out_r, gf, batch, seq)
    y_prompt = y_prompt.reshape(batch, seq, D_MODEL)

    q_s, kt_s, v_s, gate_s = _ret_inproj(xs1, g1, w_in_r, w_kt, np.full((n_s,), PAST_LEN), 1, F32)
    og_s, r_sample = _ret_sample(q_s, kt_s, v_s, gate_s, state_ret[0])
    y_sample = _outproj(og_s, w_out_r, xs1, gf).reshape(n_s, 1, D_MODEL)

    return (y_prompt, y_sample, k_prompt, v_prompt, r_prompt[None],
            k_sample, v_sample, r_sample[None])
```

```python
import functools
import math

import numpy as np
import jax
import jax.numpy as jnp
from jax import lax
from jax.experimental import pallas as pl
from jax.experimental.pallas import tpu as pltpu

F32 = jnp.float32
BF16 = jnp.bfloat16

D_MODEL = 1024
PAST_LEN = 16384
ATT_HEADS = 32
ATT_KV_HEADS = 4
ATT_GROUP = ATT_HEADS // ATT_KV_HEADS
ATT_HEAD_DIM = 64
ATT_WIDTH = ATT_HEADS * ATT_HEAD_DIM
ATT_KV_WIDTH = ATT_KV_HEADS * ATT_HEAD_DIM
ATT_IN = 2 * ATT_WIDTH + 2 * ATT_KV_WIDTH
WINDOW = 128
NUM_BUCKETS = 32
MAX_DISTANCE = 128
RET_HEADS = 4
RET_QK_DIM = 256
RET_V_DIM = 512
RET_QK_WIDTH = RET_HEADS * RET_QK_DIM
RET_WIDTH = RET_HEADS * RET_V_DIM
RET_IN = 2 * RET_QK_WIDTH + 2 * RET_WIDTH
ROPE_BASE = 10000.0
EPS = 1e-6
NEG = -1e30
LOG2E = math.log2(math.e)

LANES = 128
PADDED_KV_VARIANTS = 4
PADDED_KV_WIDTH = ATT_KV_HEADS * PADDED_KV_VARIANTS * LANES
PROJ_ROWS = 512
PROJ_COLS = 512
OUT_ROWS = 1024
ATT_ROWS = 512
RET_BLOCK = 256
RET_ROWS = 512
SAMPLE_ATT_BATCH = 8
SAMPLE_RET_BATCH = 2
VMEM_LIMIT = 60 * 1024 * 1024


def _params(n_axes):
    return pltpu.CompilerParams(
        dimension_semantics=("arbitrary",) * n_axes,
        vmem_limit_bytes=VMEM_LIMIT)


def _resident(shape, index_map):
    return pl.BlockSpec(shape, index_map, pipeline_mode=pl.Buffered(1))


def _silu(x):
    hx = 0.5 * x
    return hx + hx * jnp.tanh(hx)


def _rmsnorm(x, g):
    return x * lax.rsqrt(jnp.mean(x * x, axis=-1, keepdims=True) + EPS) * g


def _t5_bucket_np(rel):
    n = np.maximum(rel, 0)
    max_exact = NUM_BUCKETS // 2
    nf = np.maximum(n, 1).astype(np.float64)
    large = max_exact + (np.log(nf / max_exact) / math.log(MAX_DISTANCE / max_exact)
                         * (NUM_BUCKETS - max_exact)).astype(np.int32)
    large = np.minimum(large, NUM_BUCKETS - 1)
    return np.where(n < max_exact, n, large).astype(np.int32)


def _bucket_map_np():
    i = np.arange(WINDOW)[:, None]
    j = np.arange(WINDOW)[None, :]
    rel = np.where(j <= i, i - j, i - j + WINDOW)
    return _t5_bucket_np(rel).astype(np.int32)


def _gammas():
    return [1.0 - 2.0 ** (-5 - h) for h in range(RET_HEADS)]


def _decay_scales_np(rows, block):
    lg = np.log1p(-np.exp2(-5.0 - np.arange(RET_HEADS, dtype=np.float64)))
    i1 = (np.arange(rows) % block + 1.0)[None, :] * lg[:, None]
    q_scale = np.repeat(np.exp(i1)[:, :, None], LANES, axis=2)
    k_scale = np.repeat((np.exp(-i1) * RET_QK_DIM ** -0.5)[:, None, :], LANES, axis=1)
    return (q_scale.astype(np.float32), k_scale.astype(np.float32),
            np.exp(block * lg).astype(np.float32))


def _rope_tables_np(pos):
    half = RET_QK_DIM // 2
    inv = ROPE_BASE ** (-np.arange(half, dtype=np.float64) / half)
    ang = np.asarray(pos, np.float64)[:, None] * inv[None, :]
    return np.cos(ang).astype(np.float32), np.sin(ang).astype(np.float32)


def _bias_table_kernel(rb_ref, bucket_ref, out_ref):
    h = pl.program_id(0)
    bk = bucket_ref[...]
    acc = jnp.zeros(bk.shape, F32)
    for b in range(NUM_BUCKETS):
        acc = jnp.where(bk == b, rb_ref[b, h] * LOG2E, acc)
    row = lax.broadcasted_iota(jnp.int32, bk.shape, 0)
    col = lax.broadcasted_iota(jnp.int32, bk.shape, 1)
    out_ref[1, 0] = acc
    out_ref[0, 0] = jnp.where(col > row, NEG, acc)


def _bias_table(rel_bias):
    bucket = jnp.asarray(_bucket_map_np())
    return pl.pallas_call(
        _bias_table_kernel,
        grid=(ATT_HEADS,),
        in_specs=[pl.BlockSpec(memory_space=pltpu.SMEM),
                  pl.BlockSpec((WINDOW, WINDOW), lambda h: (0, 0))],
        out_specs=pl.BlockSpec((2, 1, WINDOW, WINDOW), lambda h: (0, h, 0, 0)),
        out_shape=jax.ShapeDtypeStruct((2, ATT_HEADS, WINDOW, WINDOW), F32),
        compiler_params=_params(1),
        name="bias_table",
    )(rel_bias, bucket)


def _transpose_cast_kernel(w_ref, o_ref):
    o_ref[...] = w_ref[...].T.astype(o_ref.dtype)


def _transposed_columns(w, col0, ncols, dtype):
    rows = w.shape[0]
    blk = 2 * LANES
    return pl.pallas_call(
        _transpose_cast_kernel,
        grid=(ncols // blk,),
        in_specs=[pl.BlockSpec((rows, blk), lambda i: (0, col0 // blk + i))],
        out_specs=pl.BlockSpec((blk, rows), lambda i: (i, 0)),
        out_shape=jax.ShapeDtypeStruct((ncols, rows), dtype),
        compiler_params=_params(1),
        name="transpose_cast",
    )(w)


def _pad_pair(x, x_rolled, lo, e):
    if e == 0:
        return jnp.where(lo, x, 0.0), jnp.where(lo, 0.0, x_rolled)
    return jnp.where(lo, x_rolled, 0.0), jnp.where(lo, 0.0, x)


def _attn_inproj_kernel(x_ref, g_ref, w_ref, q_ref, kv_ref, gate_ref, kvpad_ref=None):
    h = _rmsnorm(x_ref[...], g_ref[...]).astype(BF16)
    for c in range(0, ATT_IN, PROJ_COLS):
        r = jnp.dot(h, w_ref[:, c:c + PROJ_COLS], preferred_element_type=F32)
        if c < ATT_WIDTH:
            q_ref[:, c:c + PROJ_COLS] = (r * (ATT_HEAD_DIM ** -0.5 * LOG2E)).astype(q_ref.dtype)
        elif c < ATT_WIDTH + 2 * ATT_KV_WIDTH:
            kv_ref[...] = r
            if kvpad_ref is not None:
                lo = lax.broadcasted_iota(jnp.int32, (r.shape[0], LANES), 1) < ATT_HEAD_DIM
                for pair in range(ATT_KV_HEADS // 2):
                    for is_v in range(2):
                        c1 = is_v * ATT_KV_WIDTH + pair * LANES
                        x2 = r[:, c1:c1 + LANES]
                        x2r = pltpu.roll(x2, ATT_HEAD_DIM, axis=1)
                        for e in range(2):
                            g0 = ((2 * pair + e) * PADDED_KV_VARIANTS + 2 * is_v) * LANES
                            lo_half, hi_half = _pad_pair(x2, x2r, lo, e)
                            kvpad_ref[:, g0:g0 + LANES] = lo_half.astype(kvpad_ref.dtype)
                            kvpad_ref[:, g0 + LANES:g0 + 2 * LANES] = hi_half.astype(kvpad_ref.dtype)
        else:
            o = c - ATT_WIDTH - 2 * ATT_KV_WIDTH
            gate_ref[:, o:o + PROJ_COLS] = r.astype(gate_ref.dtype)


def _attn_inproj(x, g, w, out_dtype, with_padded_kv):
    m = x.shape[0]
    tm = min(PROJ_ROWS, m)
    row = lambda i: (i, 0)
    out_specs = [pl.BlockSpec((tm, ATT_WIDTH), row),
                 pl.BlockSpec((tm, 2 * ATT_KV_WIDTH), row),
                 pl.BlockSpec((tm, ATT_WIDTH), row)]
    out_shape = [jax.ShapeDtypeStruct((m, ATT_WIDTH), out_dtype),
                 jax.ShapeDtypeStruct((m, 2 * ATT_KV_WIDTH), F32),
                 jax.ShapeDtypeStruct((m, ATT_WIDTH), out_dtype)]
    if with_padded_kv:
        out_specs.append(pl.BlockSpec((tm, PADDED_KV_WIDTH), row))
        out_shape.append(jax.ShapeDtypeStruct((m, PADDED_KV_WIDTH), BF16))
    return pl.pallas_call(
        _attn_inproj_kernel,
        grid=(m // tm,),
        in_specs=[pl.BlockSpec((tm, D_MODEL), row),
                  pl.BlockSpec((1, D_MODEL), lambda i: (0, 0)),
                  pl.BlockSpec((D_MODEL, ATT_IN), lambda i: (0, 0))],
        out_specs=out_specs,
        out_shape=out_shape,
        compiler_params=_params(1),
        name="attn_inproj",
    )(x, g, w)


def _ret_sample_pieces(step, qt_ref, kt_ref, v_ref, gate_ref, st_ref, og_ref, sto_ref):
    lane = lax.broadcasted_iota(jnp.int32, (RET_QK_DIM, LANES), 1)
    gammas = _gammas()

    def piece(i, h):
        b = step * SAMPLE_RET_BATCH + i
        pick = lane == b
        rows = slice(h * RET_QK_DIM, (h + 1) * RET_QK_DIM)
        qc = jnp.sum(jnp.where(pick, qt_ref[rows, :], 0.0), axis=1, keepdims=True)
        kc = jnp.sum(jnp.where(pick, kt_ref[rows, :], 0.0), axis=1, keepdims=True)
        vcols = slice(h * RET_V_DIM, (h + 1) * RET_V_DIM)
        vrow = v_ref[pl.ds(b, 1), vcols]
        state = st_ref[i, h]
        o = (jnp.sum(qc * state, axis=0, keepdims=True)
             + jnp.sum(qc * kc, axis=0, keepdims=True) * vrow)
        sto_ref[i, h] = gammas[h] * (state + kc * vrow)
        og_ref[pl.ds(b, 1), vcols] = _group_norm_gate(o, gate_ref[pl.ds(b, 1), vcols])

    return [functools.partial(piece, i, h)
            for i in range(SAMPLE_RET_BATCH) for h in range(RET_HEADS)]


def _ret_inproj_kernel(*refs, with_sample):
    (x_ref, g_ref, w_ref, wkt_ref, cos_ref, sin_ref, cost_ref, sint_ref, qsc_ref, ksc_ref) = refs[:10]
    side_work = []
    if with_sample:
        qs_ref, kts_ref, vs_ref, gates_ref, st_ref = refs[10:15]
        q_ref, kt_ref, v_ref, gate_ref, ogs_ref, sto_ref, qts_ref = refs[15:]
        step = pl.program_id(0)

        @pl.when(step == 0)
        def _():
            qts_ref[...] = qs_ref[...].T

        side_work = _ret_sample_pieces(step, qts_ref, kts_ref, vs_ref, gates_ref, st_ref,
                                       ogs_ref, sto_ref)
    else:
        q_ref, kt_ref, v_ref, gate_ref = refs[10:]

    def interleave():
        if side_work:
            side_work.pop(0)()

    h = _rmsnorm(x_ref[...], g_ref[...]).astype(BF16)
    half = RET_QK_DIM // 2
    contract_last = (((1,), (1,)), ((), ()))
    cos = cos_ref[...]
    sin = sin_ref[...]
    for c in range(0, RET_QK_WIDTH, PROJ_COLS):
        interleave()
        r = jnp.dot(h, w_ref[:, c:c + PROJ_COLS], preferred_element_type=F32)
        for o in range(0, PROJ_COLS, RET_QK_DIM):
            scale = qsc_ref[(c + o) // RET_QK_DIM]
            x1 = r[:, o:o + half]
            x2 = r[:, o + half:o + RET_QK_DIM]
            q_ref[:, c + o:c + o + half] = ((x1 * cos - x2 * sin) * scale).astype(q_ref.dtype)
            q_ref[:, c + o + half:c + o + RET_QK_DIM] = (
                (x1 * sin + x2 * cos) * scale).astype(q_ref.dtype)
    cos = cost_ref[...]
    sin = sint_ref[...]
    for c in range(0, RET_QK_WIDTH, PROJ_COLS):
        interleave()
        r = lax.dot_general(wkt_ref[c:c + PROJ_COLS, :], h, contract_last,
                            preferred_element_type=F32)
        for o in range(0, PROJ_COLS, RET_QK_DIM):
            scale = ksc_ref[(c + o) // RET_QK_DIM]
            x1 = r[o:o + half, :]
            x2 = r[o + half:o + RET_QK_DIM, :]
            kt_ref[c + o:c + o + half, :] = ((x1 * cos - x2 * sin) * scale).astype(kt_ref.dtype)
            kt_ref[c + o + half:c + o + RET_QK_DIM, :] = (
                (x1 * sin + x2 * cos) * scale).astype(kt_ref.dtype)
    for c in range(0, RET_WIDTH, PROJ_COLS):
        interleave()
        v_ref[:, c:c + PROJ_COLS] = jnp.dot(
            h, w_ref[:, 2 * RET_QK_WIDTH + c:2 * RET_QK_WIDTH + c + PROJ_COLS],
            preferred_element_type=F32).astype(v_ref.dtype)
    for c in range(0, RET_WIDTH, PROJ_COLS):
        interleave()
        w0 = 2 * RET_QK_WIDTH + RET_WIDTH + c
        gate_ref[:, c:c + PROJ_COLS] = jnp.dot(
            h, w_ref[:, w0:w0 + PROJ_COLS], preferred_element_type=F32).astype(gate_ref.dtype)
    while side_work:
        interleave()


def _ret_inproj(x, g, w, wkt, pos, block, out_dtype, sample=None):
    m = x.shape[0]
    tm = min(PROJ_ROWS, m)
    n_pos = len(pos) // tm
    half = RET_QK_DIM // 2
    cos, sin = _rope_tables_np(pos)
    q_scale, k_scale, _ = _decay_scales_np(tm, block)
    row = lambda i: (i, 0)
    const2 = lambda i: (0, 0)
    const3 = lambda i: (0, 0, 0)
    in_specs = [pl.BlockSpec((tm, D_MODEL), row),
                pl.BlockSpec((1, D_MODEL), const2),
                _resident((D_MODEL, RET_IN), const2),
                _resident((RET_QK_WIDTH, D_MODEL), const2),
                pl.BlockSpec((tm, half), lambda i: (i % n_pos, 0)),
                pl.BlockSpec((tm, half), lambda i: (i % n_pos, 0)),
                pl.BlockSpec((half, tm), lambda i: (0, i % n_pos)),
                pl.BlockSpec((half, tm), lambda i: (0, i % n_pos)),
                _resident((RET_HEADS, tm, LANES), const3),
                _resident((RET_HEADS, LANES, tm), const3)]
    out_specs = [pl.BlockSpec((tm, RET_QK_WIDTH), row),
                 pl.BlockSpec((RET_QK_WIDTH, tm), lambda i: (0, i)),
                 pl.BlockSpec((tm, RET_WIDTH), row),
                 pl.BlockSpec((tm, RET_WIDTH), row)]
    out_shape = [jax.ShapeDtypeStruct((m, RET_QK_WIDTH), out_dtype),
                 jax.ShapeDtypeStruct((RET_QK_WIDTH, m), out_dtype),
                 jax.ShapeDtypeStruct((m, RET_WIDTH), out_dtype),
                 jax.ShapeDtypeStruct((m, RET_WIDTH), out_dtype)]
    args = [x, g, w, wkt, jnp.asarray(cos), jnp.asarray(sin), jnp.asarray(cos.T),
            jnp.asarray(sin.T), jnp.asarray(q_scale), jnp.asarray(k_scale)]
    scratch_shapes = []
    if sample is not None:
        q_s, kt_s, v_s, gate_s, state = sample
        n = q_s.shape[0]
        assert n == SAMPLE_RET_BATCH * (m // tm), (n, m, tm)
        st_spec = pl.BlockSpec((SAMPLE_RET_BATCH, RET_HEADS, RET_QK_DIM, RET_V_DIM),
                               lambda i: (i, 0, 0, 0))
        in_specs += [_resident((n, RET_QK_WIDTH), const2), _resident((RET_QK_WIDTH, n), const2),
                     _resident((n, RET_WIDTH), const2), _resident((n, RET_WIDTH), const2), st_spec]
        out_specs += [pl.BlockSpec((n, RET_WIDTH), const2), st_spec]
        out_shape += [jax.ShapeDtypeStruct((n, RET_WIDTH), F32),
                      jax.ShapeDtypeStruct(state.shape, F32)]
        args += [q_s, kt_s, v_s, gate_s, state]
        scratch_shapes = [pltpu.VMEM((RET_QK_WIDTH, n), F32)]
    return pl.pallas_call(
        functools.partial(_ret_inproj_kernel, with_sample=sample is not None),
        grid=(m // tm,),
        in_specs=in_specs,
        out_specs=out_specs,
        out_shape=out_shape,
        scratch_shapes=scratch_shapes,
        compiler_params=_params(1),
        name="ret_inproj",
    )(*args)


def _outproj_kernel(o_ref, w_ref, x_ref, y_ref):
    y_ref[...] = x_ref[...] + jnp.dot(o_ref[...].astype(BF16), w_ref[...],
                                      preferred_element_type=F32)


def _outproj_norm_kernel(o_ref, w_ref, x_ref, g_ref, y_ref):
    y = x_ref[...] + jnp.dot(o_ref[...].astype(BF16), w_ref[...], preferred_element_type=F32)
    y_ref[...] = _rmsnorm(y, g_ref[...])


def _outproj(o, w, x, final_g=None):
    m, width = o.shape
    tm = min(OUT_ROWS, m)
    row = lambda i: (i, 0)
    in_specs = [pl.BlockSpec((tm, width), row),
                pl.BlockSpec((width, D_MODEL), lambda i: (0, 0)),
                pl.BlockSpec((tm, D_MODEL), row)]
    args = [o, w, x]
    body = _outproj_kernel
    if final_g is not None:
        in_specs.append(pl.BlockSpec((1, D_MODEL), lambda i: (0, 0)))
        args.append(final_g)
        body = _outproj_norm_kernel
    return pl.pallas_call(
        body,
        grid=(m // tm,),
        in_specs=in_specs,
        out_specs=pl.BlockSpec((tm, D_MODEL), row),
        out_shape=jax.ShapeDtypeStruct((m, D_MODEL), F32),
        compiler_params=_params(1),
        name="outproj",
    )(*args)


def _swa_prompt_kernel(sink_ref, q_ref, kvc_ref, kvp_ref, gate_ref, bias_ref, out_ref):
    j = pl.program_id(1)
    half = ATT_HEAD_DIM
    pairs_per_kv = ATT_GROUP // 2
    kv_group_cols = ATT_GROUP * ATT_HEAD_DIM
    contract_last = (((1,), (1,)), ((), ()))
    lo1 = lax.broadcasted_iota(jnp.int32, (WINDOW, LANES), 1) < half
    own = (lax.broadcasted_iota(jnp.int32, (WINDOW, WINDOW), 1)
           <= lax.broadcasted_iota(jnp.int32, (WINDOW, WINDOW), 0))
    units =[(blk, kvh) for blk in range(ATT_ROWS // WINDOW) for kvh in range(ATT_KV_HEADS)]

    def keys_values(blk, kvh):
        r0 = blk * WINDOW
        out = []
        for variant in range(PADDED_KV_VARIANTS):
            c = (kvh * PADDED_KV_VARIANTS + variant) * LANES
            prev = kvp_ref[:, c:c + LANES] if blk == 0 else kvc_ref[r0 - WINDOW:r0, c:c + LANES]
            out.append(jnp.concatenate([prev, kvc_ref[r0:r0 + WINDOW, c:c + LANES]], axis=0))
        return out

    def scores(blk, kvh):
        rows = slice(blk * WINDOW, (blk + 1) * WINDOW)
        c0 = kvh * kv_group_cols
        k_a, k_b, v_a, v_b = keys_values(blk, kvh)
        qs = jnp.concatenate(
            [q_ref[rows, c0 + p * LANES:c0 + (p + 1) * LANES] for p in range(pairs_per_kv)],
            axis=0)
        logits = (lax.dot_general(qs, k_a, contract_last, preferred_element_type=F32),
                  lax.dot_general(qs, k_b, contract_last, preferred_element_type=F32))
        return logits, v_a, v_b

    def softmax(blk, kvh, logits):
        sel = jnp.where(j == 0, 0, 1) if blk == 0 else 1
        probs = ([], [])
        inv = ([], [])
        for p in range(pairs_per_kv):
            for st in range(2):
                hd = kvh * ATT_GROUP + 2 * p + st
                lg = logits[st][p * WINDOW:(p + 1) * WINDOW, :]
                l = jnp.where(own, lg[:, WINDOW:], lg[:, :WINDOW]) + bias_ref[sel, hd]
                s = sink_ref[hd]
                m = jnp.maximum(jnp.max(l, axis=-1, keepdims=True), s)
                pe = jnp.exp2(l - m)
                den = jnp.sum(pe, axis=-1, keepdims=True) + jnp.exp2(s - m)
                probs[st].append(jnp.concatenate(
                    [jnp.where(own, 0.0, pe).astype(BF16),
                     jnp.where(own, pe, 0.0).astype(BF16)], axis=1))
                inv[st].append(1.0 / den)
        return jnp.concatenate(probs[0], axis=0), jnp.concatenate(probs[1], axis=0), inv

    def output(blk, kvh, p_a, p_b, inv, v_a, v_b):
        rows = slice(blk * WINDOW, (blk + 1) * WINDOW)
        c0 = kvh * kv_group_cols
        o = (jnp.dot(p_a, v_a, preferred_element_type=F32)
             + jnp.dot(p_b, v_b, preferred_element_type=F32))
        for p in range(pairs_per_kv):
            cols = slice(c0 + p * LANES, c0 + (p + 1) * LANES)
            scale = jnp.where(lo1, inv[0][p], inv[1][p])
            gt = gate_ref[rows, cols].astype(F32)
            og = o[p * WINDOW:(p + 1) * WINDOW, :] * scale * _silu(gt)
            out_ref[rows, cols] = og.astype(out_ref.dtype)

    for unit in units:
        logits, v_a, v_b = scores(*unit)
        p_a, p_b, inv = softmax(*unit, logits)
        output(*unit, p_a, p_b, inv, v_a, v_b)


def _swa_prompt(sinks, q, kvpad, gate, bias, batch, seq):
    m = batch * seq
    nt = seq // ATT_ROWS
    blocks_per_tile = ATT_ROWS // WINDOW
    blocks_per_seq = seq // WINDOW
    tile = lambda b, j: (b * nt + j, 0)
    prev = lambda b, j: (b * blocks_per_seq + jnp.maximum(j * blocks_per_tile - 1, 0), 0)
    return pl.pallas_call(
        _swa_prompt_kernel,
        grid=(batch, nt),
        in_specs=[pl.BlockSpec(memory_space=pltpu.SMEM),
                  pl.BlockSpec((ATT_ROWS, ATT_WIDTH), tile),
                  pl.BlockSpec((ATT_ROWS, PADDED_KV_WIDTH), tile),
                  pl.BlockSpec((WINDOW, PADDED_KV_WIDTH), prev),
                  pl.BlockSpec((ATT_ROWS, ATT_WIDTH), tile),
                  pl.BlockSpec((2, ATT_HEADS, WINDOW, WINDOW), lambda b, j: (0, 0, 0, 0))],
        out_specs=pl.BlockSpec((ATT_ROWS, ATT_WIDTH), tile),
        out_shape=jax.ShapeDtypeStruct((m, ATT_WIDTH), BF16),
        compiler_params=_params(2),
        name="swa_prompt",
    )(sinks, q, kvpad, kvpad, gate, bias)


def _sample_head_order():
    order = []
    for pair in range(ATT_KV_HEADS // 2):
        base = pair * 2 * ATT_GROUP
        for kv in range(2):
            for half in range(2):
                order += [base + kv * ATT_GROUP + 2 * g + half for g in range(ATT_GROUP // 2)]
    return np.asarray(order, np.int32)


def _swa_sample_kernel(q_ref, kv_ref, ckt_ref, cvt_ref, gate_ref, bias_ref, sink_ref,
                       o_ref, cko_ref, cvo_ref, kvt_ref):
    step = pl.program_id(0)

    @pl.when(step == 0)
    def _():
        kvt_ref[...] = kv_ref[...].T

    half = ATT_HEAD_DIM
    quarter = ATT_GROUP // 2
    contract_last = (((1,), (1,)), ((), ()))
    newest = lax.broadcasted_iota(jnp.int32, (ATT_KV_WIDTH, WINDOW), 1) == WINDOW - 1
    seq_lane = lax.broadcasted_iota(jnp.int32, (2 * ATT_KV_WIDTH, LANES), 1)
    lo4 = lax.broadcasted_iota(jnp.int32, (quarter, LANES), 1) < half
    for i in range(SAMPLE_ATT_BATCH):
        b = step * SAMPLE_ATT_BATCH + i
        new_col = jnp.sum(jnp.where(seq_lane == b, kvt_ref[...], 0.0), axis=1, keepdims=True)
        kb = jnp.where(newest, new_col[:ATT_KV_WIDTH], pltpu.roll(ckt_ref[i], WINDOW - 1, axis=1))
        vb = jnp.where(newest, new_col[ATT_KV_WIDTH:], pltpu.roll(cvt_ref[i], WINDOW - 1, axis=1))
        cko_ref[i] = kb
        cvo_ref[i] = vb
        for pair in range(ATT_KV_HEADS // 2):
            groups = slice(pair * ATT_GROUP, (pair + 1) * ATT_GROUP)
            hs = slice(pair * 2 * ATT_GROUP, (pair + 1) * 2 * ATT_GROUP)
            feat = slice(pair * LANES, (pair + 1) * LANES)
            g8 = q_ref[i, groups, :]
            g8r = pltpu.roll(g8, half, axis=1)
            qbd = jnp.concatenate(
                [jnp.where(lo4, g8[:quarter], 0.0), jnp.where(lo4, g8r[:quarter], 0.0),
                 jnp.where(lo4, 0.0, g8r[quarter:]), jnp.where(lo4, 0.0, g8[quarter:])],
                axis=0)
            l = jnp.dot(qbd, kb[feat, :], preferred_element_type=F32) + bias_ref[hs, :]
            s = sink_ref[hs, 0:1]
            m = jnp.maximum(jnp.max(l, axis=-1, keepdims=True), s)
            pe = jnp.exp2(l - m)
            den = jnp.sum(pe, axis=-1, keepdims=True) + jnp.exp2(s - m)
            o2 = lax.dot_general(pe, vb[feat, :], contract_last,
                                 preferred_element_type=F32) / den
            o2r = pltpu.roll(o2, half, axis=1)
            og = jnp.concatenate(
                [jnp.where(lo4, o2[:quarter], o2r[quarter:2 * quarter]),
                 jnp.where(lo4, o2r[2 * quarter:3 * quarter], o2[3 * quarter:])], axis=0)
            o_ref[i, groups, :] = og * _silu(gate_ref[i, groups, :])


def _swa_sample(q, kv, ckt, cvt, gate, bias_s, sinks_rep):
    n = q.shape[0]
    bb = SAMPLE_ATT_BATCH
    n_groups = ATT_WIDTH // LANES
    blk3 = lambda i: (i, 0, 0)
    whole = lambda i: (0, 0)
    return pl.pallas_call(
        _swa_sample_kernel,
        grid=(n // bb,),
        in_specs=[pl.BlockSpec((bb, n_groups, LANES), blk3),
                  pl.BlockSpec((n, 2 * ATT_KV_WIDTH), whole),
                  pl.BlockSpec((bb, ATT_KV_WIDTH, WINDOW), blk3),
                  pl.BlockSpec((bb, ATT_KV_WIDTH, WINDOW), blk3),
                  pl.BlockSpec((bb, n_groups, LANES), blk3),
                  pl.BlockSpec((ATT_HEADS, WINDOW), whole),
                  pl.BlockSpec((ATT_HEADS, LANES), whole)],
        out_specs=[pl.BlockSpec((bb, n_groups, LANES), blk3),
                   pl.BlockSpec((bb, ATT_KV_WIDTH, WINDOW), blk3),
                   pl.BlockSpec((bb, ATT_KV_WIDTH, WINDOW), blk3)],
        out_shape=[jax.ShapeDtypeStruct((n, n_groups, LANES), F32),
                   jax.ShapeDtypeStruct((n, ATT_KV_WIDTH, WINDOW), F32),
                   jax.ShapeDtypeStruct((n, ATT_KV_WIDTH, WINDOW), F32)],
        scratch_shapes=[pltpu.VMEM((2 * ATT_KV_WIDTH, n), F32)],
        compiler_params=_params(1),
        name="swa_sample",
    )(q, kv, ckt, cvt, gate, bias_s, sinks_rep)


def _group_norm_gate(o, gate):
    mu = jnp.mean(o, axis=-1, keepdims=True)
    d = o - mu
    var = jnp.mean(d * d, axis=-1, keepdims=True)
    return d * lax.rsqrt(var + EPS) * _silu(gate)


def _ret_prompt_kernel(g_block_ref, q_ref, kt_ref, v_ref, gate_ref, x_ref, wout_ref, gf_ref,
                       y_ref, r_ref, og_ref):
    c = pl.program_id(1)

    @pl.when(c == 0)
    def _():
        r_ref[...] = jnp.zeros_like(r_ref)

    causal = (lax.broadcasted_iota(jnp.int32, (RET_BLOCK, RET_BLOCK), 1)
              <= lax.broadcasted_iota(jnp.int32, (RET_BLOCK, RET_BLOCK), 0))
    for h in range(RET_HEADS):
        qcols = slice(h * RET_QK_DIM, (h + 1) * RET_QK_DIM)
        vcols = slice(h * RET_V_DIM, (h + 1) * RET_V_DIM)
        for blk in range(RET_ROWS // RET_BLOCK):
            rows = slice(blk * RET_BLOCK, (blk + 1) * RET_BLOCK)
            qh = q_ref[rows, qcols]
            kt = kt_ref[qcols, rows]
            vh = v_ref[rows, vcols]
            state = r_ref[0, h]
            scores = jnp.where(causal, jnp.dot(qh, kt, preferred_element_type=F32), 0.0)
            o = (jnp.dot(scores.astype(BF16), vh, preferred_element_type=F32)
                 + jnp.dot(qh, state.astype(BF16), preferred_element_type=F32))
            r_ref[0, h] = g_block_ref[h] * (state + jnp.dot(kt, vh, preferred_element_type=F32))
            og_ref[rows, :] = _group_norm_gate(
                o, gate_ref[rows, vcols].astype(F32)).astype(og_ref.dtype)
        part = jnp.dot(og_ref[...], wout_ref[vcols, :], preferred_element_type=F32)
        if h == 0:
            y_ref[...] = x_ref[...] + part
        elif h < RET_HEADS - 1:
            y_ref[...] += part
        else:
            y_ref[...] = _rmsnorm(y_ref[...] + part, gf_ref[...])


def _ret_prompt(q, kt, v, gate, x, w_out, final_g, batch, seq):
    m = batch * seq
    nt = seq // RET_ROWS
    g_block = jnp.asarray(_decay_scales_np(RET_BLOCK, RET_BLOCK)[2])
    rows = lambda b, c: (b * nt + c, 0)
    return pl.pallas_call(
        _ret_prompt_kernel,
        grid=(batch, nt),
        in_specs=[pl.BlockSpec(memory_space=pltpu.SMEM),
                  pl.BlockSpec((RET_ROWS, RET_QK_WIDTH), rows),
                  pl.BlockSpec((RET_QK_WIDTH, RET_ROWS), lambda b, c: (0, b * nt + c)),
                  pl.BlockSpec((RET_ROWS, RET_WIDTH), rows),
                  pl.BlockSpec((RET_ROWS, RET_WIDTH), rows),
                  pl.BlockSpec((RET_ROWS, D_MODEL), rows),
                  pl.BlockSpec((RET_WIDTH, D_MODEL), lambda b, c: (0, 0)),
                  pl.BlockSpec((1, D_MODEL), lambda b, c: (0, 0))],
        out_specs=[pl.BlockSpec((RET_ROWS, D_MODEL), rows),
                   pl.BlockSpec((1, RET_HEADS, RET_QK_DIM, RET_V_DIM), lambda b, c: (b, 0, 0, 0))],
        out_shape=[jax.ShapeDtypeStruct((m, D_MODEL), F32),
                   jax.ShapeDtypeStruct((batch, RET_HEADS, RET_QK_DIM, RET_V_DIM), F32)],
        scratch_shapes=[pltpu.VMEM((RET_ROWS, RET_V_DIM), BF16)],
        compiler_params=_params(2),
        name="ret_prompt",
    )(g_block, q, kt, v, gate, x, w_out, final_g)


def kernel(x_prompt, x_sample, cache_swa_k, cache_swa_v, state_ret, norm_g, final_norm_g, rel_bias,
           w_in_attn, attn_sinks, w_out_attn, w_in_ret, w_out_ret):
    batch, seq, _ = x_prompt.shape
    n_s = x_sample.shape[0]
    xp = x_prompt.reshape(batch * seq, D_MODEL)
    xs = x_sample.reshape(n_s, D_MODEL)
    g0 = norm_g[0].reshape(1, D_MODEL)
    g1 = norm_g[1].reshape(1, D_MODEL)
    gf = final_norm_g.reshape(1, D_MODEL)
    w_in_a = w_in_attn[0].astype(BF16)
    w_out_a = w_out_attn[0].astype(BF16)
    w_in_r = w_in_ret[0].astype(BF16)
    w_out_r = w_out_ret[0].astype(BF16)
    sinks = attn_sinks[0] * LOG2E

    bias = _bias_table(rel_bias)

    q, kv, gate, kvpad = _attn_inproj(xp, g0, w_in_a, BF16, with_padded_kv=True)
    og = _swa_prompt(sinks, q, kvpad, gate, bias, batch, seq)
    xp1 = _outproj(og, w_out_a, xp)
    kv_last = kv.reshape(batch, seq, 2 * ATT_KV_WIDTH)[:, seq - WINDOW:]
    k_prompt = kv_last[..., :ATT_KV_WIDTH].reshape(1, batch, WINDOW, ATT_KV_HEADS, ATT_HEAD_DIM)
    v_prompt = kv_last[..., ATT_KV_WIDTH:].reshape(1, batch, WINDOW, ATT_KV_HEADS, ATT_HEAD_DIM)

    qs, kvs, gates = _attn_inproj(xs, g0, w_in_a, F32, with_padded_kv=False)
    order = _sample_head_order()
    bias_s = bias[1, :, WINDOW - 1, :][order]
    sinks_rep = jnp.broadcast_to(sinks[order][:, None], (ATT_HEADS, LANES))
    to_feature_major = lambda c: jnp.transpose(c, (0, 2, 3, 1)).reshape(n_s, ATT_KV_WIDTH, WINDOW)
    to_window_major = lambda c: jnp.transpose(
        c.reshape(n_s, ATT_KV_HEADS, ATT_HEAD_DIM, WINDOW), (0, 3, 1, 2))[None]
    n_groups = ATT_WIDTH // LANES
    ogs, ckt_new, cvt_new = _swa_sample(
        qs.reshape(n_s, n_groups, LANES), kvs,
        to_feature_major(cache_swa_k[0]), to_feature_major(cache_swa_v[0]),
        gates.reshape(n_s, n_groups, LANES), bias_s, sinks_rep)
    xs1 = _outproj(ogs.reshape(n_s, ATT_WIDTH), w_out_a, xs)
    k_sample = to_window_major(ckt_new)
    v_sample = to_window_major(cvt_new)

    w_kt = _transposed_columns(w_in_ret[0], RET_QK_WIDTH, RET_QK_WIDTH, BF16)
    q_s, kt_s, v_s, gate_s = _ret_inproj(xs1, g1, w_in_r, w_kt, np.full((n_s,), PAST_LEN), 1, F32)
    q, kt, v, gate, og_s, r_sample = _ret_inproj(
        xp1, g1, w_in_r, w_kt, np.arange(seq), RET_BLOCK, BF16,
        sample=(q_s, kt_s, v_s, gate_s, state_ret[0]))
    y_prompt, r_prompt = _ret_prompt(q, kt, v, gate, xp1, w_out_r, gf, batch, seq)
    y_prompt = y_prompt.reshape(batch, seq, D_MODEL)
    y_sample = _outproj(og_s, w_out_r, xs1, gf).reshape(n_s, 1, D_MODEL)

    return (y_prompt, y_sample, k_prompt, v_prompt, r_prompt[None],
            k_sample, v_sample, r_sample[None])
```

```python
import functools
import math

import numpy as np
import jax
import jax.numpy as jnp
from jax import lax
from jax.experimental import pallas as pl
from jax.experimental.pallas import tpu as pltpu

F32 = jnp.float32
BF16 = jnp.bfloat16

D_MODEL = 1024
PAST_LEN = 16384
ATT_HEADS = 32
ATT_KV_HEADS = 4
ATT_GROUP = ATT_HEADS // ATT_KV_HEADS
ATT_HEAD_DIM = 64
ATT_WIDTH = ATT_HEADS * ATT_HEAD_DIM
ATT_KV_WIDTH = ATT_KV_HEADS * ATT_HEAD_DIM
ATT_IN = 2 * ATT_WIDTH + 2 * ATT_KV_WIDTH
WINDOW = 128
NUM_BUCKETS = 32
MAX_DISTANCE = 128
RET_HEADS = 4
RET_QK_DIM = 256
RET_V_DIM = 512
RET_QK_WIDTH = RET_HEADS * RET_QK_DIM
RET_WIDTH = RET_HEADS * RET_V_DIM
RET_IN = 2 * RET_QK_WIDTH + 2 * RET_WIDTH
ROPE_BASE = 10000.0
EPS = 1e-6
NEG = -1e30
LOG2E = math.log2(math.e)

LANES = 128
PADDED_KV_VARIANTS = 4
PADDED_KV_WIDTH = ATT_KV_HEADS * PADDED_KV_VARIANTS * LANES
PROJ_ROWS = 512
PROJ_COLS = 512
OUT_ROWS = 1024
ATT_ROWS = 512
RET_BLOCK = 256
RET_ROWS = 512
SAMPLE_ATT_BATCH = 2
SAMPLE_RET_BATCH = 2
VMEM_LIMIT = 60 * 1024 * 1024


def _params(n_axes):
    return pltpu.CompilerParams(
        dimension_semantics=("arbitrary",) * n_axes,
        vmem_limit_bytes=VMEM_LIMIT)


def _resident(shape, index_map):
    return pl.BlockSpec(shape, index_map, pipeline_mode=pl.Buffered(1))


def _silu(x):
    hx = 0.5 * x
    return hx + hx * jnp.tanh(hx)


def _rmsnorm(x, g):
    return x * lax.rsqrt(jnp.mean(x * x, axis=-1, keepdims=True) + EPS) * g


def _t5_bucket_np(rel):
    n = np.maximum(rel, 0)
    max_exact = NUM_BUCKETS // 2
    nf = np.maximum(n, 1).astype(np.float64)
    large = max_exact + (np.log(nf / max_exact) / math.log(MAX_DISTANCE / max_exact)
                         * (NUM_BUCKETS - max_exact)).astype(np.int32)
    large = np.minimum(large, NUM_BUCKETS - 1)
    return np.where(n < max_exact, n, large).astype(np.int32)


def _bucket_map_np():
    i = np.arange(WINDOW)[:, None]
    j = np.arange(WINDOW)[None, :]
    rel = np.where(j <= i, i - j, i - j + WINDOW)
    return _t5_bucket_np(rel).astype(np.int32)


def _gammas():
    return [1.0 - 2.0 ** (-5 - h) for h in range(RET_HEADS)]


def _decay_scales_np(rows, block):
    lg = np.log1p(-np.exp2(-5.0 - np.arange(RET_HEADS, dtype=np.float64)))
    i1 = (np.arange(rows) % block + 1.0)[None, :] * lg[:, None]
    q_scale = np.repeat(np.exp(i1)[:, :, None], LANES, axis=2)
    k_scale = np.repeat((np.exp(-i1) * RET_QK_DIM ** -0.5)[:, None, :], LANES, axis=1)
    return (q_scale.astype(np.float32), k_scale.astype(np.float32),
            np.exp(block * lg).astype(np.float32))


def _rope_tables_np(pos):
    half = RET_QK_DIM // 2
    inv = ROPE_BASE ** (-np.arange(half, dtype=np.float64) / half)
    ang = np.asarray(pos, np.float64)[:, None] * inv[None, :]
    return np.cos(ang).astype(np.float32), np.sin(ang).astype(np.float32)


def _bias_table_kernel(rb_ref, bucket_ref, out_ref):
    bk = bucket_ref[...]
    previous_block = (lax.broadcasted_iota(jnp.int32, bk.shape, 1)
                      > lax.broadcasted_iota(jnp.int32, bk.shape, 0))

    def one_head(h, carry):
        acc = jnp.zeros(bk.shape, F32)
        for b in range(NUM_BUCKETS):
            acc = jnp.where(bk == b, rb_ref[b, h] * LOG2E, acc)
        out_ref[1, h] = acc
        out_ref[0, h] = jnp.where(previous_block, NEG, acc)
        return carry

    lax.fori_loop(0, ATT_HEADS, one_head, 0)


def _bias_table(rel_bias):
    bucket = jnp.asarray(_bucket_map_np())
    return pl.pallas_call(
        _bias_table_kernel,
        grid=(1,),
        in_specs=[pl.BlockSpec(memory_space=pltpu.SMEM),
                  pl.BlockSpec((WINDOW, WINDOW), lambda i: (0, 0))],
        out_specs=pl.BlockSpec((2, ATT_HEADS, WINDOW, WINDOW), lambda i: (0, 0, 0, 0)),
        out_shape=jax.ShapeDtypeStruct((2, ATT_HEADS, WINDOW, WINDOW), F32),
        compiler_params=_params(1),
        name="bias_table",
    )(rel_bias, bucket)


def _transpose_cast_kernel(w_ref, o_ref):
    o_ref[...] = w_ref[...].T.astype(o_ref.dtype)


def _transposed_columns(w, col0, ncols, dtype):
    rows = w.shape[0]
    blk = 2 * LANES
    return pl.pallas_call(
        _transpose_cast_kernel,
        grid=(ncols // blk,),
        in_specs=[pl.BlockSpec((rows, blk), lambda i: (0, col0 // blk + i))],
        out_specs=pl.BlockSpec((blk, rows), lambda i: (i, 0)),
        out_shape=jax.ShapeDtypeStruct((ncols, rows), dtype),
        compiler_params=_params(1),
        name="transpose_cast",
    )(w)


def _pad_pair(x, x_rolled, lo, e):
    if e == 0:
        return jnp.where(lo, x, 0.0), jnp.where(lo, 0.0, x_rolled)
    return jnp.where(lo, x_rolled, 0.0), jnp.where(lo, 0.0, x)


def _attn_inproj_kernel(*refs, hosts_sample):
    x_ref, g_ref, w_ref = refs[:3]
    side_work = []
    kvpad_ref = None
    if hosts_sample:
        qs_ref, kvs_ref, ckt_ref, cvt_ref, gates_ref, bias_ref, sink_ref = refs[3:10]
        q_ref, kv_ref, gate_ref, kvpad_ref, ogs_ref, cko_ref, cvo_ref, kvt_ref = refs[10:]
        step = pl.program_id(0)

        @pl.when(step == 0)
        def _():
            kvt_ref[...] = kvs_ref[...].T

        side_work = _swa_sample_pieces(step, qs_ref, kvt_ref, ckt_ref, cvt_ref, gates_ref,
                                       bias_ref, sink_ref, ogs_ref, cko_ref, cvo_ref)
    else:
        q_ref, kv_ref, gate_ref = refs[3:]

    h = _rmsnorm(x_ref[...], g_ref[...]).astype(BF16)
    for c in range(0, ATT_IN, PROJ_COLS):
        if side_work:
            side_work.pop(0)()
        r = jnp.dot(h, w_ref[:, c:c + PROJ_COLS], preferred_element_type=F32)
        if c < ATT_WIDTH:
            q_ref[:, c:c + PROJ_COLS] = (r * (ATT_HEAD_DIM ** -0.5 * LOG2E)).astype(q_ref.dtype)
        elif c < ATT_WIDTH + 2 * ATT_KV_WIDTH:
            kv_ref[...] = r
            if kvpad_ref is not None:
                lo = lax.broadcasted_iota(jnp.int32, (r.shape[0], LANES), 1) < ATT_HEAD_DIM
                for pair in range(ATT_KV_HEADS // 2):
                    for is_v in range(2):
                        c1 = is_v * ATT_KV_WIDTH + pair * LANES
                        x2 = r[:, c1:c1 + LANES]
                        x2r = pltpu.roll(x2, ATT_HEAD_DIM, axis=1)
                        for e in range(2):
                            g0 = ((2 * pair + e) * PADDED_KV_VARIANTS + 2 * is_v) * LANES
                            lo_half, hi_half = _pad_pair(x2, x2r, lo, e)
                            kvpad_ref[:, g0:g0 + LANES] = lo_half.astype(kvpad_ref.dtype)
                            kvpad_ref[:, g0 + LANES:g0 + 2 * LANES] = hi_half.astype(kvpad_ref.dtype)
        else:
            o = c - ATT_WIDTH - 2 * ATT_KV_WIDTH
            gate_ref[:, o:o + PROJ_COLS] = r.astype(gate_ref.dtype)
    while side_work:
        side_work.pop(0)()


def _attn_inproj(x, g, w, out_dtype, sample=None):
    m = x.shape[0]
    tm = min(PROJ_ROWS, m)
    row = lambda i: (i, 0)
    const2 = lambda i: (0, 0)
    const3 = lambda i: (0, 0, 0)
    in_specs = [pl.BlockSpec((tm, D_MODEL), row),
                pl.BlockSpec((1, D_MODEL), const2),
                _resident((D_MODEL, ATT_IN), const2)]
    out_specs = [pl.BlockSpec((tm, ATT_WIDTH), row),
                 pl.BlockSpec((tm, 2 * ATT_KV_WIDTH), row),
                 pl.BlockSpec((tm, ATT_WIDTH), row)]
    out_shape = [jax.ShapeDtypeStruct((m, ATT_WIDTH), out_dtype),
                 jax.ShapeDtypeStruct((m, 2 * ATT_KV_WIDTH), F32),
                 jax.ShapeDtypeStruct((m, ATT_WIDTH), out_dtype)]
    args = [x, g, w]
    scratch_shapes = []
    if sample is not None:
        q_s, kv_s, ckt, cvt, gate_s, bias_s, sinks_rep = sample
        n, n_groups, _ = q_s.shape
        assert n == SAMPLE_ATT_BATCH * (m // tm), (n, m, tm)
        cache_spec = pl.BlockSpec((SAMPLE_ATT_BATCH, ATT_KV_WIDTH, WINDOW), lambda i: (i, 0, 0))
        in_specs += [_resident((n, n_groups, LANES), const3),
                     _resident((n, 2 * ATT_KV_WIDTH), const2),
                     cache_spec, cache_spec,
                     _resident((n, n_groups, LANES), const3),
                     _resident((ATT_HEADS, WINDOW), const2),
                     _resident((ATT_HEADS, LANES), const2)]
        out_specs += [pl.BlockSpec((tm, PADDED_KV_WIDTH), row),
                      pl.BlockSpec((n, n_groups, LANES), const3), cache_spec, cache_spec]
        out_shape += [jax.ShapeDtypeStruct((m, PADDED_KV_WIDTH), BF16),
                      jax.ShapeDtypeStruct((n, n_groups, LANES), F32),
                      jax.ShapeDtypeStruct((n, ATT_KV_WIDTH, WINDOW), F32),
                      jax.ShapeDtypeStruct((n, ATT_KV_WIDTH, WINDOW), F32)]
        args += list(sample)
        scratch_shapes = [pltpu.VMEM((2 * ATT_KV_WIDTH, n), F32)]
    return pl.pallas_call(
        functools.partial(_attn_inproj_kernel, hosts_sample=sample is not None),
        grid=(m // tm,),
        in_specs=in_specs,
        out_specs=out_specs,
        out_shape=out_shape,
        scratch_shapes=scratch_shapes,
        compiler_params=_params(1),
        name="attn_inproj",
    )(*args)


def _ret_sample_pieces(step, qt_ref, kt_ref, v_ref, gate_ref, st_ref, og_ref, sto_ref):
    lane = lax.broadcasted_iota(jnp.int32, (RET_QK_DIM, LANES), 1)
    gammas = _gammas()

    def piece(i, h):
        b = step * SAMPLE_RET_BATCH + i
        pick = lane == b
        rows = slice(h * RET_QK_DIM, (h + 1) * RET_QK_DIM)
        qc = jnp.sum(jnp.where(pick, qt_ref[rows, :], 0.0), axis=1, keepdims=True)
        kc = jnp.sum(jnp.where(pick, kt_ref[rows, :], 0.0), axis=1, keepdims=True)
        vcols = slice(h * RET_V_DIM, (h + 1) * RET_V_DIM)
        vrow = v_ref[pl.ds(b, 1), vcols]
        state = st_ref[i, h]
        o = (jnp.sum(qc * state, axis=0, keepdims=True)
             + jnp.sum(qc * kc, axis=0, keepdims=True) * vrow)
        sto_ref[i, h] = gammas[h] * (state + kc * vrow)
        og_ref[pl.ds(b, 1), vcols] = _group_norm_gate(o, gate_ref[pl.ds(b, 1), vcols])

    return [functools.partial(piece, i, h)
            for i in range(SAMPLE_RET_BATCH) for h in range(RET_HEADS)]


def _ret_inproj_kernel(*refs, with_sample):
    (x_ref, g_ref, w_ref, wkt_ref, cos_ref, sin_ref, cost_ref, sint_ref, qsc_ref, ksc_ref) = refs[:10]
    side_work = []
    if with_sample:
        qs_ref, kts_ref, vs_ref, gates_ref, st_ref = refs[10:15]
        q_ref, kt_ref, v_ref, gate_ref, ogs_ref, sto_ref, qts_ref = refs[15:]
        step = pl.program_id(0)

        @pl.when(step == 0)
        def _():
            qts_ref[...] = qs_ref[...].T

        side_work = _ret_sample_pieces(step, qts_ref, kts_ref, vs_ref, gates_ref, st_ref,
                                       ogs_ref, sto_ref)
    else:
        q_ref, kt_ref, v_ref, gate_ref = refs[10:]

    def interleave():
        if side_work:
            side_work.pop(0)()

    h = _rmsnorm(x_ref[...], g_ref[...]).astype(BF16)
    half = RET_QK_DIM // 2
    contract_last = (((1,), (1,)), ((), ()))
    cos = cos_ref[...]
    sin = sin_ref[...]
    for c in range(0, RET_QK_WIDTH, PROJ_COLS):
        interleave()
        r = jnp.dot(h, w_ref[:, c:c + PROJ_COLS], preferred_element_type=F32)
        for o in range(0, PROJ_COLS, RET_QK_DIM):
            scale = qsc_ref[(c + o) // RET_QK_DIM]
            x1 = r[:, o:o + half]
            x2 = r[:, o + half:o + RET_QK_DIM]
            q_ref[:, c + o:c + o + half] = ((x1 * cos - x2 * sin) * scale).astype(q_ref.dtype)
            q_ref[:, c + o + half:c + o + RET_QK_DIM] = (
                (x1 * sin + x2 * cos) * scale).astype(q_ref.dtype)
    cos = cost_ref[...]
    sin = sint_ref[...]
    for c in range(0, RET_QK_WIDTH, PROJ_COLS):
        interleave()
        r = lax.dot_general(wkt_ref[c:c + PROJ_COLS, :], h, contract_last,
                            preferred_element_type=F32)
        for o in range(0, PROJ_COLS, RET_QK_DIM):
            scale = ksc_ref[(c + o) // RET_QK_DIM]
            x1 = r[o:o + half, :]
            x2 = r[o + half:o + RET_QK_DIM, :]
            kt_ref[c + o:c + o + half, :] = ((x1 * cos - x2 * sin) * scale).astype(kt_ref.dtype)
            kt_ref[c + o + half:c + o + RET_QK_DIM, :] = (
                (x1 * sin + x2 * cos) * scale).astype(kt_ref.dtype)
    for c in range(0, RET_WIDTH, PROJ_COLS):
        interleave()
        v_ref[:, c:c + PROJ_COLS] = jnp.dot(
            h, w_ref[:, 2 * RET_QK_WIDTH + c:2 * RET_QK_WIDTH + c + PROJ_COLS],
            preferred_element_type=F32).astype(v_ref.dtype)
    for c in range(0, RET_WIDTH, PROJ_COLS):
        interleave()
        w0 = 2 * RET_QK_WIDTH + RET_WIDTH + c
        gate_ref[:, c:c + PROJ_COLS] = jnp.dot(
            h, w_ref[:, w0:w0 + PROJ_COLS], preferred_element_type=F32).astype(gate_ref.dtype)
    while side_work:
        interleave()


def _ret_inproj(x, g, w, wkt, pos, block, out_dtype, sample=None):
    m = x.shape[0]
    tm = min(PROJ_ROWS, m)
    n_pos = len(pos) // tm
    half = RET_QK_DIM // 2
    cos, sin = _rope_tables_np(pos)
    q_scale, k_scale, _ = _decay_scales_np(tm, block)
    row = lambda i: (i, 0)
    const2 = lambda i: (0, 0)
    const3 = lambda i: (0, 0, 0)
    in_specs = [pl.BlockSpec((tm, D_MODEL), row),
                pl.BlockSpec((1, D_MODEL), const2),
                _resident((D_MODEL, RET_IN), const2),
                _resident((RET_QK_WIDTH, D_MODEL), const2),
                pl.BlockSpec((tm, half), lambda i: (i % n_pos, 0)),
                pl.BlockSpec((tm, half), lambda i: (i % n_pos, 0)),
                pl.BlockSpec((half, tm), lambda i: (0, i % n_pos)),
                pl.BlockSpec((half, tm), lambda i: (0, i % n_pos)),
                _resident((RET_HEADS, tm, LANES), const3),
                _resident((RET_HEADS, LANES, tm), const3)]
    out_specs = [pl.BlockSpec((tm, RET_QK_WIDTH), row),
                 pl.BlockSpec((RET_QK_WIDTH, tm), lambda i: (0, i)),
                 pl.BlockSpec((tm, RET_WIDTH), row),
                 pl.BlockSpec((tm, RET_WIDTH), row)]
    out_shape = [jax.ShapeDtypeStruct((m, RET_QK_WIDTH), out_dtype),
                 jax.ShapeDtypeStruct((RET_QK_WIDTH, m), out_dtype),
                 jax.ShapeDtypeStruct((m, RET_WIDTH), out_dtype),
                 jax.ShapeDtypeStruct((m, RET_WIDTH), out_dtype)]
    args = [x, g, w, wkt, jnp.asarray(cos), jnp.asarray(sin), jnp.asarray(cos.T),
            jnp.asarray(sin.T), jnp.asarray(q_scale), jnp.asarray(k_scale)]
    scratch_shapes = []
    if sample is not None:
        q_s, kt_s, v_s, gate_s, state = sample
        n = q_s.shape[0]
        assert n == SAMPLE_RET_BATCH * (m // tm), (n, m, tm)
        st_spec = pl.BlockSpec((SAMPLE_RET_BATCH, RET_HEADS, RET_QK_DIM, RET_V_DIM),
                               lambda i: (i, 0, 0, 0))
        in_specs += [_resident((n, RET_QK_WIDTH), const2), _resident((RET_QK_WIDTH, n), const2),
                     _resident((n, RET_WIDTH), const2), _resident((n, RET_WIDTH), const2), st_spec]
        out_specs += [pl.BlockSpec((n, RET_WIDTH), const2), st_spec]
        out_shape += [jax.ShapeDtypeStruct((n, RET_WIDTH), F32),
                      jax.ShapeDtypeStruct(state.shape, F32)]
        args += [q_s, kt_s, v_s, gate_s, state]
        scratch_shapes = [pltpu.VMEM((RET_QK_WIDTH, n), F32)]
    return pl.pallas_call(
        functools.partial(_ret_inproj_kernel, with_sample=sample is not None),
        grid=(m // tm,),
        in_specs=in_specs,
        out_specs=out_specs,
        out_shape=out_shape,
        scratch_shapes=scratch_shapes,
        compiler_params=_params(1),
        name="ret_inproj",
    )(*args)


def _outproj_kernel(o_ref, w_ref, x_ref, y_ref):
    y_ref[...] = x_ref[...] + jnp.dot(o_ref[...].astype(BF16), w_ref[...],
                                      preferred_element_type=F32)


def _outproj_norm_kernel(o_ref, w_ref, x_ref, g_ref, y_ref):
    y = x_ref[...] + jnp.dot(o_ref[...].astype(BF16), w_ref[...], preferred_element_type=F32)
    y_ref[...] = _rmsnorm(y, g_ref[...])


def _outproj(o, w, x, final_g=None):
    m, width = o.shape
    tm = min(OUT_ROWS, m)
    row = lambda i: (i, 0)
    in_specs = [pl.BlockSpec((tm, width), row),
                pl.BlockSpec((width, D_MODEL), lambda i: (0, 0)),
                pl.BlockSpec((tm, D_MODEL), row)]
    args = [o, w, x]
    body = _outproj_kernel
    if final_g is not None:
        in_specs.append(pl.BlockSpec((1, D_MODEL), lambda i: (0, 0)))
        args.append(final_g)
        body = _outproj_norm_kernel
    return pl.pallas_call(
        body,
        grid=(m // tm,),
        in_specs=in_specs,
        out_specs=pl.BlockSpec((tm, D_MODEL), row),
        out_shape=jax.ShapeDtypeStruct((m, D_MODEL), F32),
        compiler_params=_params(1),
        name="outproj",
    )(*args)


def _swa_prompt_kernel(sink_ref, q_ref, kvc_ref, kvp_ref, gate_ref, bias_ref, out_ref):
    j = pl.program_id(1)
    half = ATT_HEAD_DIM
    pairs_per_kv = ATT_GROUP // 2
    kv_group_cols = ATT_GROUP * ATT_HEAD_DIM
    contract_last = (((1,), (1,)), ((), ()))
    lo1 = lax.broadcasted_iota(jnp.int32, (WINDOW, LANES), 1) < half
    own = (lax.broadcasted_iota(jnp.int32, (WINDOW, WINDOW), 1)
           <= lax.broadcasted_iota(jnp.int32, (WINDOW, WINDOW), 0))
    units =[(blk, kvh) for blk in range(ATT_ROWS // WINDOW) for kvh in range(ATT_KV_HEADS)]

    def keys_values(blk, kvh):
        r0 = blk * WINDOW
        out = []
        for variant in range(PADDED_KV_VARIANTS):
            c = (kvh * PADDED_KV_VARIANTS + variant) * LANES
            prev = kvp_ref[:, c:c + LANES] if blk == 0 else kvc_ref[r0 - WINDOW:r0, c:c + LANES]
            out.append(jnp.concatenate([prev, kvc_ref[r0:r0 + WINDOW, c:c + LANES]], axis=0))
        return out

    def scores(blk, kvh):
        rows = slice(blk * WINDOW, (blk + 1) * WINDOW)
        c0 = kvh * kv_group_cols
        k_a, k_b, v_a, v_b = keys_values(blk, kvh)
        qs = jnp.concatenate(
            [q_ref[rows, c0 + p * LANES:c0 + (p + 1) * LANES] for p in range(pairs_per_kv)],
            axis=0)
        logits = (lax.dot_general(qs, k_a, contract_last, preferred_element_type=F32),
                  lax.dot_general(qs, k_b, contract_last, preferred_element_type=F32))
        return logits, v_a, v_b

    def softmax(blk, kvh, logits):
        sel = jnp.where(j == 0, 0, 1) if blk == 0 else 1
        probs = ([], [])
        inv = ([], [])
        for p in range(pairs_per_kv):
            for st in range(2):
                hd = kvh * ATT_GROUP + 2 * p + st
                lg = logits[st][p * WINDOW:(p + 1) * WINDOW, :]
                l = jnp.where(own, lg[:, WINDOW:], lg[:, :WINDOW]) + bias_ref[sel, hd]
                s = sink_ref[hd]
                m = jnp.maximum(jnp.max(l, axis=-1, keepdims=True), s)
                pe = jnp.exp2(l - m)
                den = jnp.sum(pe, axis=-1, keepdims=True) + jnp.exp2(s - m)
                probs[st].append(jnp.concatenate(
                    [jnp.where(own, 0.0, pe).astype(BF16),
                     jnp.where(own, pe, 0.0).astype(BF16)], axis=1))
                inv[st].append(1.0 / den)
        return jnp.concatenate(probs[0], axis=0), jnp.concatenate(probs[1], axis=0), inv

    def output(blk, kvh, p_a, p_b, inv, v_a, v_b):
        rows = slice(blk * WINDOW, (blk + 1) * WINDOW)
        c0 = kvh * kv_group_cols
        o = (jnp.dot(p_a, v_a, preferred_element_type=F32)
             + jnp.dot(p_b, v_b, preferred_element_type=F32))
        for p in range(pairs_per_kv):
            cols = slice(c0 + p * LANES, c0 + (p + 1) * LANES)
            scale = jnp.where(lo1, inv[0][p], inv[1][p])
            gt = gate_ref[rows, cols].astype(F32)
            og = o[p * WINDOW:(p + 1) * WINDOW, :] * scale * _silu(gt)
            out_ref[rows, cols] = og.astype(out_ref.dtype)

    for unit in units:
        logits, v_a, v_b = scores(*unit)
        p_a, p_b, inv = softmax(*unit, logits)
        output(*unit, p_a, p_b, inv, v_a, v_b)


def _swa_prompt(sinks, q, kvpad, gate, bias, batch, seq):
    m = batch * seq
    nt = seq // ATT_ROWS
    blocks_per_tile = ATT_ROWS // WINDOW
    blocks_per_seq = seq // WINDOW
    tile = lambda b, j: (b * nt + j, 0)
    prev = lambda b, j: (b * blocks_per_seq + jnp.maximum(j * blocks_per_tile - 1, 0), 0)
    return pl.pallas_call(
        _swa_prompt_kernel,
        grid=(batch, nt),
        in_specs=[pl.BlockSpec(memory_space=pltpu.SMEM),
                  pl.BlockSpec((ATT_ROWS, ATT_WIDTH), tile),
                  pl.BlockSpec((ATT_ROWS, PADDED_KV_WIDTH), tile),
                  pl.BlockSpec((WINDOW, PADDED_KV_WIDTH), prev),
                  pl.BlockSpec((ATT_ROWS, ATT_WIDTH), tile),
                  pl.BlockSpec((2, ATT_HEADS, WINDOW, WINDOW), lambda b, j: (0, 0, 0, 0))],
        out_specs=pl.BlockSpec((ATT_ROWS, ATT_WIDTH), tile),
        out_shape=jax.ShapeDtypeStruct((m, ATT_WIDTH), BF16),
        compiler_params=_params(2),
        name="swa_prompt",
    )(sinks, q, kvpad, kvpad, gate, bias)


def _sample_head_order():
    order = []
    for pair in range(ATT_KV_HEADS // 2):
        base = pair * 2 * ATT_GROUP
        for kv in range(2):
            for half in range(2):
                order += [base + kv * ATT_GROUP + 2 * g + half for g in range(ATT_GROUP // 2)]
    return np.asarray(order, np.int32)


def _swa_sample_pieces(step, q_ref, kvt_ref, ckt_ref, cvt_ref, gate_ref, bias_ref, sink_ref,
                       o_ref, cko_ref, cvo_ref):
    half = ATT_HEAD_DIM
    quarter = ATT_GROUP // 2
    contract_last = (((1,), (1,)), ((), ()))
    newest = lax.broadcasted_iota(jnp.int32, (ATT_KV_WIDTH, WINDOW), 1) == WINDOW - 1
    seq_lane = lax.broadcasted_iota(jnp.int32, (2 * ATT_KV_WIDTH, LANES), 1)
    lo4 = lax.broadcasted_iota(jnp.int32, (quarter, LANES), 1) < half
    new_cache = {}

    def update(i):
        b = step * SAMPLE_ATT_BATCH + i
        new_col = jnp.sum(jnp.where(seq_lane == b, kvt_ref[...], 0.0), axis=1, keepdims=True)
        kb = jnp.where(newest, new_col[:ATT_KV_WIDTH], pltpu.roll(ckt_ref[i], WINDOW - 1, axis=1))
        vb = jnp.where(newest, new_col[ATT_KV_WIDTH:], pltpu.roll(cvt_ref[i], WINDOW - 1, axis=1))
        cko_ref[i] = kb
        cvo_ref[i] = vb
        new_cache[i] = (kb, vb)

    def attend(i, pair):
        b = step * SAMPLE_ATT_BATCH + i
        kb, vb = new_cache[i]
        groups = slice(pair * ATT_GROUP, (pair + 1) * ATT_GROUP)
        hs = slice(pair * 2 * ATT_GROUP, (pair + 1) * 2 * ATT_GROUP)
        feat = slice(pair * LANES, (pair + 1) * LANES)
        g8 = q_ref[b, groups, :]
        g8r = pltpu.roll(g8, half, axis=1)
        qbd = jnp.concatenate(
            [jnp.where(lo4, g8[:quarter], 0.0), jnp.where(lo4, g8r[:quarter], 0.0),
             jnp.where(lo4, 0.0, g8r[quarter:]), jnp.where(lo4, 0.0, g8[quarter:])],
            axis=0)
        l = jnp.dot(qbd, kb[feat, :], preferred_element_type=F32) + bias_ref[hs, :]
        s = sink_ref[hs, 0:1]
        m = jnp.maximum(jnp.max(l, axis=-1, keepdims=True), s)
        pe = jnp.exp2(l - m)
        den = jnp.sum(pe, axis=-1, keepdims=True) + jnp.exp2(s - m)
        o2 = lax.dot_general(pe, vb[feat, :], contract_last,
                             preferred_element_type=F32) / den
        o2r = pltpu.roll(o2, half, axis=1)
        og = jnp.concatenate(
            [jnp.where(lo4, o2[:quarter], o2r[quarter:2 * quarter]),
             jnp.where(lo4, o2r[2 * quarter:3 * quarter], o2[3 * quarter:])], axis=0)
        o_ref[b, groups, :] = og * _silu(gate_ref[b, groups, :])

    pieces = []
    for i in range(SAMPLE_ATT_BATCH):
        pieces.append(functools.partial(update, i))
        pieces += [functools.partial(attend, i, pair) for pair in range(ATT_KV_HEADS // 2)]
    return pieces


def _group_norm_gate(o, gate):
    mu = jnp.mean(o, axis=-1, keepdims=True)
    d = o - mu
    var = jnp.mean(d * d, axis=-1, keepdims=True)
    return d * lax.rsqrt(var + EPS) * _silu(gate)


def _ret_prompt_kernel(g_block_ref, q_ref, kt_ref, v_ref, gate_ref, x_ref, wout_ref, gf_ref,
                       y_ref, r_ref, og_ref):
    c = pl.program_id(1)

    @pl.when(c == 0)
    def _():
        r_ref[...] = jnp.zeros_like(r_ref)

    causal = (lax.broadcasted_iota(jnp.int32, (RET_BLOCK, RET_BLOCK), 1)
              <= lax.broadcasted_iota(jnp.int32, (RET_BLOCK, RET_BLOCK), 0))
    for h in range(RET_HEADS):
        qcols = slice(h * RET_QK_DIM, (h + 1) * RET_QK_DIM)
        vcols = slice(h * RET_V_DIM, (h + 1) * RET_V_DIM)
        for blk in range(RET_ROWS // RET_BLOCK):
            rows = slice(blk * RET_BLOCK, (blk + 1) * RET_BLOCK)
            qh = q_ref[rows, qcols]
            kt = kt_ref[qcols, rows]
            vh = v_ref[rows, vcols]
            state = r_ref[0, h]
            scores = jnp.where(causal, jnp.dot(qh, kt, preferred_element_type=F32), 0.0)
            o = (jnp.dot(scores.astype(BF16), vh, preferred_element_type=F32)
                 + jnp.dot(qh, state.astype(BF16), preferred_element_type=F32))
            r_ref[0, h] = g_block_ref[h] * (state + jnp.dot(kt, vh, preferred_element_type=F32))
            og_ref[rows, :] = _group_norm_gate(
                o, gate_ref[rows, vcols].astype(F32)).astype(og_ref.dtype)
        part = jnp.dot(og_ref[...], wout_ref[vcols, :], preferred_element_type=F32)
        if h == 0:
            y_ref[...] = x_ref[...] + part
        elif h < RET_HEADS - 1:
            y_ref[...] += part
        else:
            y_ref[...] = _rmsnorm(y_ref[...] + part, gf_ref[...])


def _ret_prompt(q, kt, v, gate, x, w_out, final_g, batch, seq):
    m = batch * seq
    nt = seq // RET_ROWS
    g_block = jnp.asarray(_decay_scales_np(RET_BLOCK, RET_BLOCK)[2])
    rows = lambda b, c: (b * nt + c, 0)
    return pl.pallas_call(
        _ret_prompt_kernel,
        grid=(batch, nt),
        in_specs=[pl.BlockSpec(memory_space=pltpu.SMEM),
                  pl.BlockSpec((RET_ROWS, RET_QK_WIDTH), rows),
                  pl.BlockSpec((RET_QK_WIDTH, RET_ROWS), lambda b, c: (0, b * nt + c)),
                  pl.BlockSpec((RET_ROWS, RET_WIDTH), rows),
                  pl.BlockSpec((RET_ROWS, RET_WIDTH), rows),
                  pl.BlockSpec((RET_ROWS, D_MODEL), rows),
                  pl.BlockSpec((RET_WIDTH, D_MODEL), lambda b, c: (0, 0)),
                  pl.BlockSpec((1, D_MODEL), lambda b, c: (0, 0))],
        out_specs=[pl.BlockSpec((RET_ROWS, D_MODEL), rows),
                   pl.BlockSpec((1, RET_HEADS, RET_QK_DIM, RET_V_DIM), lambda b, c: (b, 0, 0, 0))],
        out_shape=[jax.ShapeDtypeStruct((m, D_MODEL), F32),
                   jax.ShapeDtypeStruct((batch, RET_HEADS, RET_QK_DIM, RET_V_DIM), F32)],
        scratch_shapes=[pltpu.VMEM((RET_ROWS, RET_V_DIM), BF16)],
        compiler_params=_params(2),
        name="ret_prompt",
    )(g_block, q, kt, v, gate, x, w_out, final_g)


def kernel(x_prompt, x_sample, cache_swa_k, cache_swa_v, state_ret, norm_g, final_norm_g, rel_bias,
           w_in_attn, attn_sinks, w_out_attn, w_in_ret, w_out_ret):
    batch, seq, _ = x_prompt.shape
    n_s = x_sample.shape[0]
    xp = x_prompt.reshape(batch * seq, D_MODEL)
    xs = x_sample.reshape(n_s, D_MODEL)
    g0 = norm_g[0].reshape(1, D_MODEL)
    g1 = norm_g[1].reshape(1, D_MODEL)
    gf = final_norm_g.reshape(1, D_MODEL)
    w_in_a = w_in_attn[0].astype(BF16)
    w_out_a = w_out_attn[0].astype(BF16)
    w_in_r = w_in_ret[0].astype(BF16)
    w_out_r = w_out_ret[0].astype(BF16)
    sinks = attn_sinks[0] * LOG2E

    bias = _bias_table(rel_bias)

    qs, kvs, gates = _attn_inproj(xs, g0, w_in_a, F32)
    order = _sample_head_order()
    bias_s = bias[1, :, WINDOW - 1, :][order]
    sinks_rep = jnp.broadcast_to(sinks[order][:, None], (ATT_HEADS, LANES))
    to_feature_major = lambda c: jnp.transpose(c, (0, 2, 3, 1)).reshape(n_s, ATT_KV_WIDTH, WINDOW)
    to_window_major = lambda c: jnp.transpose(
        c.reshape(n_s, ATT_KV_HEADS, ATT_HEAD_DIM, WINDOW), (0, 3, 1, 2))[None]
    n_groups = ATT_WIDTH // LANES
    q, kv, gate, kvpad, ogs, ckt_new, cvt_new = _attn_inproj(
        xp, g0, w_in_a, BF16,
        sample=(qs.reshape(n_s, n_groups, LANES), kvs,
                to_feature_major(cache_swa_k[0]), to_feature_major(cache_swa_v[0]),
                gates.reshape(n_s, n_groups, LANES), bias_s, sinks_rep))
    og = _swa_prompt(sinks, q, kvpad, gate, bias, batch, seq)
    xp1 = _outproj(og, w_out_a, xp)
    kv_last = kv.reshape(batch, seq, 2 * ATT_KV_WIDTH)[:, seq - WINDOW:]
    k_prompt = kv_last[..., :ATT_KV_WIDTH].reshape(1, batch, WINDOW, ATT_KV_HEADS, ATT_HEAD_DIM)
    v_prompt = kv_last[..., ATT_KV_WIDTH:].reshape(1, batch, WINDOW, ATT_KV_HEADS, ATT_HEAD_DIM)
    xs1 = _outproj(ogs.reshape(n_s, ATT_WIDTH), w_out_a, xs)
    k_sample = to_window_major(ckt_new)
    v_sample = to_window_major(cvt_new)

    w_kt = _transposed_columns(w_in_ret[0], RET_QK_WIDTH, RET_QK_WIDTH, BF16)
    q_s, kt_s, v_s, gate_s = _ret_inproj(xs1, g1, w_in_r, w_kt, np.full((n_s,), PAST_LEN), 1, F32)
    q, kt, v, gate, og_s, r_sample = _ret_inproj(
        xp1, g1, w_in_r, w_kt, np.arange(seq), RET_BLOCK, BF16,
        sample=(q_s, kt_s, v_s, gate_s, state_ret[0]))
    y_prompt, r_prompt = _ret_prompt(q, kt, v, gate, xp1, w_out_r, gf, batch, seq)
    y_prompt = y_prompt.reshape(batch, seq, D_MODEL)
    y_sample = _outproj(og_s, w_out_r, xs1, gf).reshape(n_s, 1, D_MODEL)

    return (y_prompt, y_sample, k_prompt, v_prompt, r_prompt[None],
            k_sample, v_sample, r_sample[None])
```

```python
import functools
import math

import numpy as np
import jax
import jax.numpy as jnp
from jax import lax
from jax.experimental import pallas as pl
from jax.experimental.pallas import tpu as pltpu

F32 = jnp.float32
BF16 = jnp.bfloat16

D_MODEL = 1024
PAST_LEN = 16384
ATT_HEADS = 32
ATT_KV_HEADS = 4
ATT_GROUP = ATT_HEADS // ATT_KV_HEADS
ATT_HEAD_DIM = 64
ATT_WIDTH = ATT_HEADS * ATT_HEAD_DIM
ATT_KV_WIDTH = ATT_KV_HEADS * ATT_HEAD_DIM
ATT_IN = 2 * ATT_WIDTH + 2 * ATT_KV_WIDTH
WINDOW = 128
NUM_BUCKETS = 32
MAX_DISTANCE = 128
RET_HEADS = 4
RET_QK_DIM = 256
RET_V_DIM = 512
RET_QK_WIDTH = RET_HEADS * RET_QK_DIM
RET_WIDTH = RET_HEADS * RET_V_DIM
RET_IN = 2 * RET_QK_WIDTH + 2 * RET_WIDTH
ROPE_BASE = 10000.0
EPS = 1e-6
NEG = -1e30
LOG2E = math.log2(math.e)

LANES = 128
PADDED_KV_VARIANTS = 4
PADDED_KV_WIDTH = ATT_KV_HEADS * PADDED_KV_VARIANTS * LANES
PROJ_ROWS = 512
ATT_PROJ_ROWS = 1024
PROJ_COLS = 512
OUT_ROWS = 1024
ATT_ROWS = 512
RET_BLOCK = 256
RET_ROWS = 1024
SAMPLE_ATT_BATCH = 4
SAMPLE_RET_BATCH = 2
VMEM_LIMIT = 60 * 1024 * 1024


def _params(n_axes):
    return pltpu.CompilerParams(
        dimension_semantics=("arbitrary",) * n_axes,
        vmem_limit_bytes=VMEM_LIMIT)


def _resident(shape, index_map):
    return pl.BlockSpec(shape, index_map, pipeline_mode=pl.Buffered(1))


def _silu(x):
    hx = 0.5 * x
    return hx + hx * jnp.tanh(hx)


def _rmsnorm(x, g):
    return x * lax.rsqrt(jnp.mean(x * x, axis=-1, keepdims=True) + EPS) * g


def _t5_bucket_np(rel):
    n = np.maximum(rel, 0)
    max_exact = NUM_BUCKETS // 2
    nf = np.maximum(n, 1).astype(np.float64)
    large = max_exact + (np.log(nf / max_exact) / math.log(MAX_DISTANCE / max_exact)
                         * (NUM_BUCKETS - max_exact)).astype(np.int32)
    large = np.minimum(large, NUM_BUCKETS - 1)
    return np.where(n < max_exact, n, large).astype(np.int32)


def _bucket_map_np():
    i = np.arange(WINDOW)[:, None]
    j = np.arange(WINDOW)[None, :]
    rel = np.where(j <= i, i - j, i - j + WINDOW)
    return _t5_bucket_np(rel).astype(np.int32)


def _gammas():
    return [1.0 - 2.0 ** (-5 - h) for h in range(RET_HEADS)]


def _decay_scales_np(rows, block):
    lg = np.log1p(-np.exp2(-5.0 - np.arange(RET_HEADS, dtype=np.float64)))
    i1 = (np.arange(rows) % block + 1.0)[None, :] * lg[:, None]
    q_scale = np.repeat(np.exp(i1)[:, :, None], LANES, axis=2)
    k_scale = np.repeat((np.exp(-i1) * RET_QK_DIM ** -0.5)[:, None, :], LANES, axis=1)
    return (q_scale.astype(np.float32), k_scale.astype(np.float32),
            np.exp(block * lg).astype(np.float32))


def _rope_tables_np(pos):
    half = RET_QK_DIM // 2
    inv = ROPE_BASE ** (-np.arange(half, dtype=np.float64) / half)
    ang = np.asarray(pos, np.float64)[:, None] * inv[None, :]
    return np.cos(ang).astype(np.float32), np.sin(ang).astype(np.float32)


def _bias_table_kernel(rb_ref, bucket_ref, out_ref):
    bk = bucket_ref[...]
    previous_block = (lax.broadcasted_iota(jnp.int32, bk.shape, 1)
                      > lax.broadcasted_iota(jnp.int32, bk.shape, 0))

    def one_head(h, carry):
        acc = jnp.zeros(bk.shape, F32)
        for b in range(NUM_BUCKETS):
            acc = jnp.where(bk == b, rb_ref[b, h] * LOG2E, acc)
        out_ref[1, h] = acc
        out_ref[0, h] = jnp.where(previous_block, NEG, acc)
        return carry

    lax.fori_loop(0, ATT_HEADS, one_head, 0)


def _bias_table(rel_bias):
    bucket = jnp.asarray(_bucket_map_np())
    return pl.pallas_call(
        _bias_table_kernel,
        grid=(1,),
        in_specs=[pl.BlockSpec(memory_space=pltpu.SMEM),
                  pl.BlockSpec((WINDOW, WINDOW), lambda i: (0, 0))],
        out_specs=pl.BlockSpec((2, ATT_HEADS, WINDOW, WINDOW), lambda i: (0, 0, 0, 0)),
        out_shape=jax.ShapeDtypeStruct((2, ATT_HEADS, WINDOW, WINDOW), F32),
        compiler_params=_params(1),
        name="bias_table",
    )(rel_bias, bucket)


def _transpose_cast_kernel(w_ref, o_ref):
    o_ref[...] = w_ref[...].T.astype(o_ref.dtype)


def _transposed_columns(w, col0, ncols, dtype):
    rows = w.shape[0]
    blk = 2 * LANES
    return pl.pallas_call(
        _transpose_cast_kernel,
        grid=(ncols // blk,),
        in_specs=[pl.BlockSpec((rows, blk), lambda i: (0, col0 // blk + i))],
        out_specs=pl.BlockSpec((blk, rows), lambda i: (i, 0)),
        out_shape=jax.ShapeDtypeStruct((ncols, rows), dtype),
        compiler_params=_params(1),
        name="transpose_cast",
    )(w)


def _pad_pair(x, x_rolled, lo, e):
    if e == 0:
        return jnp.where(lo, x, 0.0), jnp.where(lo, 0.0, x_rolled)
    return jnp.where(lo, x_rolled, 0.0), jnp.where(lo, 0.0, x)


def _attn_inproj_kernel(*refs, hosts_sample):
    x_ref, g_ref, w_ref = refs[:3]
    side_work = []
    kvpad_ref = None
    if hosts_sample:
        qs_ref, kvs_ref, ckt_ref, cvt_ref, gates_ref, bias_ref, sink_ref = refs[3:10]
        q_ref, kv_ref, gate_ref, kvpad_ref, ogs_ref, cko_ref, cvo_ref, kvt_ref = refs[10:]
        step = pl.program_id(0)

        @pl.when(step == 0)
        def _():
            kvt_ref[...] = kvs_ref[...].T

        side_work = _swa_sample_pieces(step, qs_ref, kvt_ref, ckt_ref, cvt_ref, gates_ref,
                                       bias_ref, sink_ref, ogs_ref, cko_ref, cvo_ref)
    else:
        q_ref, kv_ref, gate_ref = refs[3:]

    h = _rmsnorm(x_ref[...], g_ref[...]).astype(BF16)
    for c in range(0, ATT_IN, PROJ_COLS):
        if side_work:
            side_work.pop(0)()
        r = jnp.dot(h, w_ref[:, c:c + PROJ_COLS], preferred_element_type=F32)
        if c < ATT_WIDTH:
            q_ref[:, c:c + PROJ_COLS] = (r * (ATT_HEAD_DIM ** -0.5 * LOG2E)).astype(q_ref.dtype)
        elif c < ATT_WIDTH + 2 * ATT_KV_WIDTH:
            kv_ref[...] = r
            if kvpad_ref is not None:
                lo = lax.broadcasted_iota(jnp.int32, (r.shape[0], LANES), 1) < ATT_HEAD_DIM
                for pair in range(ATT_KV_HEADS // 2):
                    for is_v in range(2):
                        c1 = is_v * ATT_KV_WIDTH + pair * LANES
                        x2 = r[:, c1:c1 + LANES]
                        x2r = pltpu.roll(x2, ATT_HEAD_DIM, axis=1)
                        for e in range(2):
                            g0 = ((2 * pair + e) * PADDED_KV_VARIANTS + 2 * is_v) * LANES
                            lo_half, hi_half = _pad_pair(x2, x2r, lo, e)
                            kvpad_ref[:, g0:g0 + LANES] = lo_half.astype(kvpad_ref.dtype)
                            kvpad_ref[:, g0 + LANES:g0 + 2 * LANES] = hi_half.astype(kvpad_ref.dtype)
        else:
            o = c - ATT_WIDTH - 2 * ATT_KV_WIDTH
            gate_ref[:, o:o + PROJ_COLS] = r.astype(gate_ref.dtype)
    while side_work:
        side_work.pop(0)()


def _attn_inproj(x, g, w, out_dtype, sample=None):
    m = x.shape[0]
    tm = min(ATT_PROJ_ROWS, m)
    row = lambda i: (i, 0)
    const2 = lambda i: (0, 0)
    const3 = lambda i: (0, 0, 0)
    in_specs = [pl.BlockSpec((tm, D_MODEL), row),
                pl.BlockSpec((1, D_MODEL), const2),
                _resident((D_MODEL, ATT_IN), const2)]
    out_specs = [pl.BlockSpec((tm, ATT_WIDTH), row),
                 pl.BlockSpec((tm, 2 * ATT_KV_WIDTH), row),
                 pl.BlockSpec((tm, ATT_WIDTH), row)]
    out_shape = [jax.ShapeDtypeStruct((m, ATT_WIDTH), out_dtype),
                 jax.ShapeDtypeStruct((m, 2 * ATT_KV_WIDTH), F32),
                 jax.ShapeDtypeStruct((m, ATT_WIDTH), out_dtype)]
    args = [x, g, w]
    scratch_shapes = []
    if sample is not None:
        q_s, kv_s, ckt, cvt, gate_s, bias_s, sinks_rep = sample
        n, n_groups, _ = q_s.shape
        assert n == SAMPLE_ATT_BATCH * (m // tm), (n, m, tm)
        cache_spec = pl.BlockSpec((SAMPLE_ATT_BATCH, ATT_KV_WIDTH, WINDOW), lambda i: (i, 0, 0))
        in_specs += [_resident((n, n_groups, LANES), const3),
                     _resident((n, 2 * ATT_KV_WIDTH), const2),
                     cache_spec, cache_spec,
                     _resident((n, n_groups, LANES), const3),
                     _resident((ATT_HEADS, WINDOW), const2),
                     _resident((ATT_HEADS, LANES), const2)]
        out_specs += [pl.BlockSpec((tm, PADDED_KV_WIDTH), row),
                      pl.BlockSpec((n, n_groups, LANES), const3), cache_spec, cache_spec]
        out_shape += [jax.ShapeDtypeStruct((m, PADDED_KV_WIDTH), BF16),
                      jax.ShapeDtypeStruct((n, n_groups, LANES), F32),
                      jax.ShapeDtypeStruct((n, ATT_KV_WIDTH, WINDOW), F32),
                      jax.ShapeDtypeStruct((n, ATT_KV_WIDTH, WINDOW), F32)]
        args += list(sample)
        scratch_shapes = [pltpu.VMEM((2 * ATT_KV_WIDTH, n), F32)]
    return pl.pallas_call(
        functools.partial(_attn_inproj_kernel, hosts_sample=sample is not None),
        grid=(m // tm,),
        in_specs=in_specs,
        out_specs=out_specs,
        out_shape=out_shape,
        scratch_shapes=scratch_shapes,
        compiler_params=_params(1),
        name="attn_inproj",
    )(*args)


def _ret_sample_pieces(step, qt_ref, kt_ref, v_ref, gate_ref, st_ref, og_ref, sto_ref):
    lane = lax.broadcasted_iota(jnp.int32, (RET_QK_DIM, LANES), 1)
    gammas = _gammas()

    def piece(i, h):
        b = step * SAMPLE_RET_BATCH + i
        pick = lane == b
        rows = slice(h * RET_QK_DIM, (h + 1) * RET_QK_DIM)
        qc = jnp.sum(jnp.where(pick, qt_ref[rows, :], 0.0), axis=1, keepdims=True)
        kc = jnp.sum(jnp.where(pick, kt_ref[rows, :], 0.0), axis=1, keepdims=True)
        vcols = slice(h * RET_V_DIM, (h + 1) * RET_V_DIM)
        vrow = v_ref[pl.ds(b, 1), vcols]
        state = st_ref[i, h]
        o = (jnp.sum(qc * state, axis=0, keepdims=True)
             + jnp.sum(qc * kc, axis=0, keepdims=True) * vrow)
        sto_ref[i, h] = gammas[h] * (state + kc * vrow)
        og_ref[pl.ds(b, 1), vcols] = _group_norm_gate(o, gate_ref[pl.ds(b, 1), vcols])

    return [functools.partial(piece, i, h)
            for i in range(SAMPLE_RET_BATCH) for h in range(RET_HEADS)]


def _ret_inproj_kernel(*refs, with_sample):
    (x_ref, g_ref, w_ref, wkt_ref, cos_ref, sin_ref, cost_ref, sint_ref, qsc_ref, ksc_ref) = refs[:10]
    side_work = []
    if with_sample:
        qs_ref, kts_ref, vs_ref, gates_ref, st_ref = refs[10:15]
        q_ref, kt_ref, v_ref, gate_ref, ogs_ref, sto_ref, qts_ref = refs[15:]
        step = pl.program_id(0)

        @pl.when(step == 0)
        def _():
            qts_ref[...] = qs_ref[...].T

        side_work = _ret_sample_pieces(step, qts_ref, kts_ref, vs_ref, gates_ref, st_ref,
                                       ogs_ref, sto_ref)
    else:
        q_ref, kt_ref, v_ref, gate_ref = refs[10:]

    def interleave():
        if side_work:
            side_work.pop(0)()

    h = _rmsnorm(x_ref[...], g_ref[...]).astype(BF16)
    half = RET_QK_DIM // 2
    contract_last = (((1,), (1,)), ((), ()))
    cos = cos_ref[...]
    sin = sin_ref[...]
    for c in range(0, RET_QK_WIDTH, PROJ_COLS):
        interleave()
        r = jnp.dot(h, w_ref[:, c:c + PROJ_COLS], preferred_element_type=F32)
        for o in range(0, PROJ_COLS, RET_QK_DIM):
            scale = qsc_ref[(c + o) // RET_QK_DIM]
            x1 = r[:, o:o + half]
            x2 = r[:, o + half:o + RET_QK_DIM]
            q_ref[:, c + o:c + o + half] = ((x1 * cos - x2 * sin) * scale).astype(q_ref.dtype)
            q_ref[:, c + o + half:c + o + RET_QK_DIM] = (
                (x1 * sin + x2 * cos) * scale).astype(q_ref.dtype)
    cos = cost_ref[...]
    sin = sint_ref[...]
    for c in range(0, RET_QK_WIDTH, PROJ_COLS):
        interleave()
        r = lax.dot_general(wkt_ref[c:c + PROJ_COLS, :], h, contract_last,
                            preferred_element_type=F32)
        for o in range(0, PROJ_COLS, RET_QK_DIM):
            scale = ksc_ref[(c + o) // RET_QK_DIM]
            x1 = r[o:o + half, :]
            x2 = r[o + half:o + RET_QK_DIM, :]
            kt_ref[c + o:c + o + half, :] = ((x1 * cos - x2 * sin) * scale).astype(kt_ref.dtype)
            kt_ref[c + o + half:c + o + RET_QK_DIM, :] = (
                (x1 * sin + x2 * cos) * scale).astype(kt_ref.dtype)
    for c in range(0, RET_WIDTH, PROJ_COLS):
        interleave()
        v_ref[:, c:c + PROJ_COLS] = jnp.dot(
            h, w_ref[:, 2 * RET_QK_WIDTH + c:2 * RET_QK_WIDTH + c + PROJ_COLS],
            preferred_element_type=F32).astype(v_ref.dtype)
    for c in range(0, RET_WIDTH, PROJ_COLS):
        interleave()
        w0 = 2 * RET_QK_WIDTH + RET_WIDTH + c
        gate_ref[:, c:c + PROJ_COLS] = jnp.dot(
            h, w_ref[:, w0:w0 + PROJ_COLS], preferred_element_type=F32).astype(gate_ref.dtype)
    while side_work:
        interleave()


def _ret_inproj(x, g, w, wkt, pos, block, out_dtype, sample=None):
    m = x.shape[0]
    tm = min(PROJ_ROWS, m)
    n_pos = len(pos) // tm
    half = RET_QK_DIM // 2
    cos, sin = _rope_tables_np(pos)
    q_scale, k_scale, _ = _decay_scales_np(tm, block)
    row = lambda i: (i, 0)
    const2 = lambda i: (0, 0)
    const3 = lambda i: (0, 0, 0)
    in_specs = [pl.BlockSpec((tm, D_MODEL), row),
                pl.BlockSpec((1, D_MODEL), const2),
                _resident((D_MODEL, RET_IN), const2),
                _resident((RET_QK_WIDTH, D_MODEL), const2),
                pl.BlockSpec((tm, half), lambda i: (i % n_pos, 0)),
                pl.BlockSpec((tm, half), lambda i: (i % n_pos, 0)),
                pl.BlockSpec((half, tm), lambda i: (0, i % n_pos)),
                pl.BlockSpec((half, tm), lambda i: (0, i % n_pos)),
                _resident((RET_HEADS, tm, LANES), const3),
                _resident((RET_HEADS, LANES, tm), const3)]
    out_specs = [pl.BlockSpec((tm, RET_QK_WIDTH), row),
                 pl.BlockSpec((RET_QK_WIDTH, tm), lambda i: (0, i)),
                 pl.BlockSpec((tm, RET_WIDTH), row),
                 pl.BlockSpec((tm, RET_WIDTH), row)]
    out_shape = [jax.ShapeDtypeStruct((m, RET_QK_WIDTH), out_dtype),
                 jax.ShapeDtypeStruct((RET_QK_WIDTH, m), out_dtype),
                 jax.ShapeDtypeStruct((m, RET_WIDTH), out_dtype),
                 jax.ShapeDtypeStruct((m, RET_WIDTH), out_dtype)]
    args = [x, g, w, wkt, jnp.asarray(cos), jnp.asarray(sin), jnp.asarray(cos.T),
            jnp.asarray(sin.T), jnp.asarray(q_scale), jnp.asarray(k_scale)]
    scratch_shapes = []
    if sample is not None:
        q_s, kt_s, v_s, gate_s, state = sample
        n = q_s.shape[0]
        assert n == SAMPLE_RET_BATCH * (m // tm), (n, m, tm)
        st_spec = pl.BlockSpec((SAMPLE_RET_BATCH, RET_HEADS, RET_QK_DIM, RET_V_DIM),
                               lambda i: (i, 0, 0, 0))
        in_specs += [_resident((n, RET_QK_WIDTH), const2), _resident((RET_QK_WIDTH, n), const2),
                     _resident((n, RET_WIDTH), const2), _resident((n, RET_WIDTH), const2), st_spec]
        out_specs += [pl.BlockSpec((n, RET_WIDTH), const2), st_spec]
        out_shape += [jax.ShapeDtypeStruct((n, RET_WIDTH), F32),
                      jax.ShapeDtypeStruct(state.shape, F32)]
        args += [q_s, kt_s, v_s, gate_s, state]
        scratch_shapes = [pltpu.VMEM((RET_QK_WIDTH, n), F32)]
    return pl.pallas_call(
        functools.partial(_ret_inproj_kernel, with_sample=sample is not None),
        grid=(m // tm,),
        in_specs=in_specs,
        out_specs=out_specs,
        out_shape=out_shape,
        scratch_shapes=scratch_shapes,
        compiler_params=_params(1),
        name="ret_inproj",
    )(*args)


def _outproj_kernel(o_ref, w_ref, x_ref, y_ref):
    y_ref[...] = x_ref[...] + jnp.dot(o_ref[...].astype(BF16), w_ref[...],
                                      preferred_element_type=F32)


def _outproj_norm_kernel(o_ref, w_ref, x_ref, g_ref, y_ref):
    y = x_ref[...] + jnp.dot(o_ref[...].astype(BF16), w_ref[...], preferred_element_type=F32)
    y_ref[...] = _rmsnorm(y, g_ref[...])


def _outproj(o, w, x, final_g=None):
    m, width = o.shape
    tm = min(OUT_ROWS, m)
    row = lambda i: (i, 0)
    in_specs = [pl.BlockSpec((tm, width), row),
                _resident((width, D_MODEL), lambda i: (0, 0)),
                pl.BlockSpec((tm, D_MODEL), row)]
    args = [o, w, x]
    body = _outproj_kernel
    if final_g is not None:
        in_specs.append(pl.BlockSpec((1, D_MODEL), lambda i: (0, 0)))
        args.append(final_g)
        body = _outproj_norm_kernel
    return pl.pallas_call(
        body,
        grid=(m // tm,),
        in_specs=in_specs,
        out_specs=pl.BlockSpec((tm, D_MODEL), row),
        out_shape=jax.ShapeDtypeStruct((m, D_MODEL), F32),
        compiler_params=_params(1),
        name="outproj",
    )(*args)


def _swa_prompt_kernel(sink_ref, q_ref, kvc_ref, kvp_ref, gate_ref, bias_ref, out_ref):
    j = pl.program_id(1)
    half = ATT_HEAD_DIM
    pairs_per_kv = ATT_GROUP // 2
    kv_group_cols = ATT_GROUP * ATT_HEAD_DIM
    contract_last = (((1,), (1,)), ((), ()))
    lo1 = lax.broadcasted_iota(jnp.int32, (WINDOW, LANES), 1) < half
    own = (lax.broadcasted_iota(jnp.int32, (WINDOW, WINDOW), 1)
           <= lax.broadcasted_iota(jnp.int32, (WINDOW, WINDOW), 0))
    units =[(blk, kvh) for blk in range(ATT_ROWS // WINDOW) for kvh in range(ATT_KV_HEADS)]

    def keys_values(blk, kvh):
        r0 = blk * WINDOW
        out = []
        for variant in range(PADDED_KV_VARIANTS):
            c = (kvh * PADDED_KV_VARIANTS + variant) * LANES
            prev = kvp_ref[:, c:c + LANES] if blk == 0 else kvc_ref[r0 - WINDOW:r0, c:c + LANES]
            out.append(jnp.concatenate([prev, kvc_ref[r0:r0 + WINDOW, c:c + LANES]], axis=0))
        return out

    def scores(blk, kvh):
        rows = slice(blk * WINDOW, (blk + 1) * WINDOW)
        c0 = kvh * kv_group_cols
        k_a, k_b, v_a, v_b = keys_values(blk, kvh)
        qs = jnp.concatenate(
            [q_ref[rows, c0 + p * LANES:c0 + (p + 1) * LANES] for p in range(pairs_per_kv)],
            axis=0)
        logits = (lax.dot_general(qs, k_a, contract_last, preferred_element_type=F32),
                  lax.dot_general(qs, k_b, contract_last, preferred_element_type=F32))
        return logits, v_a, v_b

    def softmax(blk, kvh, logits):
        sel = jnp.where(j == 0, 0, 1) if blk == 0 else 1
        probs = ([], [])
        inv = ([], [])
        for p in range(pairs_per_kv):
            for st in range(2):
                hd = kvh * ATT_GROUP + 2 * p + st
                lg = logits[st][p * WINDOW:(p + 1) * WINDOW, :]
                l = jnp.where(own, lg[:, WINDOW:], lg[:, :WINDOW]) + bias_ref[sel, hd]
                s = sink_ref[hd]
                m = jnp.maximum(jnp.max(l, axis=-1, keepdims=True), s)
                pe = jnp.exp2(l - m)
                den = jnp.sum(pe, axis=-1, keepdims=True) + jnp.exp2(s - m)
                probs[st].append(jnp.concatenate(
                    [jnp.where(own, 0.0, pe).astype(BF16),
                     jnp.where(own, pe, 0.0).astype(BF16)], axis=1))
                inv[st].append(1.0 / den)
        return jnp.concatenate(probs[0], axis=0), jnp.concatenate(probs[1], axis=0), inv

    def output(blk, kvh, p_a, p_b, inv, v_a, v_b):
        rows = slice(blk * WINDOW, (blk + 1) * WINDOW)
        c0 = kvh * kv_group_cols
        o = (jnp.dot(p_a, v_a, preferred_element_type=F32)
             + jnp.dot(p_b, v_b, preferred_element_type=F32))
        for p in range(pairs_per_kv):
            cols = slice(c0 + p * LANES, c0 + (p + 1) * LANES)
            scale = jnp.where(lo1, inv[0][p], inv[1][p])
            gt = gate_ref[rows, cols].astype(F32)
            og = o[p * WINDOW:(p + 1) * WINDOW, :] * scale * _silu(gt)
            out_ref[rows, cols] = og.astype(out_ref.dtype)

    for unit in units:
        logits, v_a, v_b = scores(*unit)
        p_a, p_b, inv = softmax(*unit, logits)
        output(*unit, p_a, p_b, inv, v_a, v_b)


def _swa_prompt(sinks, q, kvpad, gate, bias, batch, seq):
    m = batch * seq
    nt = seq // ATT_ROWS
    blocks_per_tile = ATT_ROWS // WINDOW
    blocks_per_seq = seq // WINDOW
    tile = lambda b, j: (b * nt + j, 0)
    prev = lambda b, j: (b * blocks_per_seq + jnp.maximum(j * blocks_per_tile - 1, 0), 0)
    return pl.pallas_call(
        _swa_prompt_kernel,
        grid=(batch, nt),
        in_specs=[pl.BlockSpec(memory_space=pltpu.SMEM),
                  pl.BlockSpec((ATT_ROWS, ATT_WIDTH), tile),
                  pl.BlockSpec((ATT_ROWS, PADDED_KV_WIDTH), tile),
                  pl.BlockSpec((WINDOW, PADDED_KV_WIDTH), prev),
                  pl.BlockSpec((ATT_ROWS, ATT_WIDTH), tile),
                  pl.BlockSpec((2, ATT_HEADS, WINDOW, WINDOW), lambda b, j: (0, 0, 0, 0))],
        out_specs=pl.BlockSpec((ATT_ROWS, ATT_WIDTH), tile),
        out_shape=jax.ShapeDtypeStruct((m, ATT_WIDTH), BF16),
        compiler_params=_params(2),
        name="swa_prompt",
    )(sinks, q, kvpad, kvpad, gate, bias)


def _sample_head_order():
    order = []
    for pair in range(ATT_KV_HEADS // 2):
        base = pair * 2 * ATT_GROUP
        for kv in range(2):
            for half in range(2):
                order += [base + kv * ATT_GROUP + 2 * g + half for g in range(ATT_GROUP // 2)]
    return np.asarray(order, np.int32)


def _swa_sample_pieces(step, q_ref, kvt_ref, ckt_ref, cvt_ref, gate_ref, bias_ref, sink_ref,
                       o_ref, cko_ref, cvo_ref):
    half = ATT_HEAD_DIM
    quarter = ATT_GROUP // 2
    contract_last = (((1,), (1,)), ((), ()))
    newest = lax.broadcasted_iota(jnp.int32, (ATT_KV_WIDTH, WINDOW), 1) == WINDOW - 1
    seq_lane = lax.broadcasted_iota(jnp.int32, (2 * ATT_KV_WIDTH, LANES), 1)
    lo4 = lax.broadcasted_iota(jnp.int32, (quarter, LANES), 1) < half
    new_cache = {}

    def update(i):
        b = step * SAMPLE_ATT_BATCH + i
        new_col = jnp.sum(jnp.where(seq_lane == b, kvt_ref[...], 0.0), axis=1, keepdims=True)
        kb = jnp.where(newest, new_col[:ATT_KV_WIDTH], pltpu.roll(ckt_ref[i], WINDOW - 1, axis=1))
        vb = jnp.where(newest, new_col[ATT_KV_WIDTH:], pltpu.roll(cvt_ref[i], WINDOW - 1, axis=1))
        cko_ref[i] = kb
        cvo_ref[i] = vb
        new_cache[i] = (kb, vb)

    def attend(i, pair):
        b = step * SAMPLE_ATT_BATCH + i
        kb, vb = new_cache[i]
        groups = slice(pair * ATT_GROUP, (pair + 1) * ATT_GROUP)
        hs = slice(pair * 2 * ATT_GROUP, (pair + 1) * 2 * ATT_GROUP)
        feat = slice(pair * LANES, (pair + 1) * LANES)
        g8 = q_ref[b, groups, :]
        g8r = pltpu.roll(g8, half, axis=1)
        qbd = jnp.concatenate(
            [jnp.where(lo4, g8[:quarter], 0.0), jnp.where(lo4, g8r[:quarter], 0.0),
             jnp.where(lo4, 0.0, g8r[quarter:]), jnp.where(lo4, 0.0, g8[quarter:])],
            axis=0)
        l = jnp.dot(qbd, kb[feat, :], preferred_element_type=F32) + bias_ref[hs, :]
        s = sink_ref[hs, 0:1]
        m = jnp.maximum(jnp.max(l, axis=-1, keepdims=True), s)
        pe = jnp.exp2(l - m)
        den = jnp.sum(pe, axis=-1, keepdims=True) + jnp.exp2(s - m)
        o2 = lax.dot_general(pe, vb[feat, :], contract_last,
                             preferred_element_type=F32) / den
        o2r = pltpu.roll(o2, half, axis=1)
        og = jnp.concatenate(
            [jnp.where(lo4, o2[:quarter], o2r[quarter:2 * quarter]),
             jnp.where(lo4, o2r[2 * quarter:3 * quarter], o2[3 * quarter:])], axis=0)
        o_ref[b, groups, :] = og * _silu(gate_ref[b, groups, :])

    pieces = []
    for i in range(SAMPLE_ATT_BATCH):
        pieces.append(functools.partial(update, i))
        pieces += [functools.partial(attend, i, pair) for pair in range(ATT_KV_HEADS // 2)]
    return pieces


def _group_norm_gate(o, gate):
    mu = jnp.mean(o, axis=-1, keepdims=True)
    d = o - mu
    var = jnp.mean(d * d, axis=-1, keepdims=True)
    return d * lax.rsqrt(var + EPS) * _silu(gate)


def _ret_prompt_kernel(g_block_ref, q_ref, kt_ref, v_ref, gate_ref, x_ref, wout_ref, gf_ref,
                       y_ref, r_ref, og_ref):
    c = pl.program_id(1)

    @pl.when(c == 0)
    def _():
        r_ref[...] = jnp.zeros_like(r_ref)

    causal = (lax.broadcasted_iota(jnp.int32, (RET_BLOCK, RET_BLOCK), 1)
              <= lax.broadcasted_iota(jnp.int32, (RET_BLOCK, RET_BLOCK), 0))
    n_blocks = RET_ROWS // RET_BLOCK
    for blk in range(n_blocks):
        rows = slice(blk * RET_BLOCK, (blk + 1) * RET_BLOCK)
        for h in range(RET_HEADS):
            qcols = slice(h * RET_QK_DIM, (h + 1) * RET_QK_DIM)
            vcols = slice(h * RET_V_DIM, (h + 1) * RET_V_DIM)
            qh = q_ref[rows, qcols]
            kt = kt_ref[qcols, rows]
            vh = v_ref[rows, vcols]
            state = r_ref[0, h]
            scores = jnp.where(causal, jnp.dot(qh, kt, preferred_element_type=F32), 0.0)
            o = (jnp.dot(scores.astype(BF16), vh, preferred_element_type=F32)
                 + jnp.dot(qh, state.astype(BF16), preferred_element_type=F32))
            r_ref[0, h] = g_block_ref[h] * (state + jnp.dot(kt, vh, preferred_element_type=F32))
            og_ref[h, rows, :] = _group_norm_gate(
                o, gate_ref[rows, vcols].astype(F32)).astype(og_ref.dtype)
            if blk == n_blocks - 1:
                part = jnp.dot(og_ref[h], wout_ref[vcols, :], preferred_element_type=F32)
                if h == 0:
                    y_ref[...] = x_ref[...] + part
                elif h < RET_HEADS - 1:
                    y_ref[...] += part
                else:
                    y_ref[...] = _rmsnorm(y_ref[...] + part, gf_ref[...])


def _ret_prompt(q, kt, v, gate, x, w_out, final_g, batch, seq):
    m = batch * seq
    nt = seq // RET_ROWS
    g_block = jnp.asarray(_decay_scales_np(RET_BLOCK, RET_BLOCK)[2])
    rows = lambda b, c: (b * nt + c, 0)
    return pl.pallas_call(
        _ret_prompt_kernel,
        grid=(batch, nt),
        in_specs=[pl.BlockSpec(memory_space=pltpu.SMEM),
                  pl.BlockSpec((RET_ROWS, RET_QK_WIDTH), rows),
                  pl.BlockSpec((RET_QK_WIDTH, RET_ROWS), lambda b, c: (0, b * nt + c)),
                  pl.BlockSpec((RET_ROWS, RET_WIDTH), rows),
                  pl.BlockSpec((RET_ROWS, RET_WIDTH), rows),
                  pl.BlockSpec((RET_ROWS, D_MODEL), rows),
                  _resident((RET_WIDTH, D_MODEL), lambda b, c: (0, 0)),
                  pl.BlockSpec((1, D_MODEL), lambda b, c: (0, 0))],
        out_specs=[pl.BlockSpec((RET_ROWS, D_MODEL), rows),
                   pl.BlockSpec((1, RET_HEADS, RET_QK_DIM, RET_V_DIM), lambda b, c: (b, 0, 0, 0))],
        out_shape=[jax.ShapeDtypeStruct((m, D_MODEL), F32),
                   jax.ShapeDtypeStruct((batch, RET_HEADS, RET_QK_DIM, RET_V_DIM), F32)],
        scratch_shapes=[pltpu.VMEM((RET_HEADS, RET_ROWS, RET_V_DIM), BF16)],
        compiler_params=_params(2),
        name="ret_prompt",
    )(g_block, q, kt, v, gate, x, w_out, final_g)


def kernel(x_prompt, x_sample, cache_swa_k, cache_swa_v, state_ret, norm_g, final_norm_g, rel_bias,
           w_in_attn, attn_sinks, w_out_attn, w_in_ret, w_out_ret):
    batch, seq, _ = x_prompt.shape
    n_s = x_sample.shape[0]
    xp = x_prompt.reshape(batch * seq, D_MODEL)
    xs = x_sample.reshape(n_s, D_MODEL)
    g0 = norm_g[0].reshape(1, D_MODEL)
    g1 = norm_g[1].reshape(1, D_MODEL)
    gf = final_norm_g.reshape(1, D_MODEL)
    w_in_a = w_in_attn[0].astype(BF16)
    w_out_a = w_out_attn[0].astype(BF16)
    w_in_r = w_in_ret[0].astype(BF16)
    w_out_r = w_out_ret[0].astype(BF16)
    sinks = attn_sinks[0] * LOG2E

    bias = _bias_table(rel_bias)

    qs, kvs, gates = _attn_inproj(xs, g0, w_in_a, F32)
    order = _sample_head_order()
    bias_s = bias[1, :, WINDOW - 1, :][order]
    sinks_rep = jnp.broadcast_to(sinks[order][:, None], (ATT_HEADS, LANES))
    to_feature_major = lambda c: jnp.transpose(c, (0, 2, 3, 1)).reshape(n_s, ATT_KV_WIDTH, WINDOW)
    to_window_major = lambda c: jnp.transpose(
        c.reshape(n_s, ATT_KV_HEADS, ATT_HEAD_DIM, WINDOW), (0, 3, 1, 2))[None]
    n_groups = ATT_WIDTH // LANES
    q, kv, gate, kvpad, ogs, ckt_new, cvt_new = _attn_inproj(
        xp, g0, w_in_a, BF16,
        sample=(qs.reshape(n_s, n_groups, LANES), kvs,
                to_feature_major(cache_swa_k[0]), to_feature_major(cache_swa_v[0]),
                gates.reshape(n_s, n_groups, LANES), bias_s, sinks_rep))
    og = _swa_prompt(sinks, q, kvpad, gate, bias, batch, seq)
    xp1 = _outproj(og, w_out_a, xp)
    kv_last = kv.reshape(batch, seq, 2 * ATT_KV_WIDTH)[:, seq - WINDOW:]
    k_prompt = kv_last[..., :ATT_KV_WIDTH].reshape(1, batch, WINDOW, ATT_KV_HEADS, ATT_HEAD_DIM)
    v_prompt = kv_last[..., ATT_KV_WIDTH:].reshape(1, batch, WINDOW, ATT_KV_HEADS, ATT_HEAD_DIM)
    xs1 = _outproj(ogs.reshape(n_s, ATT_WIDTH), w_out_a, xs)
    k_sample = to_window_major(ckt_new)
    v_sample = to_window_major(cvt_new)

    w_kt = _transposed_columns(w_in_ret[0], RET_QK_WIDTH, RET_QK_WIDTH, BF16)
    q_s, kt_s, v_s, gate_s = _ret_inproj(xs1, g1, w_in_r, w_kt, np.full((n_s,), PAST_LEN), 1, F32)
    q, kt, v, gate, og_s, r_sample = _ret_inproj(
        xp1, g1, w_in_r, w_kt, np.arange(seq), RET_BLOCK, BF16,
        sample=(q_s, kt_s, v_s, gate_s, state_ret[0]))
    y_prompt, r_prompt = _ret_prompt(q, kt, v, gate, xp1, w_out_r, gf, batch, seq)
    y_prompt = y_prompt.reshape(batch, seq, D_MODEL)
    y_sample = _outproj(og_s, w_out_r, xs1, gf).reshape(n_s, 1, D_MODEL)

    return (y_prompt, y_sample, k_prompt, v_prompt, r_prompt[None],
            k_sample, v_sample, r_sample[None])
```

```python
import functools
import math

import numpy as np
import jax
import jax.numpy as jnp
from jax import lax
from jax.experimental import pallas as pl
from jax.experimental.pallas import tpu as pltpu

F32 = jnp.float32
BF16 = jnp.bfloat16

D_MODEL = 1024
PAST_LEN = 16384
ATT_HEADS = 32
ATT_KV_HEADS = 4
ATT_GROUP = ATT_HEADS // ATT_KV_HEADS
ATT_HEAD_DIM = 64
ATT_WIDTH = ATT_HEADS * ATT_HEAD_DIM
ATT_KV_WIDTH = ATT_KV_HEADS * ATT_HEAD_DIM
ATT_IN = 2 * ATT_WIDTH + 2 * ATT_KV_WIDTH
WINDOW = 128
NUM_BUCKETS = 32
MAX_DISTANCE = 128
RET_HEADS = 4
RET_QK_DIM = 256
RET_V_DIM = 512
RET_QK_WIDTH = RET_HEADS * RET_QK_DIM
RET_WIDTH = RET_HEADS * RET_V_DIM
RET_IN = 2 * RET_QK_WIDTH + 2 * RET_WIDTH
ROPE_BASE = 10000.0
EPS = 1e-6
NEG = -1e30
LOG2E = math.log2(math.e)

LANES = 128
PADDED_KV_VARIANTS = 4
PADDED_KV_WIDTH = ATT_KV_HEADS * PADDED_KV_VARIANTS * LANES
PROJ_ROWS = 512
ATT_PROJ_ROWS = 1024
PROJ_COLS = 512
OUT_ROWS = 1024
ATT_ROWS = 512
RET_BLOCK = 256
RET_ROWS = 1024
SAMPLE_ATT_BATCH = 4
SAMPLE_RET_BATCH = 2
VMEM_LIMIT = 60 * 1024 * 1024


def _params(n_axes):
    return pltpu.CompilerParams(
        dimension_semantics=("arbitrary",) * n_axes,
        vmem_limit_bytes=VMEM_LIMIT)


def _resident(shape, index_map):
    return pl.BlockSpec(shape, index_map, pipeline_mode=pl.Buffered(1))


def _silu(x):
    hx = 0.5 * x
    return hx + hx * jnp.tanh(hx)


def _rmsnorm(x, g):
    return x * lax.rsqrt(jnp.mean(x * x, axis=-1, keepdims=True) + EPS) * g


def _t5_bucket_np(rel):
    n = np.maximum(rel, 0)
    max_exact = NUM_BUCKETS // 2
    nf = np.maximum(n, 1).astype(np.float64)
    large = max_exact + (np.log(nf / max_exact) / math.log(MAX_DISTANCE / max_exact)
                         * (NUM_BUCKETS - max_exact)).astype(np.int32)
    large = np.minimum(large, NUM_BUCKETS - 1)
    return np.where(n < max_exact, n, large).astype(np.int32)


def _bucket_map_np():
    i = np.arange(WINDOW)[:, None]
    j = np.arange(WINDOW)[None, :]
    rel = np.where(j <= i, i - j, i - j + WINDOW)
    return _t5_bucket_np(rel).astype(np.int32)


def _gammas():
    return [1.0 - 2.0 ** (-5 - h) for h in range(RET_HEADS)]


def _decay_scales_np(rows, block):
    lg = np.log1p(-np.exp2(-5.0 - np.arange(RET_HEADS, dtype=np.float64)))
    i1 = (np.arange(rows) % block + 1.0)[None, :] * lg[:, None]
    q_scale = np.repeat(np.exp(i1)[:, :, None], LANES, axis=2)
    k_scale = np.repeat((np.exp(-i1) * RET_QK_DIM ** -0.5)[:, None, :], LANES, axis=1)
    return (q_scale.astype(np.float32), k_scale.astype(np.float32),
            np.exp(block * lg).astype(np.float32))


def _rope_tables_np(pos):
    half = RET_QK_DIM // 2
    inv = ROPE_BASE ** (-np.arange(half, dtype=np.float64) / half)
    ang = np.asarray(pos, np.float64)[:, None] * inv[None, :]
    return np.cos(ang).astype(np.float32), np.sin(ang).astype(np.float32)


def _bias_table_kernel(rb_ref, bucket_ref, out_ref):
    bk = bucket_ref[...]
    previous_block = (lax.broadcasted_iota(jnp.int32, bk.shape, 1)
                      > lax.broadcasted_iota(jnp.int32, bk.shape, 0))

    def one_head(h, carry):
        acc = jnp.zeros(bk.shape, F32)
        for b in range(NUM_BUCKETS):
            acc = jnp.where(bk == b, rb_ref[b, h] * LOG2E, acc)
        out_ref[1, h] = acc
        out_ref[0, h] = jnp.where(previous_block, NEG, acc)
        return carry

    lax.fori_loop(0, ATT_HEADS, one_head, 0)


def _bias_table(rel_bias):
    bucket = jnp.asarray(_bucket_map_np())
    return pl.pallas_call(
        _bias_table_kernel,
        grid=(1,),
        in_specs=[pl.BlockSpec(memory_space=pltpu.SMEM),
                  pl.BlockSpec((WINDOW, WINDOW), lambda i: (0, 0))],
        out_specs=pl.BlockSpec((2, ATT_HEADS, WINDOW, WINDOW), lambda i: (0, 0, 0, 0)),
        out_shape=jax.ShapeDtypeStruct((2, ATT_HEADS, WINDOW, WINDOW), F32),
        compiler_params=_params(1),
        name="bias_table",
    )(rel_bias, bucket)


def _transpose_cast_kernel(w_ref, o_ref):
    o_ref[...] = w_ref[...].T.astype(o_ref.dtype)


def _transposed_columns(w, col0, ncols, dtype):
    rows = w.shape[0]
    blk = 2 * LANES
    return pl.pallas_call(
        _transpose_cast_kernel,
        grid=(ncols // blk,),
        in_specs=[pl.BlockSpec((rows, blk), lambda i: (0, col0 // blk + i))],
        out_specs=pl.BlockSpec((blk, rows), lambda i: (i, 0)),
        out_shape=jax.ShapeDtypeStruct((ncols, rows), dtype),
        compiler_params=_params(1),
        name="transpose_cast",
    )(w)


def _pad_pair(x, x_rolled, lo, e):
    if e == 0:
        return jnp.where(lo, x, 0.0), jnp.where(lo, 0.0, x_rolled)
    return jnp.where(lo, x_rolled, 0.0), jnp.where(lo, 0.0, x)


def _attn_inproj_kernel(*refs, hosts_sample):
    x_ref, g_ref, w_ref = refs[:3]
    side_work = []
    kvpad_ref = None
    if hosts_sample:
        qs_ref, kvs_ref, ckt_ref, cvt_ref, gates_ref, bias_ref, sink_ref = refs[3:10]
        q_ref, kv_ref, gate_ref, kvpad_ref, ogs_ref, cko_ref, cvo_ref, kvt_ref = refs[10:]
        step = pl.program_id(0)

        @pl.when(step == 0)
        def _():
            kvt_ref[...] = kvs_ref[...].T

        side_work = _swa_sample_pieces(step, qs_ref, kvt_ref, ckt_ref, cvt_ref, gates_ref,
                                       bias_ref, sink_ref, ogs_ref, cko_ref, cvo_ref)
    else:
        q_ref, kv_ref, gate_ref = refs[3:]

    h = _rmsnorm(x_ref[...], g_ref[...]).astype(BF16)
    for c in range(0, ATT_IN, PROJ_COLS):
        if side_work:
            side_work.pop(0)()
        r = jnp.dot(h, w_ref[:, c:c + PROJ_COLS], preferred_element_type=F32)
        if c < ATT_WIDTH:
            q_ref[:, c:c + PROJ_COLS] = (r * (ATT_HEAD_DIM ** -0.5 * LOG2E)).astype(q_ref.dtype)
        elif c < ATT_WIDTH + 2 * ATT_KV_WIDTH:
            kv_ref[...] = r
            if kvpad_ref is not None:
                lo = lax.broadcasted_iota(jnp.int32, (r.shape[0], LANES), 1) < ATT_HEAD_DIM
                for pair in range(ATT_KV_HEADS // 2):
                    for is_v in range(2):
                        c1 = is_v * ATT_KV_WIDTH + pair * LANES
                        x2 = r[:, c1:c1 + LANES]
                        x2r = pltpu.roll(x2, ATT_HEAD_DIM, axis=1)
                        for e in range(2):
                            g0 = ((2 * pair + e) * PADDED_KV_VARIANTS + 2 * is_v) * LANES
                            lo_half, hi_half = _pad_pair(x2, x2r, lo, e)
                            kvpad_ref[:, g0:g0 + LANES] = lo_half.astype(kvpad_ref.dtype)
                            kvpad_ref[:, g0 + LANES:g0 + 2 * LANES] = hi_half.astype(kvpad_ref.dtype)
        else:
            o = c - ATT_WIDTH - 2 * ATT_KV_WIDTH
            gate_ref[:, o:o + PROJ_COLS] = r.astype(gate_ref.dtype)
    while side_work:
        side_work.pop(0)()


def _attn_inproj(x, g, w, out_dtype, sample=None):
    m = x.shape[0]
    tm = min(ATT_PROJ_ROWS, m)
    row = lambda i: (i, 0)
    const2 = lambda i: (0, 0)
    const3 = lambda i: (0, 0, 0)
    in_specs = [pl.BlockSpec((tm, D_MODEL), row),
                pl.BlockSpec((1, D_MODEL), const2),
                _resident((D_MODEL, ATT_IN), const2)]
    out_specs = [pl.BlockSpec((tm, ATT_WIDTH), row),
                 pl.BlockSpec((tm, 2 * ATT_KV_WIDTH), row),
                 pl.BlockSpec((tm, ATT_WIDTH), row)]
    out_shape = [jax.ShapeDtypeStruct((m, ATT_WIDTH), out_dtype),
                 jax.ShapeDtypeStruct((m, 2 * ATT_KV_WIDTH), F32),
                 jax.ShapeDtypeStruct((m, ATT_WIDTH), out_dtype)]
    args = [x, g, w]
    scratch_shapes = []
    if sample is not None:
        q_s, kv_s, ckt, cvt, gate_s, bias_s, sinks_rep = sample
        n, n_groups, _ = q_s.shape
        assert n == SAMPLE_ATT_BATCH * (m // tm), (n, m, tm)
        cache_spec = pl.BlockSpec((SAMPLE_ATT_BATCH, ATT_KV_WIDTH, WINDOW), lambda i: (i, 0, 0))
        in_specs += [_resident((n, n_groups, LANES), const3),
                     _resident((n, 2 * ATT_KV_WIDTH), const2),
                     cache_spec, cache_spec,
                     _resident((n, n_groups, LANES), const3),
                     _resident((ATT_HEADS, WINDOW), const2),
                     _resident((ATT_HEADS, LANES), const2)]
        out_specs += [pl.BlockSpec((tm, PADDED_KV_WIDTH), row),
                      pl.BlockSpec((n, n_groups, LANES), const3), cache_spec, cache_spec]
        out_shape += [jax.ShapeDtypeStruct((m, PADDED_KV_WIDTH), BF16),
                      jax.ShapeDtypeStruct((n, n_groups, LANES), F32),
                      jax.ShapeDtypeStruct((n, ATT_KV_WIDTH, WINDOW), F32),
                      jax.ShapeDtypeStruct((n, ATT_KV_WIDTH, WINDOW), F32)]
        args += list(sample)
        scratch_shapes = [pltpu.VMEM((2 * ATT_KV_WIDTH, n), F32)]
    return pl.pallas_call(
        functools.partial(_attn_inproj_kernel, hosts_sample=sample is not None),
        grid=(m // tm,),
        in_specs=in_specs,
        out_specs=out_specs,
        out_shape=out_shape,
        scratch_shapes=scratch_shapes,
        compiler_params=_params(1),
        name="attn_inproj",
    )(*args)


def _ret_sample_pieces(step, qt_ref, kt_ref, v_ref, gate_ref, st_ref, og_ref, sto_ref):
    lane = lax.broadcasted_iota(jnp.int32, (RET_QK_DIM, LANES), 1)
    gammas = _gammas()

    def piece(i, h):
        b = step * SAMPLE_RET_BATCH + i
        pick = lane == b
        rows = slice(h * RET_QK_DIM, (h + 1) * RET_QK_DIM)
        qc = jnp.sum(jnp.where(pick, qt_ref[rows, :], 0.0), axis=1, keepdims=True)
        kc = jnp.sum(jnp.where(pick, kt_ref[rows, :], 0.0), axis=1, keepdims=True)
        vcols = slice(h * RET_V_DIM, (h + 1) * RET_V_DIM)
        vrow = v_ref[pl.ds(b, 1), vcols]
        state = st_ref[i, h]
        o = (jnp.sum(qc * state, axis=0, keepdims=True)
             + jnp.sum(qc * kc, axis=0, keepdims=True) * vrow)
        sto_ref[i, h] = gammas[h] * (state + kc * vrow)
        og_ref[pl.ds(b, 1), vcols] = _group_norm_gate(o, gate_ref[pl.ds(b, 1), vcols])

    return [functools.partial(piece, i, h)
            for i in range(SAMPLE_RET_BATCH) for h in range(RET_HEADS)]


def _ret_inproj_kernel(*refs, with_sample):
    (x_ref, g_ref, w_ref, wkt_ref, cos_ref, sin_ref, cost_ref, sint_ref, qsc_ref, ksc_ref) = refs[:10]
    side_work = []
    if with_sample:
        qs_ref, kts_ref, vs_ref, gates_ref, st_ref = refs[10:15]
        q_ref, kt_ref, v_ref, gate_ref, ogs_ref, sto_ref, qts_ref = refs[15:]
        step = pl.program_id(0)

        @pl.when(step == 0)
        def _():
            qts_ref[...] = qs_ref[...].T

        side_work = _ret_sample_pieces(step, qts_ref, kts_ref, vs_ref, gates_ref, st_ref,
                                       ogs_ref, sto_ref)
    else:
        q_ref, kt_ref, v_ref, gate_ref = refs[10:]

    def interleave():
        if side_work:
            side_work.pop(0)()

    h = _rmsnorm(x_ref[...], g_ref[...]).astype(BF16)
    half = RET_QK_DIM // 2
    contract_last = (((1,), (1,)), ((), ()))
    cos = cos_ref[...]
    sin = sin_ref[...]
    for c in range(0, RET_QK_WIDTH, PROJ_COLS):
        interleave()
        r = jnp.dot(h, w_ref[:, c:c + PROJ_COLS], preferred_element_type=F32)
        for o in range(0, PROJ_COLS, RET_QK_DIM):
            scale = qsc_ref[(c + o) // RET_QK_DIM]
            x1 = r[:, o:o + half]
            x2 = r[:, o + half:o + RET_QK_DIM]
            q_ref[:, c + o:c + o + half] = ((x1 * cos - x2 * sin) * scale).astype(q_ref.dtype)
            q_ref[:, c + o + half:c + o + RET_QK_DIM] = (
                (x1 * sin + x2 * cos) * scale).astype(q_ref.dtype)
    cos = cost_ref[...]
    sin = sint_ref[...]
    for c in range(0, RET_QK_WIDTH, PROJ_COLS):
        interleave()
        r = lax.dot_general(wkt_ref[c:c + PROJ_COLS, :], h, contract_last,
                            preferred_element_type=F32)
        for o in range(0, PROJ_COLS, RET_QK_DIM):
            scale = ksc_ref[(c + o) // RET_QK_DIM]
            x1 = r[o:o + half, :]
            x2 = r[o + half:o + RET_QK_DIM, :]
            kt_ref[c + o:c + o + half, :] = ((x1 * cos - x2 * sin) * scale).astype(kt_ref.dtype)
            kt_ref[c + o + half:c + o + RET_QK_DIM, :] = (
                (x1 * sin + x2 * cos) * scale).astype(kt_ref.dtype)
    for c in range(0, RET_WIDTH, PROJ_COLS):
        interleave()
        v_ref[:, c:c + PROJ_COLS] = jnp.dot(
            h, w_ref[:, 2 * RET_QK_WIDTH + c:2 * RET_QK_WIDTH + c + PROJ_COLS],
            preferred_element_type=F32).astype(v_ref.dtype)
    for c in range(0, RET_WIDTH, PROJ_COLS):
        interleave()
        w0 = 2 * RET_QK_WIDTH + RET_WIDTH + c
        gate_ref[:, c:c + PROJ_COLS] = jnp.dot(
            h, w_ref[:, w0:w0 + PROJ_COLS], preferred_element_type=F32).astype(gate_ref.dtype)
    while side_work:
        interleave()


def _ret_inproj(x, g, w, wkt, pos, block, out_dtype, sample=None):
    m = x.shape[0]
    tm = min(PROJ_ROWS, m)
    n_pos = len(pos) // tm
    half = RET_QK_DIM // 2
    cos, sin = _rope_tables_np(pos)
    q_scale, k_scale, _ = _decay_scales_np(tm, block)
    row = lambda i: (i, 0)
    const2 = lambda i: (0, 0)
    const3 = lambda i: (0, 0, 0)
    in_specs = [pl.BlockSpec((tm, D_MODEL), row),
                pl.BlockSpec((1, D_MODEL), const2),
                _resident((D_MODEL, RET_IN), const2),
                _resident((RET_QK_WIDTH, D_MODEL), const2),
                pl.BlockSpec((tm, half), lambda i: (i % n_pos, 0)),
                pl.BlockSpec((tm, half), lambda i: (i % n_pos, 0)),
                pl.BlockSpec((half, tm), lambda i: (0, i % n_pos)),
                pl.BlockSpec((half, tm), lambda i: (0, i % n_pos)),
                _resident((RET_HEADS, tm, LANES), const3),
                _resident((RET_HEADS, LANES, tm), const3)]
    out_specs = [pl.BlockSpec((tm, RET_QK_WIDTH), row),
                 pl.BlockSpec((RET_QK_WIDTH, tm), lambda i: (0, i)),
                 pl.BlockSpec((tm, RET_WIDTH), row),
                 pl.BlockSpec((tm, RET_WIDTH), row)]
    out_shape = [jax.ShapeDtypeStruct((m, RET_QK_WIDTH), out_dtype),
                 jax.ShapeDtypeStruct((RET_QK_WIDTH, m), out_dtype),
                 jax.ShapeDtypeStruct((m, RET_WIDTH), out_dtype),
                 jax.ShapeDtypeStruct((m, RET_WIDTH), out_dtype)]
    args = [x, g, w, wkt, jnp.asarray(cos), jnp.asarray(sin), jnp.asarray(cos.T),
            jnp.asarray(sin.T), jnp.asarray(q_scale), jnp.asarray(k_scale)]
    scratch_shapes = []
    if sample is not None:
        q_s, kt_s, v_s, gate_s, state = sample
        n = q_s.shape[0]
        assert n == SAMPLE_RET_BATCH * (m // tm), (n, m, tm)
        st_spec = pl.BlockSpec((SAMPLE_RET_BATCH, RET_HEADS, RET_QK_DIM, RET_V_DIM),
                               lambda i: (i, 0, 0, 0))
        in_specs += [_resident((n, RET_QK_WIDTH), const2), _resident((RET_QK_WIDTH, n), const2),
                     _resident((n, RET_WIDTH), const2), _resident((n, RET_WIDTH), const2), st_spec]
        out_specs += [pl.BlockSpec((n, RET_WIDTH), const2), st_spec]
        out_shape += [jax.ShapeDtypeStruct((n, RET_WIDTH), F32),
                      jax.ShapeDtypeStruct(state.shape, F32)]
        args += [q_s, kt_s, v_s, gate_s, state]
        scratch_shapes = [pltpu.VMEM((RET_QK_WIDTH, n), F32)]
    return pl.pallas_call(
        functools.partial(_ret_inproj_kernel, with_sample=sample is not None),
        grid=(m // tm,),
        in_specs=in_specs,
        out_specs=out_specs,
        out_shape=out_shape,
        scratch_shapes=scratch_shapes,
        compiler_params=_params(1),
        name="ret_inproj",
    )(*args)


def _outproj_kernel(o_ref, w_ref, x_ref, y_ref):
    y_ref[...] = x_ref[...] + jnp.dot(o_ref[...].astype(BF16), w_ref[...],
                                      preferred_element_type=F32)


def _outproj_norm_kernel(o_ref, w_ref, x_ref, g_ref, y_ref):
    y = x_ref[...] + jnp.dot(o_ref[...].astype(BF16), w_ref[...], preferred_element_type=F32)
    y_ref[...] = _rmsnorm(y, g_ref[...])


def _outproj(o, w, x, final_g=None):
    m, width = o.shape
    tm = min(OUT_ROWS, m)
    row = lambda i: (i, 0)
    in_specs = [pl.BlockSpec((tm, width), row),
                _resident((width, D_MODEL), lambda i: (0, 0)),
                pl.BlockSpec((tm, D_MODEL), row)]
    args = [o, w, x]
    body = _outproj_kernel
    if final_g is not None:
        in_specs.append(pl.BlockSpec((1, D_MODEL), lambda i: (0, 0)))
        args.append(final_g)
        body = _outproj_norm_kernel
    return pl.pallas_call(
        body,
        grid=(m // tm,),
        in_specs=in_specs,
        out_specs=pl.BlockSpec((tm, D_MODEL), row),
        out_shape=jax.ShapeDtypeStruct((m, D_MODEL), F32),
        compiler_params=_params(1),
        name="outproj",
    )(*args)


def _swa_prompt_kernel(sink_ref, q_ref, kvc_ref, kvp_ref, gate_ref, bias_ref, out_ref):
    j = pl.program_id(1)
    half = ATT_HEAD_DIM
    pairs_per_kv = ATT_GROUP // 2
    kv_group_cols = ATT_GROUP * ATT_HEAD_DIM
    contract_last = (((1,), (1,)), ((), ()))
    lo1 = lax.broadcasted_iota(jnp.int32, (WINDOW, LANES), 1) < half
    own = (lax.broadcasted_iota(jnp.int32, (WINDOW, WINDOW), 1)
           <= lax.broadcasted_iota(jnp.int32, (WINDOW, WINDOW), 0))
    units =[(blk, kvh) for blk in range(ATT_ROWS // WINDOW) for kvh in range(ATT_KV_HEADS)]

    def keys_values(blk, kvh):
        r0 = blk * WINDOW
        out = []
        for variant in range(PADDED_KV_VARIANTS):
            c = (kvh * PADDED_KV_VARIANTS + variant) * LANES
            prev = kvp_ref[:, c:c + LANES] if blk == 0 else kvc_ref[r0 - WINDOW:r0, c:c + LANES]
            out.append(jnp.concatenate([prev, kvc_ref[r0:r0 + WINDOW, c:c + LANES]], axis=0))
        return out

    def scores(blk, kvh):
        rows = slice(blk * WINDOW, (blk + 1) * WINDOW)
        c0 = kvh * kv_group_cols
        k_a, k_b, v_a, v_b = keys_values(blk, kvh)
        qs = jnp.concatenate(
            [q_ref[rows, c0 + p * LANES:c0 + (p + 1) * LANES] for p in range(pairs_per_kv)],
            axis=0)
        logits = (lax.dot_general(qs, k_a, contract_last, preferred_element_type=F32),
                  lax.dot_general(qs, k_b, contract_last, preferred_element_type=F32))
        return logits, v_a, v_b

    def softmax(blk, kvh, logits):
        sel = jnp.where(j == 0, 0, 1) if blk == 0 else 1
        probs = ([], [])
        inv = []
        for p in range(pairs_per_kv):
            top, total, sink = [], [], []
            for st in range(2):
                hd = kvh * ATT_GROUP + 2 * p + st
                lg = logits[st][p * WINDOW:(p + 1) * WINDOW, :]
                l = jnp.where(own, lg[:, WINDOW:], lg[:, :WINDOW]) + bias_ref[sel, hd]
                s = sink_ref[hd]
                m = jnp.maximum(jnp.max(l, axis=-1, keepdims=True), s)
                pe = jnp.exp2(l - m)
                probs[st].append(jnp.concatenate(
                    [jnp.where(own, 0.0, pe).astype(BF16),
                     jnp.where(own, pe, 0.0).astype(BF16)], axis=1))
                top.append(m)
                total.append(jnp.sum(pe, axis=-1, keepdims=True))
                sink.append(s)
            sink_term = jnp.exp2(jnp.where(lo1, sink[0], sink[1]) - jnp.where(lo1, top[0], top[1]))
            inv.append(1.0 / (jnp.where(lo1, total[0], total[1]) + sink_term))
        return jnp.concatenate(probs[0], axis=0), jnp.concatenate(probs[1], axis=0), inv

    def output(blk, kvh, p_a, p_b, inv, v_a, v_b):
        rows = slice(blk * WINDOW, (blk + 1) * WINDOW)
        c0 = kvh * kv_group_cols
        o = (jnp.dot(p_a, v_a, preferred_element_type=F32)
             + jnp.dot(p_b, v_b, preferred_element_type=F32))
        for p in range(pairs_per_kv):
            cols = slice(c0 + p * LANES, c0 + (p + 1) * LANES)
            gt = gate_ref[rows, cols].astype(F32)
            og = o[p * WINDOW:(p + 1) * WINDOW, :] * inv[p] * _silu(gt)
            out_ref[rows, cols] = og.astype(out_ref.dtype)

    for unit in units:
        logits, v_a, v_b = scores(*unit)
        p_a, p_b, inv = softmax(*unit, logits)
        output(*unit, p_a, p_b, inv, v_a, v_b)


def _swa_prompt(sinks, q, kvpad, gate, bias, batch, seq):
    m = batch * seq
    nt = seq // ATT_ROWS
    blocks_per_tile = ATT_ROWS // WINDOW
    blocks_per_seq = seq // WINDOW
    tile = lambda b, j: (b * nt + j, 0)
    prev = lambda b, j: (b * blocks_per_seq + jnp.maximum(j * blocks_per_tile - 1, 0), 0)
    return pl.pallas_call(
        _swa_prompt_kernel,
        grid=(batch, nt),
        in_specs=[pl.BlockSpec(memory_space=pltpu.SMEM),
                  pl.BlockSpec((ATT_ROWS, ATT_WIDTH), tile),
                  pl.BlockSpec((ATT_ROWS, PADDED_KV_WIDTH), tile),
                  pl.BlockSpec((WINDOW, PADDED_KV_WIDTH), prev),
                  pl.BlockSpec((ATT_ROWS, ATT_WIDTH), tile),
                  pl.BlockSpec((2, ATT_HEADS, WINDOW, WINDOW), lambda b, j: (0, 0, 0, 0))],
        out_specs=pl.BlockSpec((ATT_ROWS, ATT_WIDTH), tile),
        out_shape=jax.ShapeDtypeStruct((m, ATT_WIDTH), BF16),
        compiler_params=_params(2),
        name="swa_prompt",
    )(sinks, q, kvpad, kvpad, gate, bias)


def _sample_head_order():
    order = []
    for pair in range(ATT_KV_HEADS // 2):
        base = pair * 2 * ATT_GROUP
        for kv in range(2):
            for half in range(2):
                order += [base + kv * ATT_GROUP + 2 * g + half for g in range(ATT_GROUP // 2)]
    return np.asarray(order, np.int32)


def _swa_sample_pieces(step, q_ref, kvt_ref, ckt_ref, cvt_ref, gate_ref, bias_ref, sink_ref,
                       o_ref, cko_ref, cvo_ref):
    half = ATT_HEAD_DIM
    quarter = ATT_GROUP // 2
    contract_last = (((1,), (1,)), ((), ()))
    newest = lax.broadcasted_iota(jnp.int32, (ATT_KV_WIDTH, WINDOW), 1) == WINDOW - 1
    seq_lane = lax.broadcasted_iota(jnp.int32, (2 * ATT_KV_WIDTH, LANES), 1)
    lo4 = lax.broadcasted_iota(jnp.int32, (quarter, LANES), 1) < half
    new_cache = {}

    def update(i):
        b = step * SAMPLE_ATT_BATCH + i
        new_col = jnp.sum(jnp.where(seq_lane == b, kvt_ref[...], 0.0), axis=1, keepdims=True)
        kb = jnp.where(newest, new_col[:ATT_KV_WIDTH], pltpu.roll(ckt_ref[i], WINDOW - 1, axis=1))
        vb = jnp.where(newest, new_col[ATT_KV_WIDTH:], pltpu.roll(cvt_ref[i], WINDOW - 1, axis=1))
        cko_ref[i] = kb
        cvo_ref[i] = vb
        new_cache[i] = (kb, vb)

    def attend(i, pair):
        b = step * SAMPLE_ATT_BATCH + i
        kb, vb = new_cache[i]
        groups = slice(pair * ATT_GROUP, (pair + 1) * ATT_GROUP)
        hs = slice(pair * 2 * ATT_GROUP, (pair + 1) * 2 * ATT_GROUP)
        feat = slice(pair * LANES, (pair + 1) * LANES)
        g8 = q_ref[b, groups, :]
        g8r = pltpu.roll(g8, half, axis=1)
        qbd = jnp.concatenate(
            [jnp.where(lo4, g8[:quarter], 0.0), jnp.where(lo4, g8r[:quarter], 0.0),
             jnp.where(lo4, 0.0, g8r[quarter:]), jnp.where(lo4, 0.0, g8[quarter:])],
            axis=0)
        l = jnp.dot(qbd, kb[feat, :], preferred_element_type=F32) + bias_ref[hs, :]
        s = sink_ref[hs, 0:1]
        m = jnp.maximum(jnp.max(l, axis=-1, keepdims=True), s)
        pe = jnp.exp2(l - m)
        den = jnp.sum(pe, axis=-1, keepdims=True) + jnp.exp2(s - m)
        o2 = lax.dot_general(pe, vb[feat, :], contract_last,
                             preferred_element_type=F32) / den
        o2r = pltpu.roll(o2, half, axis=1)
        og = jnp.concatenate(
            [jnp.where(lo4, o2[:quarter], o2r[quarter:2 * quarter]),
             jnp.where(lo4, o2r[2 * quarter:3 * quarter], o2[3 * quarter:])], axis=0)
        o_ref[b, groups, :] = og * _silu(gate_ref[b, groups, :])

    pieces = []
    for i in range(SAMPLE_ATT_BATCH):
        pieces.append(functools.partial(update, i))
        pieces += [functools.partial(attend, i, pair) for pair in range(ATT_KV_HEADS // 2)]
    return pieces


def _group_norm_gate(o, gate):
    mu = jnp.mean(o, axis=-1, keepdims=True)
    d = o - mu
    var = jnp.mean(d * d, axis=-1, keepdims=True)
    return d * lax.rsqrt(var + EPS) * _silu(gate)


def _ret_prompt_kernel(g_block_ref, q_ref, kt_ref, v_ref, gate_ref, x_ref, wout_ref, gf_ref,
                       y_ref, r_ref, og_ref):
    c = pl.program_id(1)

    @pl.when(c == 0)
    def _():
        r_ref[...] = jnp.zeros_like(r_ref)

    causal = (lax.broadcasted_iota(jnp.int32, (RET_BLOCK, RET_BLOCK), 1)
              <= lax.broadcasted_iota(jnp.int32, (RET_BLOCK, RET_BLOCK), 0))
    n_blocks = RET_ROWS // RET_BLOCK
    for blk in range(n_blocks):
        rows = slice(blk * RET_BLOCK, (blk + 1) * RET_BLOCK)
        for h in range(RET_HEADS):
            qcols = slice(h * RET_QK_DIM, (h + 1) * RET_QK_DIM)
            vcols = slice(h * RET_V_DIM, (h + 1) * RET_V_DIM)
            qh = q_ref[rows, qcols]
            kt = kt_ref[qcols, rows]
            vh = v_ref[rows, vcols]
            state = r_ref[0, h]
            scores = jnp.where(causal, jnp.dot(qh, kt, preferred_element_type=F32), 0.0)
            o = (jnp.dot(scores.astype(BF16), vh, preferred_element_type=F32)
                 + jnp.dot(qh, state.astype(BF16), preferred_element_type=F32))
            r_ref[0, h] = g_block_ref[h] * (state + jnp.dot(kt, vh, preferred_element_type=F32))
            og_ref[h, rows, :] = _group_norm_gate(
                o, gate_ref[rows, vcols].astype(F32)).astype(og_ref.dtype)
            if blk == n_blocks - 1:
                part = jnp.dot(og_ref[h], wout_ref[vcols, :], preferred_element_type=F32)
                if h == 0:
                    y_ref[...] = x_ref[...] + part
                elif h < RET_HEADS - 1:
                    y_ref[...] += part
                else:
                    y_ref[...] = _rmsnorm(y_ref[...] + part, gf_ref[...])


def _ret_prompt(q, kt, v, gate, x, w_out, final_g, batch, seq):
    m = batch * seq
    nt = seq // RET_ROWS
    g_block = jnp.asarray(_decay_scales_np(RET_BLOCK, RET_BLOCK)[2])
    rows = lambda b, c: (b * nt + c, 0)
    return pl.pallas_call(
        _ret_prompt_kernel,
        grid=(batch, nt),
        in_specs=[pl.BlockSpec(memory_space=pltpu.SMEM),
                  pl.BlockSpec((RET_ROWS, RET_QK_WIDTH), rows),
                  pl.BlockSpec((RET_QK_WIDTH, RET_ROWS), lambda b, c: (0, b * nt + c)),
                  pl.BlockSpec((RET_ROWS, RET_WIDTH), rows),
                  pl.BlockSpec((RET_ROWS, RET_WIDTH), rows),
                  pl.BlockSpec((RET_ROWS, D_MODEL), rows),
                  _resident((RET_WIDTH, D_MODEL), lambda b, c: (0, 0)),
                  pl.BlockSpec((1, D_MODEL), lambda b, c: (0, 0))],
        out_specs=[pl.BlockSpec((RET_ROWS, D_MODEL), rows),
                   pl.BlockSpec((1, RET_HEADS, RET_QK_DIM, RET_V_DIM), lambda b, c: (b, 0, 0, 0))],
        out_shape=[jax.ShapeDtypeStruct((m, D_MODEL), F32),
                   jax.ShapeDtypeStruct((batch, RET_HEADS, RET_QK_DIM, RET_V_DIM), F32)],
        scratch_shapes=[pltpu.VMEM((RET_HEADS, RET_ROWS, RET_V_DIM), BF16)],
        compiler_params=_params(2),
        name="ret_prompt",
    )(g_block, q, kt, v, gate, x, w_out, final_g)


def kernel(x_prompt, x_sample, cache_swa_k, cache_swa_v, state_ret, norm_g, final_norm_g, rel_bias,
           w_in_attn, attn_sinks, w_out_attn, w_in_ret, w_out_ret):
    batch, seq, _ = x_prompt.shape
    n_s = x_sample.shape[0]
    xp = x_prompt.reshape(batch * seq, D_MODEL)
    xs = x_sample.reshape(n_s, D_MODEL)
    g0 = norm_g[0].reshape(1, D_MODEL)
    g1 = norm_g[1].reshape(1, D_MODEL)
    gf = final_norm_g.reshape(1, D_MODEL)
    w_in_a = w_in_attn[0].astype(BF16)
    w_out_a = w_out_attn[0].astype(BF16)
    w_in_r = w_in_ret[0].astype(BF16)
    w_out_r = w_out_ret[0].astype(BF16)
    sinks = attn_sinks[0] * LOG2E

    bias = _bias_table(rel_bias)

    qs, kvs, gates = _attn_inproj(xs, g0, w_in_a, F32)
    order = _sample_head_order()
    bias_s = bias[1, :, WINDOW - 1, :][order]
    sinks_rep = jnp.broadcast_to(sinks[order][:, None], (ATT_HEADS, LANES))
    to_feature_major = lambda c: jnp.transpose(c, (0, 2, 3, 1)).reshape(n_s, ATT_KV_WIDTH, WINDOW)
    to_window_major = lambda c: jnp.transpose(
        c.reshape(n_s, ATT_KV_HEADS, ATT_HEAD_DIM, WINDOW), (0, 3, 1, 2))[None]
    n_groups = ATT_WIDTH // LANES
    q, kv, gate, kvpad, ogs, ckt_new, cvt_new = _attn_inproj(
        xp, g0, w_in_a, BF16,
        sample=(qs.reshape(n_s, n_groups, LANES), kvs,
                to_feature_major(cache_swa_k[0]), to_feature_major(cache_swa_v[0]),
                gates.reshape(n_s, n_groups, LANES), bias_s, sinks_rep))
    og = _swa_prompt(sinks, q, kvpad, gate, bias, batch, seq)
    xp1 = _outproj(og, w_out_a, xp)
    kv_last = kv.reshape(batch, seq, 2 * ATT_KV_WIDTH)[:, seq - WINDOW:]
    k_prompt = kv_last[..., :ATT_KV_WIDTH].reshape(1, batch, WINDOW, ATT_KV_HEADS, ATT_HEAD_DIM)
    v_prompt = kv_last[..., ATT_KV_WIDTH:].reshape(1, batch, WINDOW, ATT_KV_HEADS, ATT_HEAD_DIM)
    xs1 = _outproj(ogs.reshape(n_s, ATT_WIDTH), w_out_a, xs)
    k_sample = to_window_major(ckt_new)
    v_sample = to_window_major(cvt_new)

    w_kt = _transposed_columns(w_in_ret[0], RET_QK_WIDTH, RET_QK_WIDTH, BF16)
    q_s, kt_s, v_s, gate_s = _ret_inproj(xs1, g1, w_in_r, w_kt, np.full((n_s,), PAST_LEN), 1, F32)
    q, kt, v, gate, og_s, r_sample = _ret_inproj(
        xp1, g1, w_in_r, w_kt, np.arange(seq), RET_BLOCK, BF16,
        sample=(q_s, kt_s, v_s, gate_s, state_ret[0]))
    y_prompt, r_prompt = _ret_prompt(q, kt, v, gate, xp1, w_out_r, gf, batch, seq)
    y_prompt = y_prompt.reshape(batch, seq, D_MODEL)
    y_sample = _outproj(og_s, w_out_r, xs1, gf).reshape(n_s, 1, D_MODEL)

    return (y_prompt, y_sample, k_prompt, v_prompt, r_prompt[None],
            k_sample, v_sample, r_sample[None])
```

```python
import functools
import math

import numpy as np
import jax
import jax.numpy as jnp
from jax import lax
from jax.experimental import pallas as pl
from jax.experimental.pallas import tpu as pltpu

F32 = jnp.float32
BF16 = jnp.bfloat16

D_MODEL = 1024
PAST_LEN = 16384
ATT_HEADS = 32
ATT_KV_HEADS = 4
ATT_GROUP = ATT_HEADS // ATT_KV_HEADS
ATT_HEAD_DIM = 64
ATT_WIDTH = ATT_HEADS * ATT_HEAD_DIM
ATT_KV_WIDTH = ATT_KV_HEADS * ATT_HEAD_DIM
ATT_IN = 2 * ATT_WIDTH + 2 * ATT_KV_WIDTH
WINDOW = 128
NUM_BUCKETS = 32
MAX_DISTANCE = 128
RET_HEADS = 4
RET_QK_DIM = 256
RET_V_DIM = 512
RET_QK_WIDTH = RET_HEADS * RET_QK_DIM
RET_WIDTH = RET_HEADS * RET_V_DIM
RET_IN = 2 * RET_QK_WIDTH + 2 * RET_WIDTH
ROPE_BASE = 10000.0
EPS = 1e-6
NEG = -1e30
LOG2E = math.log2(math.e)

LANES = 128
PADDED_KV_VARIANTS = 4
PADDED_KV_WIDTH = ATT_KV_HEADS * PADDED_KV_VARIANTS * LANES
PROJ_ROWS = 512
ATT_PROJ_ROWS = 1024
PROJ_COLS = 512
OUT_ROWS = 1024
ATT_ROWS = 1024
RET_BLOCK = 256
RET_ROWS = 1024
SAMPLE_ATT_BATCH = 4
SAMPLE_RET_BATCH = 2
VMEM_LIMIT = 60 * 1024 * 1024


def _params(n_axes):
    return pltpu.CompilerParams(
        dimension_semantics=("arbitrary",) * n_axes,
        vmem_limit_bytes=VMEM_LIMIT)


def _resident(shape, index_map):
    return pl.BlockSpec(shape, index_map, pipeline_mode=pl.Buffered(1))


def _silu(x):
    hx = 0.5 * x
    return hx + hx * jnp.tanh(hx)


def _rmsnorm(x, g):
    return x * lax.rsqrt(jnp.mean(x * x, axis=-1, keepdims=True) + EPS) * g


def _t5_bucket_np(rel):
    n = np.maximum(rel, 0)
    max_exact = NUM_BUCKETS // 2
    nf = np.maximum(n, 1).astype(np.float64)
    large = max_exact + (np.log(nf / max_exact) / math.log(MAX_DISTANCE / max_exact)
                         * (NUM_BUCKETS - max_exact)).astype(np.int32)
    large = np.minimum(large, NUM_BUCKETS - 1)
    return np.where(n < max_exact, n, large).astype(np.int32)


def _bucket_map_np():
    i = np.arange(WINDOW)[:, None]
    j = np.arange(WINDOW)[None, :]
    rel = np.where(j <= i, i - j, i - j + WINDOW)
    return _t5_bucket_np(rel).astype(np.int32)


def _gammas():
    return [1.0 - 2.0 ** (-5 - h) for h in range(RET_HEADS)]


def _decay_scales_np(rows, block):
    lg = np.log1p(-np.exp2(-5.0 - np.arange(RET_HEADS, dtype=np.float64)))
    i1 = (np.arange(rows) % block + 1.0)[None, :] * lg[:, None]
    q_scale = np.repeat(np.exp(i1)[:, :, None], LANES, axis=2)
    k_scale = np.repeat((np.exp(-i1) * RET_QK_DIM ** -0.5)[:, :, None], LANES, axis=2)
    return (q_scale.astype(np.float32), k_scale.astype(np.float32),
            np.exp(block * lg).astype(np.float32))


def _rope_tables_np(pos):
    half = RET_QK_DIM // 2
    inv = ROPE_BASE ** (-np.arange(half, dtype=np.float64) / half)
    ang = np.asarray(pos, np.float64)[:, None] * inv[None, :]
    return np.cos(ang).astype(np.float32), np.sin(ang).astype(np.float32)


def _bias_table_kernel(rb_ref, bucket_ref, out_ref):
    bk = bucket_ref[...]
    previous_block = (lax.broadcasted_iota(jnp.int32, bk.shape, 1)
                      > lax.broadcasted_iota(jnp.int32, bk.shape, 0))

    def one_head(h, carry):
        acc = jnp.zeros(bk.shape, F32)
        for b in range(NUM_BUCKETS):
            acc = jnp.where(bk == b, rb_ref[b, h] * LOG2E, acc)
        out_ref[1, h] = acc
        out_ref[0, h] = jnp.where(previous_block, NEG, acc)
        return carry

    lax.fori_loop(0, ATT_HEADS, one_head, 0)


def _bias_table(rel_bias):
    bucket = jnp.asarray(_bucket_map_np())
    return pl.pallas_call(
        _bias_table_kernel,
        grid=(1,),
        in_specs=[pl.BlockSpec(memory_space=pltpu.SMEM),
                  pl.BlockSpec((WINDOW, WINDOW), lambda i: (0, 0))],
        out_specs=pl.BlockSpec((2, ATT_HEADS, WINDOW, WINDOW), lambda i: (0, 0, 0, 0)),
        out_shape=jax.ShapeDtypeStruct((2, ATT_HEADS, WINDOW, WINDOW), F32),
        compiler_params=_params(1),
        name="bias_table",
    )(rel_bias, bucket)


def _pad_pair(x, x_rolled, lo, e):
    if e == 0:
        return jnp.where(lo, x, 0.0), jnp.where(lo, 0.0, x_rolled)
    return jnp.where(lo, x_rolled, 0.0), jnp.where(lo, 0.0, x)


def _attn_inproj_kernel(*refs, hosts_sample):
    x_ref, g_ref, w_ref = refs[:3]
    side_work = []
    kvpad_ref = None
    if hosts_sample:
        qs_ref, kvs_ref, ckt_ref, cvt_ref, gates_ref, bias_ref, sink_ref = refs[3:10]
        q_ref, kv_ref, gate_ref, kvpad_ref, ogs_ref, cko_ref, cvo_ref, kvt_ref = refs[10:]
        step = pl.program_id(0)

        @pl.when(step == 0)
        def _():
            kvt_ref[...] = kvs_ref[...].T

        side_work = _swa_sample_pieces(step, qs_ref, kvt_ref, ckt_ref, cvt_ref, gates_ref,
                                       bias_ref, sink_ref, ogs_ref, cko_ref, cvo_ref)
    else:
        q_ref, kv_ref, gate_ref = refs[3:]

    h = _rmsnorm(x_ref[...], g_ref[...]).astype(BF16)
    for c in range(0, ATT_IN, PROJ_COLS):
        if side_work:
            side_work.pop(0)()
        r = jnp.dot(h, w_ref[:, c:c + PROJ_COLS], preferred_element_type=F32)
        if c < ATT_WIDTH:
            q_ref[:, c:c + PROJ_COLS] = (r * (ATT_HEAD_DIM ** -0.5 * LOG2E)).astype(q_ref.dtype)
        elif c < ATT_WIDTH + 2 * ATT_KV_WIDTH:
            kv_ref[...] = r
            if kvpad_ref is not None:
                lo = lax.broadcasted_iota(jnp.int32, (r.shape[0], LANES), 1) < ATT_HEAD_DIM
                for pair in range(ATT_KV_HEADS // 2):
                    for is_v in range(2):
                        c1 = is_v * ATT_KV_WIDTH + pair * LANES
                        x2 = r[:, c1:c1 + LANES]
                        x2r = pltpu.roll(x2, ATT_HEAD_DIM, axis=1)
                        for e in range(2):
                            g0 = ((2 * pair + e) * PADDED_KV_VARIANTS + 2 * is_v) * LANES
                            lo_half, hi_half = _pad_pair(x2, x2r, lo, e)
                            kvpad_ref[:, g0:g0 + LANES] = lo_half.astype(kvpad_ref.dtype)
                            kvpad_ref[:, g0 + LANES:g0 + 2 * LANES] = hi_half.astype(kvpad_ref.dtype)
        else:
            o = c - ATT_WIDTH - 2 * ATT_KV_WIDTH
            gate_ref[:, o:o + PROJ_COLS] = r.astype(gate_ref.dtype)
    while side_work:
        side_work.pop(0)()


def _attn_inproj(x, g, w, out_dtype, sample=None):
    m = x.shape[0]
    tm = min(ATT_PROJ_ROWS, m)
    row = lambda i: (i, 0)
    const2 = lambda i: (0, 0)
    const3 = lambda i: (0, 0, 0)
    in_specs = [pl.BlockSpec((tm, D_MODEL), row),
                pl.BlockSpec((1, D_MODEL), const2),
                _resident((D_MODEL, ATT_IN), const2)]
    out_specs = [pl.BlockSpec((tm, ATT_WIDTH), row),
                 pl.BlockSpec((tm, 2 * ATT_KV_WIDTH), row),
                 pl.BlockSpec((tm, ATT_WIDTH), row)]
    out_shape = [jax.ShapeDtypeStruct((m, ATT_WIDTH), out_dtype),
                 jax.ShapeDtypeStruct((m, 2 * ATT_KV_WIDTH), F32),
                 jax.ShapeDtypeStruct((m, ATT_WIDTH), out_dtype)]
    args = [x, g, w]
    scratch_shapes = []
    if sample is not None:
        q_s, kv_s, ckt, cvt, gate_s, bias_s, sinks_rep = sample
        n, n_groups, _ = q_s.shape
        assert n == SAMPLE_ATT_BATCH * (m // tm), (n, m, tm)
        cache_spec = pl.BlockSpec((SAMPLE_ATT_BATCH, ATT_KV_WIDTH, WINDOW), lambda i: (i, 0, 0))
        in_specs += [_resident((n, n_groups, LANES), const3),
                     _resident((n, 2 * ATT_KV_WIDTH), const2),
                     cache_spec, cache_spec,
                     _resident((n, n_groups, LANES), const3),
                     _resident((ATT_HEADS, WINDOW), const2),
                     _resident((ATT_HEADS, LANES), const2)]
        out_specs += [pl.BlockSpec((tm, PADDED_KV_WIDTH), row),
                      pl.BlockSpec((n, n_groups, LANES), const3), cache_spec, cache_spec]
        out_shape += [jax.ShapeDtypeStruct((m, PADDED_KV_WIDTH), BF16),
                      jax.ShapeDtypeStruct((n, n_groups, LANES), F32),
                      jax.ShapeDtypeStruct((n, ATT_KV_WIDTH, WINDOW), F32),
                      jax.ShapeDtypeStruct((n, ATT_KV_WIDTH, WINDOW), F32)]
        args += list(sample)
        scratch_shapes = [pltpu.VMEM((2 * ATT_KV_WIDTH, n), F32)]
    return pl.pallas_call(
        functools.partial(_attn_inproj_kernel, hosts_sample=sample is not None),
        grid=(m // tm,),
        in_specs=in_specs,
        out_specs=out_specs,
        out_shape=out_shape,
        scratch_shapes=scratch_shapes,
        compiler_params=_params(1),
        name="attn_inproj",
    )(*args)


def _ret_sample_pieces(step, qt_ref, kt_ref, v_ref, gate_ref, st_ref, og_ref, sto_ref):
    lane = lax.broadcasted_iota(jnp.int32, (RET_QK_DIM, LANES), 1)
    gammas = _gammas()

    def piece(i, h):
        b = step * SAMPLE_RET_BATCH + i
        pick = lane == b
        rows = slice(h * RET_QK_DIM, (h + 1) * RET_QK_DIM)
        qc = jnp.sum(jnp.where(pick, qt_ref[rows, :], 0.0), axis=1, keepdims=True)
        kc = jnp.sum(jnp.where(pick, kt_ref[rows, :], 0.0), axis=1, keepdims=True)
        vcols = slice(h * RET_V_DIM, (h + 1) * RET_V_DIM)
        vrow = v_ref[pl.ds(b, 1), vcols]
        state = st_ref[i, h]
        o = (jnp.sum(qc * state, axis=0, keepdims=True)
             + jnp.sum(qc * kc, axis=0, keepdims=True) * vrow)
        sto_ref[i, h] = gammas[h] * (state + kc * vrow)
        og_ref[pl.ds(b, 1), vcols] = _group_norm_gate(o, gate_ref[pl.ds(b, 1), vcols])

    return [functools.partial(piece, i, h)
            for i in range(SAMPLE_RET_BATCH) for h in range(RET_HEADS)]


def _ret_inproj_kernel(*refs, with_sample):
    x_ref, g_ref, w_ref, cos_ref, sin_ref, qsc_ref, ksc_ref = refs[:7]
    side_work = []
    if with_sample:
        qs_ref, ks_ref, vs_ref, gates_ref, st_ref = refs[7:12]
        q_ref, k_ref, v_ref, gate_ref, ogs_ref, sto_ref, qts_ref, kts_ref = refs[12:]
        step = pl.program_id(0)

        @pl.when(step == 0)
        def _():
            qts_ref[...] = qs_ref[...].T
            kts_ref[...] = ks_ref[...].T

        side_work = _ret_sample_pieces(step, qts_ref, kts_ref, vs_ref, gates_ref, st_ref,
                                       ogs_ref, sto_ref)
    else:
        q_ref, k_ref, v_ref, gate_ref = refs[7:]

    def interleave():
        if side_work:
            side_work.pop(0)()

    h = _rmsnorm(x_ref[...], g_ref[...]).astype(BF16)
    half = RET_QK_DIM // 2
    cos = cos_ref[...]
    sin = sin_ref[...]
    for out_ref, scale_ref, w0 in ((q_ref, qsc_ref, 0), (k_ref, ksc_ref, RET_QK_WIDTH)):
        for c in range(0, RET_QK_WIDTH, PROJ_COLS):
            interleave()
            r = jnp.dot(h, w_ref[:, w0 + c:w0 + c + PROJ_COLS], preferred_element_type=F32)
            for o in range(0, PROJ_COLS, RET_QK_DIM):
                scale = scale_ref[(c + o) // RET_QK_DIM]
                x1 = r[:, o:o + half]
                x2 = r[:, o + half:o + RET_QK_DIM]
                out_ref[:, c + o:c + o + half] = (
                    (x1 * cos - x2 * sin) * scale).astype(out_ref.dtype)
                out_ref[:, c + o + half:c + o + RET_QK_DIM] = (
                    (x1 * sin + x2 * cos) * scale).astype(out_ref.dtype)
    for c in range(0, RET_WIDTH, PROJ_COLS):
        interleave()
        v_ref[:, c:c + PROJ_COLS] = jnp.dot(
            h, w_ref[:, 2 * RET_QK_WIDTH + c:2 * RET_QK_WIDTH + c + PROJ_COLS],
            preferred_element_type=F32).astype(v_ref.dtype)
    for c in range(0, RET_WIDTH, PROJ_COLS):
        interleave()
        w0 = 2 * RET_QK_WIDTH + RET_WIDTH + c
        gate_ref[:, c:c + PROJ_COLS] = jnp.dot(
            h, w_ref[:, w0:w0 + PROJ_COLS], preferred_element_type=F32).astype(gate_ref.dtype)
    while side_work:
        interleave()


def _ret_inproj(x, g, w, pos, block, out_dtype, sample=None):
    m = x.shape[0]
    tm = min(PROJ_ROWS, m)
    n_pos = len(pos) // tm
    half = RET_QK_DIM // 2
    cos, sin = _rope_tables_np(pos)
    q_scale, k_scale, _ = _decay_scales_np(tm, block)
    row = lambda i: (i, 0)
    const2 = lambda i: (0, 0)
    const3 = lambda i: (0, 0, 0)
    in_specs = [pl.BlockSpec((tm, D_MODEL), row),
                pl.BlockSpec((1, D_MODEL), const2),
                _resident((D_MODEL, RET_IN), const2),
                pl.BlockSpec((tm, half), lambda i: (i % n_pos, 0)),
                pl.BlockSpec((tm, half), lambda i: (i % n_pos, 0)),
                _resident((RET_HEADS, tm, LANES), const3),
                _resident((RET_HEADS, tm, LANES), const3)]
    out_specs = [pl.BlockSpec((tm, RET_QK_WIDTH), row),
                 pl.BlockSpec((tm, RET_QK_WIDTH), row),
                 pl.BlockSpec((tm, RET_WIDTH), row),
                 pl.BlockSpec((tm, RET_WIDTH), row)]
    out_shape = [jax.ShapeDtypeStruct((m, RET_QK_WIDTH), out_dtype),
                 jax.ShapeDtypeStruct((m, RET_QK_WIDTH), out_dtype),
                 jax.ShapeDtypeStruct((m, RET_WIDTH), out_dtype),
                 jax.ShapeDtypeStruct((m, RET_WIDTH), out_dtype)]
    args = [x, g, w, jnp.asarray(cos), jnp.asarray(sin), jnp.asarray(q_scale), jnp.asarray(k_scale)]
    scratch_shapes = []
    if sample is not None:
        q_s, k_s, v_s, gate_s, state = sample
        n = q_s.shape[0]
        assert n == SAMPLE_RET_BATCH * (m // tm), (n, m, tm)
        st_spec = pl.BlockSpec((SAMPLE_RET_BATCH, RET_HEADS, RET_QK_DIM, RET_V_DIM),
                               lambda i: (i, 0, 0, 0))
        in_specs += [_resident((n, RET_QK_WIDTH), const2), _resident((n, RET_QK_WIDTH), const2),
                     _resident((n, RET_WIDTH), const2), _resident((n, RET_WIDTH), const2), st_spec]
        out_specs += [pl.BlockSpec((n, RET_WIDTH), const2), st_spec]
        out_shape += [jax.ShapeDtypeStruct((n, RET_WIDTH), F32),
                      jax.ShapeDtypeStruct(state.shape, F32)]
        args += [q_s, k_s, v_s, gate_s, state]
        scratch_shapes = [pltpu.VMEM((RET_QK_WIDTH, n), F32), pltpu.VMEM((RET_QK_WIDTH, n), F32)]
    return pl.pallas_call(
        functools.partial(_ret_inproj_kernel, with_sample=sample is not None),
        grid=(m // tm,),
        in_specs=in_specs,
        out_specs=out_specs,
        out_shape=out_shape,
        scratch_shapes=scratch_shapes,
        compiler_params=_params(1),
        name="ret_inproj",
    )(*args)


def _outproj_kernel(o_ref, w_ref, x_ref, y_ref):
    y_ref[...] = x_ref[...] + jnp.dot(o_ref[...].astype(BF16), w_ref[...],
                                      preferred_element_type=F32)


def _outproj_norm_kernel(o_ref, w_ref, x_ref, g_ref, y_ref):
    y = x_ref[...] + jnp.dot(o_ref[...].astype(BF16), w_ref[...], preferred_element_type=F32)
    y_ref[...] = _rmsnorm(y, g_ref[...])


def _outproj(o, w, x, final_g=None):
    m, width = o.shape
    tm = min(OUT_ROWS, m)
    row = lambda i: (i, 0)
    in_specs = [pl.BlockSpec((tm, width), row),
                _resident((width, D_MODEL), lambda i: (0, 0)),
                pl.BlockSpec((tm, D_MODEL), row)]
    args = [o, w, x]
    body = _outproj_kernel
    if final_g is not None:
        in_specs.append(pl.BlockSpec((1, D_MODEL), lambda i: (0, 0)))
        args.append(final_g)
        body = _outproj_norm_kernel
    return pl.pallas_call(
        body,
        grid=(m // tm,),
        in_specs=in_specs,
        out_specs=pl.BlockSpec((tm, D_MODEL), row),
        out_shape=jax.ShapeDtypeStruct((m, D_MODEL), F32),
        compiler_params=_params(1),
        name="outproj",
    )(*args)


def _swa_prompt_kernel(sink_ref, q_ref, kvc_ref, kvp_ref, gate_ref, bias_ref, out_ref):
    j = pl.program_id(1)
    half = ATT_HEAD_DIM
    pairs_per_kv = ATT_GROUP // 2
    kv_group_cols = ATT_GROUP * ATT_HEAD_DIM
    contract_last = (((1,), (1,)), ((), ()))
    lo1 = lax.broadcasted_iota(jnp.int32, (WINDOW, LANES), 1) < half
    own = (lax.broadcasted_iota(jnp.int32, (WINDOW, WINDOW), 1)
           <= lax.broadcasted_iota(jnp.int32, (WINDOW, WINDOW), 0))
    units =[(blk, kvh) for blk in range(ATT_ROWS // WINDOW) for kvh in range(ATT_KV_HEADS)]

    def keys_values(blk, kvh):
        r0 = blk * WINDOW
        out = []
        for variant in range(PADDED_KV_VARIANTS):
            c = (kvh * PADDED_KV_VARIANTS + variant) * LANES
            prev = kvp_ref[:, c:c + LANES] if blk == 0 else kvc_ref[r0 - WINDOW:r0, c:c + LANES]
            out.append(jnp.concatenate([prev, kvc_ref[r0:r0 + WINDOW, c:c + LANES]], axis=0))
        return out

    def scores(blk, kvh):
        rows = slice(blk * WINDOW, (blk + 1) * WINDOW)
        c0 = kvh * kv_group_cols
        k_a, k_b, v_a, v_b = keys_values(blk, kvh)
        qs = jnp.concatenate(
            [q_ref[rows, c0 + p * LANES:c0 + (p + 1) * LANES] for p in range(pairs_per_kv)],
            axis=0)
        logits = (lax.dot_general(qs, k_a, contract_last, preferred_element_type=F32),
                  lax.dot_general(qs, k_b, contract_last, preferred_element_type=F32))
        return logits, v_a, v_b

    def softmax(blk, kvh, logits):
        sel = jnp.where(j == 0, 0, 1) if blk == 0 else 1
        probs = ([], [])
        inv = []
        for p in range(pairs_per_kv):
            top, total, sink = [], [], []
            for st in range(2):
                hd = kvh * ATT_GROUP + 2 * p + st
                lg = logits[st][p * WINDOW:(p + 1) * WINDOW, :]
                l = jnp.where(own, lg[:, WINDOW:], lg[:, :WINDOW]) + bias_ref[sel, hd]
                s = sink_ref[hd]
                m = jnp.maximum(jnp.max(l, axis=-1, keepdims=True), s)
                pe = jnp.exp2(l - m)
                probs[st].append(jnp.concatenate(
                    [jnp.where(own, 0.0, pe).astype(BF16),
                     jnp.where(own, pe, 0.0).astype(BF16)], axis=1))
                top.append(m)
                total.append(jnp.sum(pe, axis=-1, keepdims=True))
                sink.append(s)
            sink_term = jnp.exp2(jnp.where(lo1, sink[0], sink[1]) - jnp.where(lo1, top[0], top[1]))
            inv.append(1.0 / (jnp.where(lo1, total[0], total[1]) + sink_term))
        return jnp.concatenate(probs[0], axis=0), jnp.concatenate(probs[1], axis=0), inv

    def output(blk, kvh, p_a, p_b, inv, v_a, v_b):
        rows = slice(blk * WINDOW, (blk + 1) * WINDOW)
        c0 = kvh * kv_group_cols
        o = (jnp.dot(p_a, v_a, preferred_element_type=F32)
             + jnp.dot(p_b, v_b, preferred_element_type=F32))
        for p in range(pairs_per_kv):
            cols = slice(c0 + p * LANES, c0 + (p + 1) * LANES)
            gt = gate_ref[rows, cols].astype(F32)
            og = o[p * WINDOW:(p + 1) * WINDOW, :] * inv[p] * _silu(gt)
            out_ref[rows, cols] = og.astype(out_ref.dtype)

    for unit in units:
        logits, v_a, v_b = scores(*unit)
        p_a, p_b, inv = softmax(*unit, logits)
        output(*unit, p_a, p_b, inv, v_a, v_b)


def _swa_prompt(sinks, q, kvpad, gate, bias, batch, seq):
    m = batch * seq
    nt = seq // ATT_ROWS
    blocks_per_tile = ATT_ROWS // WINDOW
    blocks_per_seq = seq // WINDOW
    tile = lambda b, j: (b * nt + j, 0)
    prev = lambda b, j: (b * blocks_per_seq + jnp.maximum(j * blocks_per_tile - 1, 0), 0)
    return pl.pallas_call(
        _swa_prompt_kernel,
        grid=(batch, nt),
        in_specs=[pl.BlockSpec(memory_space=pltpu.SMEM),
                  pl.BlockSpec((ATT_ROWS, ATT_WIDTH), tile),
                  pl.BlockSpec((ATT_ROWS, PADDED_KV_WIDTH), tile),
                  pl.BlockSpec((WINDOW, PADDED_KV_WIDTH), prev),
                  pl.BlockSpec((ATT_ROWS, ATT_WIDTH), tile),
                  pl.BlockSpec((2, ATT_HEADS, WINDOW, WINDOW), lambda b, j: (0, 0, 0, 0))],
        out_specs=pl.BlockSpec((ATT_ROWS, ATT_WIDTH), tile),
        out_shape=jax.ShapeDtypeStruct((m, ATT_WIDTH), BF16),
        compiler_params=_params(2),
        name="swa_prompt",
    )(sinks, q, kvpad, kvpad, gate, bias)


def _sample_head_order():
    order = []
    for pair in range(ATT_KV_HEADS // 2):
        base = pair * 2 * ATT_GROUP
        for kv in range(2):
            for half in range(2):
                order += [base + kv * ATT_GROUP + 2 * g + half for g in range(ATT_GROUP // 2)]
    return np.asarray(order, np.int32)


def _swa_sample_pieces(step, q_ref, kvt_ref, ckt_ref, cvt_ref, gate_ref, bias_ref, sink_ref,
                       o_ref, cko_ref, cvo_ref):
    half = ATT_HEAD_DIM
    quarter = ATT_GROUP // 2
    contract_last = (((1,), (1,)), ((), ()))
    newest = lax.broadcasted_iota(jnp.int32, (ATT_KV_WIDTH, WINDOW), 1) == WINDOW - 1
    seq_lane = lax.broadcasted_iota(jnp.int32, (2 * ATT_KV_WIDTH, LANES), 1)
    lo4 = lax.broadcasted_iota(jnp.int32, (quarter, LANES), 1) < half
    new_cache = {}

    def update(i):
        b = step * SAMPLE_ATT_BATCH + i
        new_col = jnp.sum(jnp.where(seq_lane == b, kvt_ref[...], 0.0), axis=1, keepdims=True)
        kb = jnp.where(newest, new_col[:ATT_KV_WIDTH], pltpu.roll(ckt_ref[i], WINDOW - 1, axis=1))
        vb = jnp.where(newest, new_col[ATT_KV_WIDTH:], pltpu.roll(cvt_ref[i], WINDOW - 1, axis=1))
        cko_ref[i] = kb
        cvo_ref[i] = vb
        new_cache[i] = (kb, vb)

    def attend(i, pair):
        b = step * SAMPLE_ATT_BATCH + i
        kb, vb = new_cache[i]
        groups = slice(pair * ATT_GROUP, (pair + 1) * ATT_GROUP)
        hs = slice(pair * 2 * ATT_GROUP, (pair + 1) * 2 * ATT_GROUP)
        feat = slice(pair * LANES, (pair + 1) * LANES)
        g8 = q_ref[b, groups, :]
        g8r = pltpu.roll(g8, half, axis=1)
        qbd = jnp.concatenate(
            [jnp.where(lo4, g8[:quarter], 0.0), jnp.where(lo4, g8r[:quarter], 0.0),
             jnp.where(lo4, 0.0, g8r[quarter:]), jnp.where(lo4, 0.0, g8[quarter:])],
            axis=0)
        l = jnp.dot(qbd, kb[feat, :], preferred_element_type=F32) + bias_ref[hs, :]
        s = sink_ref[hs, 0:1]
        m = jnp.maximum(jnp.max(l, axis=-1, keepdims=True), s)
        pe = jnp.exp2(l - m)
        den = jnp.sum(pe, axis=-1, keepdims=True) + jnp.exp2(s - m)
        o2 = lax.dot_general(pe, vb[feat, :], contract_last,
                             preferred_element_type=F32) / den
        o2r = pltpu.roll(o2, half, axis=1)
        og = jnp.concatenate(
            [jnp.where(lo4, o2[:quarter], o2r[quarter:2 * quarter]),
             jnp.where(lo4, o2r[2 * quarter:3 * quarter], o2[3 * quarter:])], axis=0)
        o_ref[b, groups, :] = og * _silu(gate_ref[b, groups, :])

    pieces = []
    for i in range(SAMPLE_ATT_BATCH):
        pieces.append(functools.partial(update, i))
        pieces += [functools.partial(attend, i, pair) for pair in range(ATT_KV_HEADS // 2)]
    return pieces


def _group_norm_gate(o, gate):
    mu = jnp.mean(o, axis=-1, keepdims=True)
    d = o - mu
    var = jnp.mean(d * d, axis=-1, keepdims=True)
    return d * lax.rsqrt(var + EPS) * _silu(gate)


def _ret_prompt_kernel(g_block_ref, q_ref, k_ref, v_ref, gate_ref, x_ref, wout_ref, gf_ref,
                       y_ref, r_ref, og_ref):
    c = pl.program_id(1)

    @pl.when(c == 0)
    def _():
        r_ref[...] = jnp.zeros_like(r_ref)

    causal = (lax.broadcasted_iota(jnp.int32, (RET_BLOCK, RET_BLOCK), 1)
              <= lax.broadcasted_iota(jnp.int32, (RET_BLOCK, RET_BLOCK), 0))
    contract_last = (((1,), (1,)), ((), ()))
    n_blocks = RET_ROWS // RET_BLOCK
    for blk in range(n_blocks):
        rows = slice(blk * RET_BLOCK, (blk + 1) * RET_BLOCK)
        for h in range(RET_HEADS):
            qcols = slice(h * RET_QK_DIM, (h + 1) * RET_QK_DIM)
            vcols = slice(h * RET_V_DIM, (h + 1) * RET_V_DIM)
            qh = q_ref[rows, qcols]
            kh = k_ref[rows, qcols]
            vh = v_ref[rows, vcols]
            state = r_ref[0, h]
            scores = jnp.where(
                causal, lax.dot_general(qh, kh, contract_last, preferred_element_type=F32), 0.0)
            o = (jnp.dot(scores.astype(BF16), vh, preferred_element_type=F32)
                 + jnp.dot(qh, state.astype(BF16), preferred_element_type=F32))
            r_ref[0, h] = g_block_ref[h] * (
                state + jnp.dot(kh.T, vh, preferred_element_type=F32))
            og_ref[h, rows, :] = _group_norm_gate(
                o, gate_ref[rows, vcols].astype(F32)).astype(og_ref.dtype)
            if blk == n_blocks - 1:
                part = jnp.dot(og_ref[h], wout_ref[vcols, :], preferred_element_type=F32)
                if h == 0:
                    y_ref[...] = x_ref[...] + part
                elif h < RET_HEADS - 1:
                    y_ref[...] += part
                else:
                    y_ref[...] = _rmsnorm(y_ref[...] + part, gf_ref[...])


def _ret_prompt(q, k, v, gate, x, w_out, final_g, batch, seq):
    m = batch * seq
    nt = seq // RET_ROWS
    g_block = jnp.asarray(_decay_scales_np(RET_BLOCK, RET_BLOCK)[2])
    rows = lambda b, c: (b * nt + c, 0)
    return pl.pallas_call(
        _ret_prompt_kernel,
        grid=(batch, nt),
        in_specs=[pl.BlockSpec(memory_space=pltpu.SMEM),
                  pl.BlockSpec((RET_ROWS, RET_QK_WIDTH), rows),
                  pl.BlockSpec((RET_ROWS, RET_QK_WIDTH), rows),
                  pl.BlockSpec((RET_ROWS, RET_WIDTH), rows),
                  pl.BlockSpec((RET_ROWS, RET_WIDTH), rows),
                  pl.BlockSpec((RET_ROWS, D_MODEL), rows),
                  _resident((RET_WIDTH, D_MODEL), lambda b, c: (0, 0)),
                  pl.BlockSpec((1, D_MODEL), lambda b, c: (0, 0))],
        out_specs=[pl.BlockSpec((RET_ROWS, D_MODEL), rows),
                   pl.BlockSpec((1, RET_HEADS, RET_QK_DIM, RET_V_DIM), lambda b, c: (b, 0, 0, 0))],
        out_shape=[jax.ShapeDtypeStruct((m, D_MODEL), F32),
                   jax.ShapeDtypeStruct((batch, RET_HEADS, RET_QK_DIM, RET_V_DIM), F32)],
        scratch_shapes=[pltpu.VMEM((RET_HEADS, RET_ROWS, RET_V_DIM), BF16)],
        compiler_params=_params(2),
        name="ret_prompt",
    )(g_block, q, k, v, gate, x, w_out, final_g)


def kernel(x_prompt, x_sample, cache_swa_k, cache_swa_v, state_ret, norm_g, final_norm_g, rel_bias,
           w_in_attn, attn_sinks, w_out_attn, w_in_ret, w_out_ret):
    batch, seq, _ = x_prompt.shape
    n_s = x_sample.shape[0]
    xp = x_prompt.reshape(batch * seq, D_MODEL)
    xs = x_sample.reshape(n_s, D_MODEL)
    g0 = norm_g[0].reshape(1, D_MODEL)
    g1 = norm_g[1].reshape(1, D_MODEL)
    gf = final_norm_g.reshape(1, D_MODEL)
    w_in_a = w_in_attn[0].astype(BF16)
    w_out_a = w_out_attn[0].astype(BF16)
    w_in_r = w_in_ret[0].astype(BF16)
    w_out_r = w_out_ret[0].astype(BF16)
    sinks = attn_sinks[0] * LOG2E

    bias = _bias_table(rel_bias)

    qs, kvs, gates = _attn_inproj(xs, g0, w_in_a, F32)
    order = _sample_head_order()
    bias_s = bias[1, :, WINDOW - 1, :][order]
    sinks_rep = jnp.broadcast_to(sinks[order][:, None], (ATT_HEADS, LANES))
    to_feature_major = lambda c: jnp.transpose(c, (0, 2, 3, 1)).reshape(n_s, ATT_KV_WIDTH, WINDOW)
    to_window_major = lambda c: jnp.transpose(
        c.reshape(n_s, ATT_KV_HEADS, ATT_HEAD_DIM, WINDOW), (0, 3, 1, 2))[None]
    n_groups = ATT_WIDTH // LANES
    q, kv, gate, kvpad, ogs, ckt_new, cvt_new = _attn_inproj(
        xp, g0, w_in_a, BF16,
        sample=(qs.reshape(n_s, n_groups, LANES), kvs,
                to_feature_major(cache_swa_k[0]), to_feature_major(cache_swa_v[0]),
                gates.reshape(n_s, n_groups, LANES), bias_s, sinks_rep))
    og = _swa_prompt(sinks, q, kvpad, gate, bias, batch, seq)
    xp1 = _outproj(og, w_out_a, xp)
    kv_last = kv.reshape(batch, seq, 2 * ATT_KV_WIDTH)[:, seq - WINDOW:]
    k_prompt = kv_last[..., :ATT_KV_WIDTH].reshape(1, batch, WINDOW, ATT_KV_HEADS, ATT_HEAD_DIM)
    v_prompt = kv_last[..., ATT_KV_WIDTH:].reshape(1, batch, WINDOW, ATT_KV_HEADS, ATT_HEAD_DIM)
    xs1 = _outproj(ogs.reshape(n_s, ATT_WIDTH), w_out_a, xs)
    k_sample = to_window_major(ckt_new)
    v_sample = to_window_major(cvt_new)

    q_s, k_s, v_s, gate_s = _ret_inproj(xs1, g1, w_in_r, np.full((n_s,), PAST_LEN), 1, F32)
    q, k, v, gate, og_s, r_sample = _ret_inproj(
        xp1, g1, w_in_r, np.arange(seq), RET_BLOCK, BF16,
        sample=(q_s, k_s, v_s, gate_s, state_ret[0]))
    y_prompt, r_prompt = _ret_prompt(q, k, v, gate, xp1, w_out_r, gf, batch, seq)
    y_prompt = y_prompt.reshape(batch, seq, D_MODEL)
    y_sample = _outproj(og_s, w_out_r, xs1, gf).reshape(n_s, 1, D_MODEL)

    return (y_prompt, y_sample, k_prompt, v_prompt, r_prompt[None],
            k_sample, v_sample, r_sample[None])
```

```python
import functools
import math

import numpy as np
import jax
import jax.numpy as jnp
from jax import lax
from jax.experimental import pallas as pl
from jax.experimental.pallas import tpu as pltpu

F32 = jnp.float32
BF16 = jnp.bfloat16

D_MODEL = 1024
PAST_LEN = 16384
ATT_HEADS = 32
ATT_KV_HEADS = 4
ATT_GROUP = ATT_HEADS // ATT_KV_HEADS
ATT_HEAD_DIM = 64
ATT_WIDTH = ATT_HEADS * ATT_HEAD_DIM
ATT_KV_WIDTH = ATT_KV_HEADS * ATT_HEAD_DIM
ATT_IN = 2 * ATT_WIDTH + 2 * ATT_KV_WIDTH
WINDOW = 128
NUM_BUCKETS = 32
MAX_DISTANCE = 128
RET_HEADS = 4
RET_QK_DIM = 256
RET_V_DIM = 512
RET_QK_WIDTH = RET_HEADS * RET_QK_DIM
RET_WIDTH = RET_HEADS * RET_V_DIM
RET_IN = 2 * RET_QK_WIDTH + 2 * RET_WIDTH
ROPE_BASE = 10000.0
EPS = 1e-6
NEG = -1e30
LOG2E = math.log2(math.e)

LANES = 128
PADDED_KV_VARIANTS = 4
PADDED_KV_WIDTH = ATT_KV_HEADS * PADDED_KV_VARIANTS * LANES
PROJ_ROWS = 512
ATT_PROJ_ROWS = 1024
PROJ_COLS = 512
OUT_ROWS = 1024
ATT_ROWS = 1024
RET_BLOCK = 256
RET_ROWS = 1024
SAMPLE_ATT_BATCH = 4
SAMPLE_RET_BATCH = 2
VMEM_LIMIT = 60 * 1024 * 1024


def _params(n_axes):
    return pltpu.CompilerParams(
        dimension_semantics=("arbitrary",) * n_axes,
        vmem_limit_bytes=VMEM_LIMIT)


def _resident(shape, index_map):
    return pl.BlockSpec(shape, index_map, pipeline_mode=pl.Buffered(1))


def _silu(x):
    hx = 0.5 * x
    return hx + hx * jnp.tanh(hx)


def _rmsnorm(x, g):
    return x * lax.rsqrt(jnp.mean(x * x, axis=-1, keepdims=True) + EPS) * g


def _t5_bucket_np(rel):
    n = np.maximum(rel, 0)
    max_exact = NUM_BUCKETS // 2
    nf = np.maximum(n, 1).astype(np.float64)
    large = max_exact + (np.log(nf / max_exact) / math.log(MAX_DISTANCE / max_exact)
                         * (NUM_BUCKETS - max_exact)).astype(np.int32)
    large = np.minimum(large, NUM_BUCKETS - 1)
    return np.where(n < max_exact, n, large).astype(np.int32)


def _bucket_map_np():
    i = np.arange(WINDOW)[:, None]
    j = np.arange(WINDOW)[None, :]
    rel = np.where(j <= i, i - j, i - j + WINDOW)
    return _t5_bucket_np(rel).astype(np.int32)


def _gammas():
    return [1.0 - 2.0 ** (-5 - h) for h in range(RET_HEADS)]


def _decay_scales_np(rows, block):
    lg = np.log1p(-np.exp2(-5.0 - np.arange(RET_HEADS, dtype=np.float64)))
    i1 = (np.arange(rows) % block + 1.0)[None, :] * lg[:, None]
    q_scale = np.repeat(np.exp(i1)[:, :, None], LANES, axis=2)
    k_scale = np.repeat((np.exp(-i1) * RET_QK_DIM ** -0.5)[:, :, None], LANES, axis=2)
    return (q_scale.astype(np.float32), k_scale.astype(np.float32),
            np.exp(block * lg).astype(np.float32))


def _rope_tables_np(pos):
    half = RET_QK_DIM // 2
    inv = ROPE_BASE ** (-np.arange(half, dtype=np.float64) / half)
    ang = np.asarray(pos, np.float64)[:, None] * inv[None, :]
    return np.cos(ang).astype(np.float32), np.sin(ang).astype(np.float32)


def _bias_table_kernel(rb_ref, bucket_ref, out_ref):
    bk = bucket_ref[...]
    previous_block = (lax.broadcasted_iota(jnp.int32, bk.shape, 1)
                      > lax.broadcasted_iota(jnp.int32, bk.shape, 0))

    def one_head(h, carry):
        acc = jnp.zeros(bk.shape, F32)
        for b in range(NUM_BUCKETS):
            acc = jnp.where(bk == b, rb_ref[b, h] * LOG2E, acc)
        out_ref[1, h] = acc
        out_ref[0, h] = jnp.where(previous_block, NEG, acc)
        return carry

    lax.fori_loop(0, ATT_HEADS, one_head, 0)


def _bias_table(rel_bias):
    bucket = jnp.asarray(_bucket_map_np())
    return pl.pallas_call(
        _bias_table_kernel,
        grid=(1,),
        in_specs=[pl.BlockSpec(memory_space=pltpu.SMEM),
                  pl.BlockSpec((WINDOW, WINDOW), lambda i: (0, 0))],
        out_specs=pl.BlockSpec((2, ATT_HEADS, WINDOW, WINDOW), lambda i: (0, 0, 0, 0)),
        out_shape=jax.ShapeDtypeStruct((2, ATT_HEADS, WINDOW, WINDOW), F32),
        compiler_params=_params(1),
        name="bias_table",
    )(rel_bias, bucket)


def _pad_pair(x, x_rolled, lo, e):
    if e == 0:
        return jnp.where(lo, x, 0.0), jnp.where(lo, 0.0, x_rolled)
    return jnp.where(lo, x_rolled, 0.0), jnp.where(lo, 0.0, x)


def _attn_inproj_kernel(*refs, hosts_sample):
    x_ref, g_ref, w_ref = refs[:3]
    side_work = []
    kvpad_ref = None
    if hosts_sample:
        qs_ref, kvs_ref, ckt_ref, cvt_ref, gates_ref, bias_ref, sink_ref = refs[3:10]
        q_ref, kv_ref, gate_ref, kvpad_ref, ogs_ref, cko_ref, cvo_ref, kvt_ref = refs[10:]
        step = pl.program_id(0)

        @pl.when(step == 0)
        def _():
            kvt_ref[...] = kvs_ref[...].T

        side_work = _swa_sample_pieces(step, qs_ref, kvt_ref, ckt_ref, cvt_ref, gates_ref,
                                       bias_ref, sink_ref, ogs_ref, cko_ref, cvo_ref)
    else:
        q_ref, kv_ref, gate_ref = refs[3:]

    h = _rmsnorm(x_ref[...], g_ref[...]).astype(BF16)
    for c in range(0, ATT_IN, PROJ_COLS):
        if side_work:
            side_work.pop(0)()
        r = jnp.dot(h, w_ref[:, c:c + PROJ_COLS], preferred_element_type=F32)
        if c < ATT_WIDTH:
            q_ref[:, c:c + PROJ_COLS] = (r * (ATT_HEAD_DIM ** -0.5 * LOG2E)).astype(q_ref.dtype)
        elif c < ATT_WIDTH + 2 * ATT_KV_WIDTH:
            kv_ref[...] = r
            if kvpad_ref is not None:
                lo = lax.broadcasted_iota(jnp.int32, (r.shape[0], LANES), 1) < ATT_HEAD_DIM
                for pair in range(ATT_KV_HEADS // 2):
                    for is_v in range(2):
                        c1 = is_v * ATT_KV_WIDTH + pair * LANES
                        x2 = r[:, c1:c1 + LANES]
                        x2r = pltpu.roll(x2, ATT_HEAD_DIM, axis=1)
                        for e in range(2):
                            g0 = ((2 * pair + e) * PADDED_KV_VARIANTS + 2 * is_v) * LANES
                            lo_half, hi_half = _pad_pair(x2, x2r, lo, e)
                            kvpad_ref[:, g0:g0 + LANES] = lo_half.astype(kvpad_ref.dtype)
                            kvpad_ref[:, g0 + LANES:g0 + 2 * LANES] = hi_half.astype(kvpad_ref.dtype)
        else:
            o = c - ATT_WIDTH - 2 * ATT_KV_WIDTH
            gate_ref[:, o:o + PROJ_COLS] = r.astype(gate_ref.dtype)
    while side_work:
        side_work.pop(0)()


def _attn_inproj(x, g, w, out_dtype, sample=None):
    m = x.shape[0]
    tm = min(ATT_PROJ_ROWS, m)
    row = lambda i: (i, 0)
    const2 = lambda i: (0, 0)
    const3 = lambda i: (0, 0, 0)
    in_specs = [pl.BlockSpec((tm, D_MODEL), row),
                pl.BlockSpec((1, D_MODEL), const2),
                _resident((D_MODEL, ATT_IN), const2)]
    out_specs = [pl.BlockSpec((tm, ATT_WIDTH), row),
                 pl.BlockSpec((tm, 2 * ATT_KV_WIDTH), row),
                 pl.BlockSpec((tm, ATT_WIDTH), row)]
    out_shape = [jax.ShapeDtypeStruct((m, ATT_WIDTH), out_dtype),
                 jax.ShapeDtypeStruct((m, 2 * ATT_KV_WIDTH), F32),
                 jax.ShapeDtypeStruct((m, ATT_WIDTH), out_dtype)]
    args = [x, g, w]
    scratch_shapes = []
    if sample is not None:
        q_s, kv_s, ckt, cvt, gate_s, bias_s, sinks_rep = sample
        n, n_groups, _ = q_s.shape
        assert n == SAMPLE_ATT_BATCH * (m // tm), (n, m, tm)
        cache_spec = pl.BlockSpec((SAMPLE_ATT_BATCH, ATT_KV_WIDTH, WINDOW), lambda i: (i, 0, 0))
        in_specs += [_resident((n, n_groups, LANES), const3),
                     _resident((n, 2 * ATT_KV_WIDTH), const2),
                     cache_spec, cache_spec,
                     _resident((n, n_groups, LANES), const3),
                     _resident((ATT_HEADS, WINDOW), const2),
                     _resident((ATT_HEADS, LANES), const2)]
        out_specs += [pl.BlockSpec((tm, PADDED_KV_WIDTH), row),
                      pl.BlockSpec((n, n_groups, LANES), const3), cache_spec, cache_spec]
        out_shape += [jax.ShapeDtypeStruct((m, PADDED_KV_WIDTH), BF16),
                      jax.ShapeDtypeStruct((n, n_groups, LANES), F32),
                      jax.ShapeDtypeStruct((n, ATT_KV_WIDTH, WINDOW), F32),
                      jax.ShapeDtypeStruct((n, ATT_KV_WIDTH, WINDOW), F32)]
        args += list(sample)
        scratch_shapes = [pltpu.VMEM((2 * ATT_KV_WIDTH, n), F32)]
    return pl.pallas_call(
        functools.partial(_attn_inproj_kernel, hosts_sample=sample is not None),
        grid=(m // tm,),
        in_specs=in_specs,
        out_specs=out_specs,
        out_shape=out_shape,
        scratch_shapes=scratch_shapes,
        compiler_params=_params(1),
        name="attn_inproj",
    )(*args)


def _ret_sample_pieces(step, qt_ref, kt_ref, v_ref, gate_ref, st_ref, og_ref, sto_ref):
    lane = lax.broadcasted_iota(jnp.int32, (RET_QK_DIM, LANES), 1)
    gammas = _gammas()

    def piece(i, h):
        b = step * SAMPLE_RET_BATCH + i
        pick = lane == b
        rows = slice(h * RET_QK_DIM, (h + 1) * RET_QK_DIM)
        qc = jnp.sum(jnp.where(pick, qt_ref[rows, :], 0.0), axis=1, keepdims=True)
        kc = jnp.sum(jnp.where(pick, kt_ref[rows, :], 0.0), axis=1, keepdims=True)
        vcols = slice(h * RET_V_DIM, (h + 1) * RET_V_DIM)
        vrow = v_ref[pl.ds(b, 1), vcols]
        state = st_ref[i, h]
        o = (jnp.sum(qc * state, axis=0, keepdims=True)
             + jnp.sum(qc * kc, axis=0, keepdims=True) * vrow)
        sto_ref[i, h] = gammas[h] * (state + kc * vrow)
        og_ref[pl.ds(b, 1), vcols] = _group_norm_gate(o, gate_ref[pl.ds(b, 1), vcols])

    return [functools.partial(piece, i, h)
            for i in range(SAMPLE_RET_BATCH) for h in range(RET_HEADS)]


def _ret_inproj_kernel(*refs, with_sample):
    x_first_ref, x_next_ref, g_ref, w_ref, cos_ref, sin_ref, qsc_ref, ksc_ref = refs[:8]
    step = pl.program_id(0)
    if with_sample:
        qs_ref, ks_ref, vs_ref, gates_ref, st_ref = refs[8:13]
        (q_ref, k_ref, v_ref, gate_ref, ogs_ref, sto_ref,
         h_ref, h_next_ref, qts_ref, kts_ref) = refs[13:]
    else:
        q_ref, k_ref, v_ref, gate_ref, h_ref, h_next_ref = refs[8:]

    @pl.when(step == 0)
    def _():
        h_next_ref[...] = _rmsnorm(x_first_ref[...], g_ref[...]).astype(BF16)
        if with_sample:
            qts_ref[...] = qs_ref[...].T
            kts_ref[...] = ks_ref[...].T

    h_ref[...] = h_next_ref[...]

    def norm_next_tile():
        h_next_ref[...] = _rmsnorm(x_next_ref[...], g_ref[...]).astype(BF16)

    side_work = [norm_next_tile]
    if with_sample:
        side_work += _ret_sample_pieces(step, qts_ref, kts_ref, vs_ref, gates_ref, st_ref,
                                        ogs_ref, sto_ref)

    def interleave():
        if side_work:
            side_work.pop(0)()

    h = h_ref[...]
    half = RET_QK_DIM // 2
    cos = cos_ref[...]
    sin = sin_ref[...]
    for out_ref, scale_ref, w0 in ((q_ref, qsc_ref, 0), (k_ref, ksc_ref, RET_QK_WIDTH)):
        for c in range(0, RET_QK_WIDTH, PROJ_COLS):
            r = jnp.dot(h, w_ref[:, w0 + c:w0 + c + PROJ_COLS], preferred_element_type=F32)
            for o in range(0, PROJ_COLS, RET_QK_DIM):
                scale = scale_ref[(c + o) // RET_QK_DIM]
                x1 = r[:, o:o + half]
                x2 = r[:, o + half:o + RET_QK_DIM]
                out_ref[:, c + o:c + o + half] = (
                    (x1 * cos - x2 * sin) * scale).astype(out_ref.dtype)
                out_ref[:, c + o + half:c + o + RET_QK_DIM] = (
                    (x1 * sin + x2 * cos) * scale).astype(out_ref.dtype)
            interleave()
    for c in range(0, RET_WIDTH, PROJ_COLS):
        v_ref[:, c:c + PROJ_COLS] = jnp.dot(
            h, w_ref[:, 2 * RET_QK_WIDTH + c:2 * RET_QK_WIDTH + c + PROJ_COLS],
            preferred_element_type=F32).astype(v_ref.dtype)
        interleave()
    for c in range(0, RET_WIDTH, PROJ_COLS):
        w0 = 2 * RET_QK_WIDTH + RET_WIDTH + c
        gate_ref[:, c:c + PROJ_COLS] = jnp.dot(
            h, w_ref[:, w0:w0 + PROJ_COLS], preferred_element_type=F32).astype(gate_ref.dtype)
        interleave()
    while side_work:
        interleave()


def _ret_inproj(x, g, w, pos, block, out_dtype, sample=None):
    m = x.shape[0]
    tm = min(PROJ_ROWS, m)
    n_pos = len(pos) // tm
    half = RET_QK_DIM // 2
    cos, sin = _rope_tables_np(pos)
    q_scale, k_scale, _ = _decay_scales_np(tm, block)
    row = lambda i: (i, 0)
    const2 = lambda i: (0, 0)
    const3 = lambda i: (0, 0, 0)
    last_tile = m // tm - 1
    in_specs = [_resident((tm, D_MODEL), const2),
                pl.BlockSpec((tm, D_MODEL), lambda i: (jnp.minimum(i + 1, last_tile), 0)),
                pl.BlockSpec((1, D_MODEL), const2),
                _resident((D_MODEL, RET_IN), const2),
                pl.BlockSpec((tm, half), lambda i: (i % n_pos, 0)),
                pl.BlockSpec((tm, half), lambda i: (i % n_pos, 0)),
                _resident((RET_HEADS, tm, LANES), const3),
                _resident((RET_HEADS, tm, LANES), const3)]
    out_specs = [pl.BlockSpec((tm, RET_QK_WIDTH), row),
                 pl.BlockSpec((tm, RET_QK_WIDTH), row),
                 pl.BlockSpec((tm, RET_WIDTH), row),
                 pl.BlockSpec((tm, RET_WIDTH), row)]
    out_shape = [jax.ShapeDtypeStruct((m, RET_QK_WIDTH), out_dtype),
                 jax.ShapeDtypeStruct((m, RET_QK_WIDTH), out_dtype),
                 jax.ShapeDtypeStruct((m, RET_WIDTH), out_dtype),
                 jax.ShapeDtypeStruct((m, RET_WIDTH), out_dtype)]
    args = [x, x, g, w, jnp.asarray(cos), jnp.asarray(sin), jnp.asarray(q_scale),
            jnp.asarray(k_scale)]
    scratch_shapes = [pltpu.VMEM((tm, D_MODEL), BF16), pltpu.VMEM((tm, D_MODEL), BF16)]
    if sample is not None:
        q_s, k_s, v_s, gate_s, state = sample
        n = q_s.shape[0]
        assert n == SAMPLE_RET_BATCH * (m // tm), (n, m, tm)
        st_spec = pl.BlockSpec((SAMPLE_RET_BATCH, RET_HEADS, RET_QK_DIM, RET_V_DIM),
                               lambda i: (i, 0, 0, 0))
        in_specs += [_resident((n, RET_QK_WIDTH), const2), _resident((n, RET_QK_WIDTH), const2),
                     _resident((n, RET_WIDTH), const2), _resident((n, RET_WIDTH), const2), st_spec]
        out_specs += [pl.BlockSpec((n, RET_WIDTH), const2), st_spec]
        out_shape += [jax.ShapeDtypeStruct((n, RET_WIDTH), F32),
                      jax.ShapeDtypeStruct(state.shape, F32)]
        args += [q_s, k_s, v_s, gate_s, state]
        scratch_shapes += [pltpu.VMEM((RET_QK_WIDTH, n), F32), pltpu.VMEM((RET_QK_WIDTH, n), F32)]
    return pl.pallas_call(
        functools.partial(_ret_inproj_kernel, with_sample=sample is not None),
        grid=(m // tm,),
        in_specs=in_specs,
        out_specs=out_specs,
        out_shape=out_shape,
        scratch_shapes=scratch_shapes,
        compiler_params=_params(1),
        name="ret_inproj",
    )(*args)


def _outproj_kernel(o_ref, w_ref, x_ref, y_ref):
    y_ref[...] = x_ref[...] + jnp.dot(o_ref[...].astype(BF16), w_ref[...].astype(BF16),
                                      preferred_element_type=F32)


def _outproj_norm_kernel(o_ref, w_ref, x_ref, g_ref, y_ref):
    y = x_ref[...] + jnp.dot(o_ref[...].astype(BF16), w_ref[...].astype(BF16), preferred_element_type=F32)
    y_ref[...] = _rmsnorm(y, g_ref[...])


def _outproj(o, w, x, final_g=None):
    m, width = o.shape
    tm = min(OUT_ROWS, m)
    row = lambda i: (i, 0)
    in_specs = [pl.BlockSpec((tm, width), row),
                _resident((width, D_MODEL), lambda i: (0, 0)),
                pl.BlockSpec((tm, D_MODEL), row)]
    args = [o, w, x]
    body = _outproj_kernel
    if final_g is not None:
        in_specs.append(pl.BlockSpec((1, D_MODEL), lambda i: (0, 0)))
        args.append(final_g)
        body = _outproj_norm_kernel
    return pl.pallas_call(
        body,
        grid=(m // tm,),
        in_specs=in_specs,
        out_specs=pl.BlockSpec((tm, D_MODEL), row),
        out_shape=jax.ShapeDtypeStruct((m, D_MODEL), F32),
        compiler_params=_params(1),
        name="outproj",
    )(*args)


def _swa_prompt_kernel(sink_ref, q_ref, kvc_ref, kvp_ref, gate_ref, bias_ref, out_ref):
    j = pl.program_id(1)
    half = ATT_HEAD_DIM
    pairs_per_kv = ATT_GROUP // 2
    kv_group_cols = ATT_GROUP * ATT_HEAD_DIM
    contract_last = (((1,), (1,)), ((), ()))
    lo1 = lax.broadcasted_iota(jnp.int32, (WINDOW, LANES), 1) < half
    own = (lax.broadcasted_iota(jnp.int32, (WINDOW, WINDOW), 1)
           <= lax.broadcasted_iota(jnp.int32, (WINDOW, WINDOW), 0))
    units =[(blk, kvh) for blk in range(ATT_ROWS // WINDOW) for kvh in range(ATT_KV_HEADS)]

    def keys_values(blk, kvh):
        r0 = blk * WINDOW
        out = []
        for variant in range(PADDED_KV_VARIANTS):
            c = (kvh * PADDED_KV_VARIANTS + variant) * LANES
            prev = kvp_ref[:, c:c + LANES] if blk == 0 else kvc_ref[r0 - WINDOW:r0, c:c + LANES]
            out.append(jnp.concatenate([prev, kvc_ref[r0:r0 + WINDOW, c:c + LANES]], axis=0))
        return out

    def scores(blk, kvh):
        rows = slice(blk * WINDOW, (blk + 1) * WINDOW)
        c0 = kvh * kv_group_cols
        k_a, k_b, v_a, v_b = keys_values(blk, kvh)
        qs = jnp.concatenate(
            [q_ref[rows, c0 + p * LANES:c0 + (p + 1) * LANES] for p in range(pairs_per_kv)],
            axis=0)
        logits = (lax.dot_general(qs, k_a, contract_last, preferred_element_type=F32),
                  lax.dot_general(qs, k_b, contract_last, preferred_element_type=F32))
        return logits, v_a, v_b

    def softmax(blk, kvh, logits):
        sel = jnp.where(j == 0, 0, 1) if blk == 0 else 1
        probs = ([], [])
        inv = []
        for p in range(pairs_per_kv):
            top, total, sink = [], [], []
            for st in range(2):
                hd = kvh * ATT_GROUP + 2 * p + st
                lg = logits[st][p * WINDOW:(p + 1) * WINDOW, :]
                l = jnp.where(own, lg[:, WINDOW:], lg[:, :WINDOW]) + bias_ref[sel, hd]
                s = sink_ref[hd]
                m = jnp.maximum(jnp.max(l, axis=-1, keepdims=True), s)
                pe = jnp.exp2(l - m)
                probs[st].append(jnp.concatenate(
                    [jnp.where(own, 0.0, pe).astype(BF16),
                     jnp.where(own, pe, 0.0).astype(BF16)], axis=1))
                top.append(m)
                total.append(jnp.sum(pe, axis=-1, keepdims=True))
                sink.append(s)
            sink_term = jnp.exp2(jnp.where(lo1, sink[0], sink[1]) - jnp.where(lo1, top[0], top[1]))
            inv.append(1.0 / (jnp.where(lo1, total[0], total[1]) + sink_term))
        return jnp.concatenate(probs[0], axis=0), jnp.concatenate(probs[1], axis=0), inv

    def output(blk, kvh, p_a, p_b, inv, v_a, v_b):
        rows = slice(blk * WINDOW, (blk + 1) * WINDOW)
        c0 = kvh * kv_group_cols
        o = (jnp.dot(p_a, v_a, preferred_element_type=F32)
             + jnp.dot(p_b, v_b, preferred_element_type=F32))
        for p in range(pairs_per_kv):
            cols = slice(c0 + p * LANES, c0 + (p + 1) * LANES)
            gt = gate_ref[rows, cols].astype(F32)
            og = o[p * WINDOW:(p + 1) * WINDOW, :] * inv[p] * _silu(gt)
            out_ref[rows, cols] = og.astype(out_ref.dtype)

    for unit in units:
        logits, v_a, v_b = scores(*unit)
        p_a, p_b, inv = softmax(*unit, logits)
        output(*unit, p_a, p_b, inv, v_a, v_b)


def _swa_prompt(sinks, q, kvpad, gate, bias, batch, seq):
    m = batch * seq
    nt = seq // ATT_ROWS
    blocks_per_tile = ATT_ROWS // WINDOW
    blocks_per_seq = seq // WINDOW
    tile = lambda b, j: (b * nt + j, 0)
    prev = lambda b, j: (b * blocks_per_seq + jnp.maximum(j * blocks_per_tile - 1, 0), 0)
    return pl.pallas_call(
        _swa_prompt_kernel,
        grid=(batch, nt),
        in_specs=[pl.BlockSpec(memory_space=pltpu.SMEM),
                  pl.BlockSpec((ATT_ROWS, ATT_WIDTH), tile),
                  pl.BlockSpec((ATT_ROWS, PADDED_KV_WIDTH), tile),
                  pl.BlockSpec((WINDOW, PADDED_KV_WIDTH), prev),
                  pl.BlockSpec((ATT_ROWS, ATT_WIDTH), tile),
                  pl.BlockSpec((2, ATT_HEADS, WINDOW, WINDOW), lambda b, j: (0, 0, 0, 0))],
        out_specs=pl.BlockSpec((ATT_ROWS, ATT_WIDTH), tile),
        out_shape=jax.ShapeDtypeStruct((m, ATT_WIDTH), BF16),
        compiler_params=_params(2),
        name="swa_prompt",
    )(sinks, q, kvpad, kvpad, gate, bias)


def _sample_head_order():
    order = []
    for pair in range(ATT_KV_HEADS // 2):
        base = pair * 2 * ATT_GROUP
        for kv in range(2):
            for half in range(2):
                order += [base + kv * ATT_GROUP + 2 * g + half for g in range(ATT_GROUP // 2)]
    return np.asarray(order, np.int32)


def _swa_sample_pieces(step, q_ref, kvt_ref, ckt_ref, cvt_ref, gate_ref, bias_ref, sink_ref,
                       o_ref, cko_ref, cvo_ref):
    half = ATT_HEAD_DIM
    quarter = ATT_GROUP // 2
    contract_last = (((1,), (1,)), ((), ()))
    newest = lax.broadcasted_iota(jnp.int32, (ATT_KV_WIDTH, WINDOW), 1) == WINDOW - 1
    seq_lane = lax.broadcasted_iota(jnp.int32, (2 * ATT_KV_WIDTH, LANES), 1)
    lo4 = lax.broadcasted_iota(jnp.int32, (quarter, LANES), 1) < half
    new_cache = {}

    def update(i):
        b = step * SAMPLE_ATT_BATCH + i
        new_col = jnp.sum(jnp.where(seq_lane == b, kvt_ref[...], 0.0), axis=1, keepdims=True)
        kb = jnp.where(newest, new_col[:ATT_KV_WIDTH], pltpu.roll(ckt_ref[i], WINDOW - 1, axis=1))
        vb = jnp.where(newest, new_col[ATT_KV_WIDTH:], pltpu.roll(cvt_ref[i], WINDOW - 1, axis=1))
        cko_ref[i] = kb
        cvo_ref[i] = vb
        new_cache[i] = (kb, vb)

    def attend(i, pair):
        b = step * SAMPLE_ATT_BATCH + i
        kb, vb = new_cache[i]
        groups = slice(pair * ATT_GROUP, (pair + 1) * ATT_GROUP)
        hs = slice(pair * 2 * ATT_GROUP, (pair + 1) * 2 * ATT_GROUP)
        feat = slice(pair * LANES, (pair + 1) * LANES)
        g8 = q_ref[b, groups, :]
        g8r = pltpu.roll(g8, half, axis=1)
        qbd = jnp.concatenate(
            [jnp.where(lo4, g8[:quarter], 0.0), jnp.where(lo4, g8r[:quarter], 0.0),
             jnp.where(lo4, 0.0, g8r[quarter:]), jnp.where(lo4, 0.0, g8[quarter:])],
            axis=0)
        l = jnp.dot(qbd, kb[feat, :], preferred_element_type=F32) + bias_ref[hs, :]
        s = sink_ref[hs, 0:1]
        m = jnp.maximum(jnp.max(l, axis=-1, keepdims=True), s)
        pe = jnp.exp2(l - m)
        den = jnp.sum(pe, axis=-1, keepdims=True) + jnp.exp2(s - m)
        o2 = lax.dot_general(pe, vb[feat, :], contract_last,
                             preferred_element_type=F32) / den
        o2r = pltpu.roll(o2, half, axis=1)
        og = jnp.concatenate(
            [jnp.where(lo4, o2[:quarter], o2r[quarter:2 * quarter]),
             jnp.where(lo4, o2r[2 * quarter:3 * quarter], o2[3 * quarter:])], axis=0)
        o_ref[b, groups, :] = og * _silu(gate_ref[b, groups, :])

    pieces = []
    for i in range(SAMPLE_ATT_BATCH):
        pieces.append(functools.partial(update, i))
        pieces += [functools.partial(attend, i, pair) for pair in range(ATT_KV_HEADS // 2)]
    return pieces


def _group_norm_gate(o, gate):
    mu = jnp.mean(o, axis=-1, keepdims=True)
    d = o - mu
    var = jnp.mean(d * d, axis=-1, keepdims=True)
    return d * lax.rsqrt(var + EPS) * _silu(gate)


def _ret_prompt_kernel(g_block_ref, q_ref, k_ref, v_ref, gate_ref, x_ref, wout_ref, gf_ref,
                       y_ref, r_ref, og_ref):
    c = pl.program_id(1)

    @pl.when(c == 0)
    def _():
        r_ref[...] = jnp.zeros_like(r_ref)

    causal = (lax.broadcasted_iota(jnp.int32, (RET_BLOCK, RET_BLOCK), 1)
              <= lax.broadcasted_iota(jnp.int32, (RET_BLOCK, RET_BLOCK), 0))
    contract_last = (((1,), (1,)), ((), ()))
    n_blocks = RET_ROWS // RET_BLOCK
    for blk in range(n_blocks):
        rows = slice(blk * RET_BLOCK, (blk + 1) * RET_BLOCK)
        for h in range(RET_HEADS):
            qcols = slice(h * RET_QK_DIM, (h + 1) * RET_QK_DIM)
            vcols = slice(h * RET_V_DIM, (h + 1) * RET_V_DIM)
            qh = q_ref[rows, qcols]
            kh = k_ref[rows, qcols]
            vh = v_ref[rows, vcols]
            state = r_ref[0, h]
            scores = jnp.where(
                causal, lax.dot_general(qh, kh, contract_last, preferred_element_type=F32), 0.0)
            o = (jnp.dot(scores.astype(BF16), vh, preferred_element_type=F32)
                 + jnp.dot(qh, state.astype(BF16), preferred_element_type=F32))
            r_ref[0, h] = g_block_ref[h] * (
                state + jnp.dot(kh.T, vh, preferred_element_type=F32))
            og_ref[h, rows, :] = _group_norm_gate(
                o, gate_ref[rows, vcols].astype(F32)).astype(og_ref.dtype)
            if blk == n_blocks - 1:
                part = jnp.dot(og_ref[h], wout_ref[vcols, :].astype(BF16),
                               preferred_element_type=F32)
                if h == 0:
                    y_ref[...] = x_ref[...] + part
                elif h < RET_HEADS - 1:
                    y_ref[...] += part
                else:
                    y_ref[...] = _rmsnorm(y_ref[...] + part, gf_ref[...])


def _ret_prompt(q, k, v, gate, x, w_out, final_g, batch, seq):
    m = batch * seq
    nt = seq // RET_ROWS
    g_block = jnp.asarray(_decay_scales_np(RET_BLOCK, RET_BLOCK)[2])
    rows = lambda b, c: (b * nt + c, 0)
    return pl.pallas_call(
        _ret_prompt_kernel,
        grid=(batch, nt),
        in_specs=[pl.BlockSpec(memory_space=pltpu.SMEM),
                  pl.BlockSpec((RET_ROWS, RET_QK_WIDTH), rows),
                  pl.BlockSpec((RET_ROWS, RET_QK_WIDTH), rows),
                  pl.BlockSpec((RET_ROWS, RET_WIDTH), rows),
                  pl.BlockSpec((RET_ROWS, RET_WIDTH), rows),
                  pl.BlockSpec((RET_ROWS, D_MODEL), rows),
                  _resident((RET_WIDTH, D_MODEL), lambda b, c: (0, 0)),
                  pl.BlockSpec((1, D_MODEL), lambda b, c: (0, 0))],
        out_specs=[pl.BlockSpec((RET_ROWS, D_MODEL), rows),
                   pl.BlockSpec((1, RET_HEADS, RET_QK_DIM, RET_V_DIM), lambda b, c: (b, 0, 0, 0))],
        out_shape=[jax.ShapeDtypeStruct((m, D_MODEL), F32),
                   jax.ShapeDtypeStruct((batch, RET_HEADS, RET_QK_DIM, RET_V_DIM), F32)],
        scratch_shapes=[pltpu.VMEM((RET_HEADS, RET_ROWS, RET_V_DIM), BF16)],
        compiler_params=_params(2),
        name="ret_prompt",
    )(g_block, q, k, v, gate, x, w_out, final_g)


def kernel(x_prompt, x_sample, cache_swa_k, cache_swa_v, state_ret, norm_g, final_norm_g, rel_bias,
           w_in_attn, attn_sinks, w_out_attn, w_in_ret, w_out_ret):
    batch, seq, _ = x_prompt.shape
    n_s = x_sample.shape[0]
    xp = x_prompt.reshape(batch * seq, D_MODEL)
    xs = x_sample.reshape(n_s, D_MODEL)
    g0 = norm_g[0].reshape(1, D_MODEL)
    g1 = norm_g[1].reshape(1, D_MODEL)
    gf = final_norm_g.reshape(1, D_MODEL)
    w_in_a = w_in_attn[0].astype(BF16)
    w_out_a = w_out_attn[0]
    w_in_r = w_in_ret[0].astype(BF16)
    w_out_r = w_out_ret[0]
    sinks = attn_sinks[0] * LOG2E

    bias = _bias_table(rel_bias)

    qs, kvs, gates = _attn_inproj(xs, g0, w_in_a, F32)
    order = _sample_head_order()
    bias_s = bias[1, :, WINDOW - 1, :][order]
    sinks_rep = jnp.broadcast_to(sinks[order][:, None], (ATT_HEADS, LANES))
    to_feature_major = lambda c: jnp.transpose(c, (0, 2, 3, 1)).reshape(n_s, ATT_KV_WIDTH, WINDOW)
    to_window_major = lambda c: jnp.transpose(
        c.reshape(n_s, ATT_KV_HEADS, ATT_HEAD_DIM, WINDOW), (0, 3, 1, 2))[None]
    n_groups = ATT_WIDTH // LANES
    q, kv, gate, kvpad, ogs, ckt_new, cvt_new = _attn_inproj(
        xp, g0, w_in_a, BF16,
        sample=(qs.reshape(n_s, n_groups, LANES), kvs,
                to_feature_major(cache_swa_k[0]), to_feature_major(cache_swa_v[0]),
                gates.reshape(n_s, n_groups, LANES), bias_s, sinks_rep))
    og = _swa_prompt(sinks, q, kvpad, gate, bias, batch, seq)
    xp1 = _outproj(og, w_out_a, xp)
    kv_last = kv.reshape(batch, seq, 2 * ATT_KV_WIDTH)[:, seq - WINDOW:]
    k_prompt = kv_last[..., :ATT_KV_WIDTH].reshape(1, batch, WINDOW, ATT_KV_HEADS, ATT_HEAD_DIM)
    v_prompt = kv_last[..., ATT_KV_WIDTH:].reshape(1, batch, WINDOW, ATT_KV_HEADS, ATT_HEAD_DIM)
    xs1 = _outproj(ogs.reshape(n_s, ATT_WIDTH), w_out_a, xs)
    k_sample = to_window_major(ckt_new)
    v_sample = to_window_major(cvt_new)

    q_s, k_s, v_s, gate_s = _ret_inproj(xs1, g1, w_in_r, np.full((n_s,), PAST_LEN), 1, F32)
    q, k, v, gate, og_s, r_sample = _ret_inproj(
        xp1, g1, w_in_r, np.arange(seq), RET_BLOCK, BF16,
        sample=(q_s, k_s, v_s, gate_s, state_ret[0]))
    y_prompt, r_prompt = _ret_prompt(q, k, v, gate, xp1, w_out_r, gf, batch, seq)
    y_prompt = y_prompt.reshape(batch, seq, D_MODEL)
    y_sample = _outproj(og_s, w_out_r, xs1, gf).reshape(n_s, 1, D_MODEL)

    return (y_prompt, y_sample, k_prompt, v_prompt, r_prompt[None],
            k_sample, v_sample, r_sample[None])
```

```python
import functools
import math

import numpy as np
import jax
import jax.numpy as jnp
from jax import lax
from jax.experimental import pallas as pl
from jax.experimental.pallas import tpu as pltpu

F32 = jnp.float32
BF16 = jnp.bfloat16

D_MODEL = 1024
PAST_LEN = 16384
ATT_HEADS = 32
ATT_KV_HEADS = 4
ATT_GROUP = ATT_HEADS // ATT_KV_HEADS
ATT_HEAD_DIM = 64
ATT_WIDTH = ATT_HEADS * ATT_HEAD_DIM
ATT_KV_WIDTH = ATT_KV_HEADS * ATT_HEAD_DIM
ATT_IN = 2 * ATT_WIDTH + 2 * ATT_KV_WIDTH
WINDOW = 128
NUM_BUCKETS = 32
MAX_DISTANCE = 128
RET_HEADS = 4
RET_QK_DIM = 256
RET_V_DIM = 512
RET_QK_WIDTH = RET_HEADS * RET_QK_DIM
RET_WIDTH = RET_HEADS * RET_V_DIM
RET_IN = 2 * RET_QK_WIDTH + 2 * RET_WIDTH
ROPE_BASE = 10000.0
EPS = 1e-6
NEG = -1e30
LOG2E = math.log2(math.e)

LANES = 128
PADDED_KV_VARIANTS = 4
PADDED_KV_WIDTH = ATT_KV_HEADS * PADDED_KV_VARIANTS * LANES
PROJ_ROWS = 512
ATT_PROJ_ROWS = 1024
PROJ_COLS = 512
OUT_ROWS = 1024
ATT_ROWS = 1024
RET_BLOCK = 256
RET_ROWS = 1024
SAMPLE_ATT_BATCH = 4
SAMPLE_RET_BATCH = 2
VMEM_LIMIT = 60 * 1024 * 1024


def _params(n_axes):
    return pltpu.CompilerParams(
        dimension_semantics=("arbitrary",) * n_axes,
        vmem_limit_bytes=VMEM_LIMIT)


def _resident(shape, index_map):
    return pl.BlockSpec(shape, index_map, pipeline_mode=pl.Buffered(1))


def _silu(x):
    hx = 0.5 * x
    return hx + hx * jnp.tanh(hx)


def _rmsnorm(x, g):
    return x * lax.rsqrt(jnp.mean(x * x, axis=-1, keepdims=True) + EPS) * g


def _t5_bucket_np(rel):
    n = np.maximum(rel, 0)
    max_exact = NUM_BUCKETS // 2
    nf = np.maximum(n, 1).astype(np.float64)
    large = max_exact + (np.log(nf / max_exact) / math.log(MAX_DISTANCE / max_exact)
                         * (NUM_BUCKETS - max_exact)).astype(np.int32)
    large = np.minimum(large, NUM_BUCKETS - 1)
    return np.where(n < max_exact, n, large).astype(np.int32)


def _bucket_map_np():
    i = np.arange(WINDOW)[:, None]
    j = np.arange(WINDOW)[None, :]
    rel = np.where(j <= i, i - j, i - j + WINDOW)
    return _t5_bucket_np(rel).astype(np.int32)


def _gammas():
    return [1.0 - 2.0 ** (-5 - h) for h in range(RET_HEADS)]


def _decay_scales_np(rows, block):
    lg = np.log1p(-np.exp2(-5.0 - np.arange(RET_HEADS, dtype=np.float64)))
    i1 = (np.arange(rows) % block + 1.0)[None, :] * lg[:, None]
    q_scale = np.repeat(np.exp(i1)[:, :, None], LANES, axis=2)
    k_scale = np.repeat((np.exp(-i1) * RET_QK_DIM ** -0.5)[:, :, None], LANES, axis=2)
    return (q_scale.astype(np.float32), k_scale.astype(np.float32),
            np.exp(block * lg).astype(np.float32))


def _rope_tables_np(pos):
    half = RET_QK_DIM // 2
    inv = ROPE_BASE ** (-np.arange(half, dtype=np.float64) / half)
    ang = np.asarray(pos, np.float64)[:, None] * inv[None, :]
    return np.cos(ang).astype(np.float32), np.sin(ang).astype(np.float32)


def _bias_table_kernel(rb_ref, bucket_ref, out_ref):
    bk = bucket_ref[...]
    previous_block = (lax.broadcasted_iota(jnp.int32, bk.shape, 1)
                      > lax.broadcasted_iota(jnp.int32, bk.shape, 0))

    def one_head(h, carry):
        acc = jnp.zeros(bk.shape, F32)
        for b in range(NUM_BUCKETS):
            acc = jnp.where(bk == b, rb_ref[b, h] * LOG2E, acc)
        out_ref[1, h] = acc
        out_ref[0, h] = jnp.where(previous_block, NEG, acc)
        return carry

    lax.fori_loop(0, ATT_HEADS, one_head, 0)


def _bias_table(rel_bias):
    bucket = jnp.asarray(_bucket_map_np())
    return pl.pallas_call(
        _bias_table_kernel,
        grid=(1,),
        in_specs=[pl.BlockSpec(memory_space=pltpu.SMEM),
                  pl.BlockSpec((WINDOW, WINDOW), lambda i: (0, 0))],
        out_specs=pl.BlockSpec((2, ATT_HEADS, WINDOW, WINDOW), lambda i: (0, 0, 0, 0)),
        out_shape=jax.ShapeDtypeStruct((2, ATT_HEADS, WINDOW, WINDOW), F32),
        compiler_params=_params(1),
        name="bias_table",
    )(rel_bias, bucket)


def _pad_pair(x, x_rolled, lo, e):
    if e == 0:
        return jnp.where(lo, x, 0.0), jnp.where(lo, 0.0, x_rolled)
    return jnp.where(lo, x_rolled, 0.0), jnp.where(lo, 0.0, x)


def _attn_inproj_kernel(*refs, hosts_sample):
    x_ref, g_ref, w_ref = refs[:3]
    side_work = []
    kvpad_ref = None
    if hosts_sample:
        qs_ref, kvs_ref, ckt_ref, cvt_ref, gates_ref, bias_ref, sink_ref = refs[3:10]
        q_ref, kv_ref, gate_ref, kvpad_ref, ogs_ref, cko_ref, cvo_ref, kvt_ref = refs[10:]
        step = pl.program_id(0)

        @pl.when(step == 0)
        def _():
            kvt_ref[...] = kvs_ref[...].T

        side_work = _swa_sample_pieces(step, qs_ref, kvt_ref, ckt_ref, cvt_ref, gates_ref,
                                       bias_ref, sink_ref, ogs_ref, cko_ref, cvo_ref)
    else:
        q_ref, kv_ref, gate_ref = refs[3:]

    h = _rmsnorm(x_ref[...], g_ref[...]).astype(BF16)
    for c in range(0, ATT_IN, PROJ_COLS):
        if side_work:
            side_work.pop(0)()
        r = jnp.dot(h, w_ref[:, c:c + PROJ_COLS], preferred_element_type=F32)
        if c < ATT_WIDTH:
            q_ref[:, c:c + PROJ_COLS] = (r * (ATT_HEAD_DIM ** -0.5 * LOG2E)).astype(q_ref.dtype)
        elif c < ATT_WIDTH + 2 * ATT_KV_WIDTH:
            kv_ref[...] = r
            if kvpad_ref is not None:
                lo = lax.broadcasted_iota(jnp.int32, (r.shape[0], LANES), 1) < ATT_HEAD_DIM
                for pair in range(ATT_KV_HEADS // 2):
                    for is_v in range(2):
                        c1 = is_v * ATT_KV_WIDTH + pair * LANES
                        x2 = r[:, c1:c1 + LANES]
                        x2r = pltpu.roll(x2, ATT_HEAD_DIM, axis=1)
                        for e in range(2):
                            g0 = ((2 * pair + e) * PADDED_KV_VARIANTS + 2 * is_v) * LANES
                            lo_half, hi_half = _pad_pair(x2, x2r, lo, e)
                            kvpad_ref[:, g0:g0 + LANES] = lo_half.astype(kvpad_ref.dtype)
                            kvpad_ref[:, g0 + LANES:g0 + 2 * LANES] = hi_half.astype(kvpad_ref.dtype)
        else:
            o = c - ATT_WIDTH - 2 * ATT_KV_WIDTH
            gate_ref[:, o:o + PROJ_COLS] = r.astype(gate_ref.dtype)
    while side_work:
        side_work.pop(0)()


def _attn_inproj(x, g, w, out_dtype, sample=None):
    m = x.shape[0]
    tm = min(ATT_PROJ_ROWS, m)
    row = lambda i: (i, 0)
    const2 = lambda i: (0, 0)
    const3 = lambda i: (0, 0, 0)
    in_specs = [pl.BlockSpec((tm, D_MODEL), row),
                pl.BlockSpec((1, D_MODEL), const2),
                _resident((D_MODEL, ATT_IN), const2)]
    out_specs = [pl.BlockSpec((tm, ATT_WIDTH), row),
                 pl.BlockSpec((tm, 2 * ATT_KV_WIDTH), row),
                 pl.BlockSpec((tm, ATT_WIDTH), row)]
    out_shape = [jax.ShapeDtypeStruct((m, ATT_WIDTH), out_dtype),
                 jax.ShapeDtypeStruct((m, 2 * ATT_KV_WIDTH), F32),
                 jax.ShapeDtypeStruct((m, ATT_WIDTH), out_dtype)]
    args = [x, g, w]
    scratch_shapes = []
    if sample is not None:
        q_s, kv_s, ckt, cvt, gate_s, bias_s, sinks_rep = sample
        n, n_groups, _ = q_s.shape
        assert n == SAMPLE_ATT_BATCH * (m // tm), (n, m, tm)
        cache_spec = pl.BlockSpec((SAMPLE_ATT_BATCH, ATT_KV_WIDTH, WINDOW), lambda i: (i, 0, 0))
        in_specs += [_resident((n, n_groups, LANES), const3),
                     _resident((n, 2 * ATT_KV_WIDTH), const2),
                     cache_spec, cache_spec,
                     _resident((n, n_groups, LANES), const3),
                     _resident((ATT_HEADS, WINDOW), const2),
                     _resident((ATT_HEADS, LANES), const2)]
        out_specs += [pl.BlockSpec((tm, PADDED_KV_WIDTH), row),
                      pl.BlockSpec((n, n_groups, LANES), const3), cache_spec, cache_spec]
        out_shape += [jax.ShapeDtypeStruct((m, PADDED_KV_WIDTH), BF16),
                      jax.ShapeDtypeStruct((n, n_groups, LANES), F32),
                      jax.ShapeDtypeStruct((n, ATT_KV_WIDTH, WINDOW), F32),
                      jax.ShapeDtypeStruct((n, ATT_KV_WIDTH, WINDOW), F32)]
        args += list(sample)
        scratch_shapes = [pltpu.VMEM((2 * ATT_KV_WIDTH, n), F32)]
    return pl.pallas_call(
        functools.partial(_attn_inproj_kernel, hosts_sample=sample is not None),
        grid=(m // tm,),
        in_specs=in_specs,
        out_specs=out_specs,
        out_shape=out_shape,
        scratch_shapes=scratch_shapes,
        compiler_params=_params(1),
        name="attn_inproj",
    )(*args)


def _ret_sample_pieces(step, qt_ref, kt_ref, v_ref, gate_ref, st_ref, og_ref, sto_ref):
    lane = lax.broadcasted_iota(jnp.int32, (RET_QK_DIM, LANES), 1)
    gammas = _gammas()

    def piece(i, h):
        b = step * SAMPLE_RET_BATCH + i
        pick = lane == b
        rows = slice(h * RET_QK_DIM, (h + 1) * RET_QK_DIM)
        qc = jnp.sum(jnp.where(pick, qt_ref[rows, :], 0.0), axis=1, keepdims=True)
        kc = jnp.sum(jnp.where(pick, kt_ref[rows, :], 0.0), axis=1, keepdims=True)
        vcols = slice(h * RET_V_DIM, (h + 1) * RET_V_DIM)
        vrow = v_ref[pl.ds(b, 1), vcols]
        state = st_ref[i, h]
        o = (jnp.sum(qc * state, axis=0, keepdims=True)
             + jnp.sum(qc * kc, axis=0, keepdims=True) * vrow)
        sto_ref[i, h] = gammas[h] * (state + kc * vrow)
        og_ref[pl.ds(b, 1), vcols] = _group_norm_gate(o, gate_ref[pl.ds(b, 1), vcols])

    return [functools.partial(piece, i, h)
            for i in range(SAMPLE_RET_BATCH) for h in range(RET_HEADS)]


def _ret_inproj_kernel(*refs, with_sample):
    x_ref, g_ref, w_ref, cos_ref, sin_ref, qsc_ref, ksc_ref = refs[:7]
    side_work = []
    if with_sample:
        qs_ref, ks_ref, vs_ref, gates_ref, st_ref = refs[7:12]
        q_ref, k_ref, v_ref, gate_ref, ogs_ref, sto_ref, qts_ref, kts_ref = refs[12:]
        step = pl.program_id(0)

        @pl.when(step == 0)
        def _():
            qts_ref[...] = qs_ref[...].T
            kts_ref[...] = ks_ref[...].T

        side_work = _ret_sample_pieces(step, qts_ref, kts_ref, vs_ref, gates_ref, st_ref,
                                       ogs_ref, sto_ref)
    else:
        q_ref, k_ref, v_ref, gate_ref = refs[7:]

    def interleave():
        if side_work:
            side_work.pop(0)()

    h = _rmsnorm(x_ref[...], g_ref[...]).astype(BF16)
    half = RET_QK_DIM // 2
    cos = cos_ref[...]
    sin = sin_ref[...]
    for out_ref, scale_ref, w0 in ((q_ref, qsc_ref, 0), (k_ref, ksc_ref, RET_QK_WIDTH)):
        for c in range(0, RET_QK_WIDTH, PROJ_COLS):
            interleave()
            r = jnp.dot(h, w_ref[:, w0 + c:w0 + c + PROJ_COLS], preferred_element_type=F32)
            for o in range(0, PROJ_COLS, RET_QK_DIM):
                scale = scale_ref[(c + o) // RET_QK_DIM]
                x1 = r[:, o:o + half]
                x2 = r[:, o + half:o + RET_QK_DIM]
                out_ref[:, c + o:c + o + half] = (
                    (x1 * cos - x2 * sin) * scale).astype(out_ref.dtype)
                out_ref[:, c + o + half:c + o + RET_QK_DIM] = (
                    (x1 * sin + x2 * cos) * scale).astype(out_ref.dtype)
    for c in range(0, RET_WIDTH, PROJ_COLS):
        interleave()
        v_ref[:, c:c + PROJ_COLS] = jnp.dot(
            h, w_ref[:, 2 * RET_QK_WIDTH + c:2 * RET_QK_WIDTH + c + PROJ_COLS],
            preferred_element_type=F32).astype(v_ref.dtype)
    for c in range(0, RET_WIDTH, PROJ_COLS):
        interleave()
        w0 = 2 * RET_QK_WIDTH + RET_WIDTH + c
        gate_ref[:, c:c + PROJ_COLS] = jnp.dot(
            h, w_ref[:, w0:w0 + PROJ_COLS], preferred_element_type=F32).astype(gate_ref.dtype)
    while side_work:
        interleave()


def _ret_inproj(x, g, w, pos, block, out_dtype, sample=None):
    m = x.shape[0]
    tm = min(PROJ_ROWS, m)
    n_pos = len(pos) // tm
    half = RET_QK_DIM // 2
    cos, sin = _rope_tables_np(pos)
    q_scale, k_scale, _ = _decay_scales_np(tm, block)
    row = lambda i: (i, 0)
    const2 = lambda i: (0, 0)
    const3 = lambda i: (0, 0, 0)
    in_specs = [pl.BlockSpec((tm, D_MODEL), row),
                pl.BlockSpec((1, D_MODEL), const2),
                _resident((D_MODEL, RET_IN), const2),
                pl.BlockSpec((tm, half), lambda i: (i % n_pos, 0)),
                pl.BlockSpec((tm, half), lambda i: (i % n_pos, 0)),
                _resident((RET_HEADS, tm, LANES), const3),
                _resident((RET_HEADS, tm, LANES), const3)]
    out_specs = [pl.BlockSpec((tm, RET_QK_WIDTH), row),
                 pl.BlockSpec((tm, RET_QK_WIDTH), row),
                 pl.BlockSpec((tm, RET_WIDTH), row),
                 pl.BlockSpec((tm, RET_WIDTH), row)]
    out_shape = [jax.ShapeDtypeStruct((m, RET_QK_WIDTH), out_dtype),
                 jax.ShapeDtypeStruct((m, RET_QK_WIDTH), out_dtype),
                 jax.ShapeDtypeStruct((m, RET_WIDTH), out_dtype),
                 jax.ShapeDtypeStruct((m, RET_WIDTH), out_dtype)]
    args = [x, g, w, jnp.asarray(cos), jnp.asarray(sin), jnp.asarray(q_scale), jnp.asarray(k_scale)]
    scratch_shapes = []
    if sample is not None:
        q_s, k_s, v_s, gate_s, state = sample
        n = q_s.shape[0]
        assert n == SAMPLE_RET_BATCH * (m // tm), (n, m, tm)
        st_spec = pl.BlockSpec((SAMPLE_RET_BATCH, RET_HEADS, RET_QK_DIM, RET_V_DIM),
                               lambda i: (i, 0, 0, 0))
        in_specs += [_resident((n, RET_QK_WIDTH), const2), _resident((n, RET_QK_WIDTH), const2),
                     _resident((n, RET_WIDTH), const2), _resident((n, RET_WIDTH), const2), st_spec]
        out_specs += [pl.BlockSpec((n, RET_WIDTH), const2), st_spec]
        out_shape += [jax.ShapeDtypeStruct((n, RET_WIDTH), F32),
                      jax.ShapeDtypeStruct(state.shape, F32)]
        args += [q_s, k_s, v_s, gate_s, state]
        scratch_shapes = [pltpu.VMEM((RET_QK_WIDTH, n), F32), pltpu.VMEM((RET_QK_WIDTH, n), F32)]
    return pl.pallas_call(
        functools.partial(_ret_inproj_kernel, with_sample=sample is not None),
        grid=(m // tm,),
        in_specs=in_specs,
        out_specs=out_specs,
        out_shape=out_shape,
        scratch_shapes=scratch_shapes,
        compiler_params=_params(1),
        name="ret_inproj",
    )(*args)


def _outproj_kernel(o_ref, w_ref, x_ref, y_ref):
    y_ref[...] = x_ref[...] + jnp.dot(o_ref[...].astype(BF16), w_ref[...].astype(BF16),
                                      preferred_element_type=F32)


def _outproj_norm_kernel(o_ref, w_ref, x_ref, g_ref, y_ref):
    y = x_ref[...] + jnp.dot(o_ref[...].astype(BF16), w_ref[...].astype(BF16),
                             preferred_element_type=F32)
    y_ref[...] = _rmsnorm(y, g_ref[...])


def _outproj(o, w, x, final_g=None):
    m, width = o.shape
    tm = min(OUT_ROWS, m)
    row = lambda i: (i, 0)
    in_specs = [pl.BlockSpec((tm, width), row),
                _resident((width, D_MODEL), lambda i: (0, 0)),
                pl.BlockSpec((tm, D_MODEL), row)]
    args = [o, w, x]
    body = _outproj_kernel
    if final_g is not None:
        in_specs.append(pl.BlockSpec((1, D_MODEL), lambda i: (0, 0)))
        args.append(final_g)
        body = _outproj_norm_kernel
    return pl.pallas_call(
        body,
        grid=(m // tm,),
        in_specs=in_specs,
        out_specs=pl.BlockSpec((tm, D_MODEL), row),
        out_shape=jax.ShapeDtypeStruct((m, D_MODEL), F32),
        compiler_params=_params(1),
        name="outproj",
    )(*args)


def _swa_prompt_kernel(sink_ref, q_ref, kvc_ref, kvp_ref, gate_ref, bias_ref, out_ref):
    j = pl.program_id(1)
    half = ATT_HEAD_DIM
    pairs_per_kv = ATT_GROUP // 2
    kv_group_cols = ATT_GROUP * ATT_HEAD_DIM
    contract_last = (((1,), (1,)), ((), ()))
    lo1 = lax.broadcasted_iota(jnp.int32, (WINDOW, LANES), 1) < half
    own = (lax.broadcasted_iota(jnp.int32, (WINDOW, WINDOW), 1)
           <= lax.broadcasted_iota(jnp.int32, (WINDOW, WINDOW), 0))
    units =[(blk, kvh) for blk in range(ATT_ROWS // WINDOW) for kvh in range(ATT_KV_HEADS)]

    def keys_values(blk, kvh):
        r0 = blk * WINDOW
        out = []
        for variant in range(PADDED_KV_VARIANTS):
            c = (kvh * PADDED_KV_VARIANTS + variant) * LANES
            prev = kvp_ref[:, c:c + LANES] if blk == 0 else kvc_ref[r0 - WINDOW:r0, c:c + LANES]
            out.append(jnp.concatenate([prev, kvc_ref[r0:r0 + WINDOW, c:c + LANES]], axis=0))
        return out

    def scores(blk, kvh):
        rows = slice(blk * WINDOW, (blk + 1) * WINDOW)
        c0 = kvh * kv_group_cols
        k_a, k_b, v_a, v_b = keys_values(blk, kvh)
        qs = jnp.concatenate(
            [q_ref[rows, c0 + p * LANES:c0 + (p + 1) * LANES] for p in range(pairs_per_kv)],
            axis=0)
        logits = (lax.dot_general(qs, k_a, contract_last, preferred_element_type=F32),
                  lax.dot_general(qs, k_b, contract_last, preferred_element_type=F32))
        return logits, v_a, v_b

    def softmax(blk, kvh, logits):
        sel = jnp.where(j == 0, 0, 1) if blk == 0 else 1
        probs = ([], [])
        inv = []
        for p in range(pairs_per_kv):
            top, total, sink = [], [], []
            for st in range(2):
                hd = kvh * ATT_GROUP + 2 * p + st
                lg = logits[st][p * WINDOW:(p + 1) * WINDOW, :]
                l = jnp.where(own, lg[:, WINDOW:], lg[:, :WINDOW]) + bias_ref[sel, hd]
                s = sink_ref[hd]
                m = jnp.maximum(jnp.max(l, axis=-1, keepdims=True), s)
                pe = jnp.exp2(l - m)
                probs[st].append(jnp.concatenate(
                    [jnp.where(own, 0.0, pe).astype(BF16),
                     jnp.where(own, pe, 0.0).astype(BF16)], axis=1))
                top.append(m)
                total.append(jnp.sum(pe, axis=-1, keepdims=True))
                sink.append(s)
            sink_term = jnp.exp2(jnp.where(lo1, sink[0], sink[1]) - jnp.where(lo1, top[0], top[1]))
            inv.append(1.0 / (jnp.where(lo1, total[0], total[1]) + sink_term))
        return jnp.concatenate(probs[0], axis=0), jnp.concatenate(probs[1], axis=0), inv

    def output(blk, kvh, p_a, p_b, inv, v_a, v_b):
        rows = slice(blk * WINDOW, (blk + 1) * WINDOW)
        c0 = kvh * kv_group_cols
        o = (jnp.dot(p_a, v_a, preferred_element_type=F32)
             + jnp.dot(p_b, v_b, preferred_element_type=F32))
        for p in range(pairs_per_kv):
            cols = slice(c0 + p * LANES, c0 + (p + 1) * LANES)
            gt = gate_ref[rows, cols].astype(F32)
            og = o[p * WINDOW:(p + 1) * WINDOW, :] * inv[p] * _silu(gt)
            out_ref[rows, cols] = og.astype(out_ref.dtype)

    for unit in units:
        logits, v_a, v_b = scores(*unit)
        p_a, p_b, inv = softmax(*unit, logits)
        output(*unit, p_a, p_b, inv, v_a, v_b)


def _swa_prompt(sinks, q, kvpad, gate, bias, batch, seq):
    m = batch * seq
    nt = seq // ATT_ROWS
    blocks_per_tile = ATT_ROWS // WINDOW
    blocks_per_seq = seq // WINDOW
    tile = lambda b, j: (b * nt + j, 0)
    prev = lambda b, j: (b * blocks_per_seq + jnp.maximum(j * blocks_per_tile - 1, 0), 0)
    return pl.pallas_call(
        _swa_prompt_kernel,
        grid=(batch, nt),
        in_specs=[pl.BlockSpec(memory_space=pltpu.SMEM),
                  pl.BlockSpec((ATT_ROWS, ATT_WIDTH), tile),
                  pl.BlockSpec((ATT_ROWS, PADDED_KV_WIDTH), tile),
                  pl.BlockSpec((WINDOW, PADDED_KV_WIDTH), prev),
                  pl.BlockSpec((ATT_ROWS, ATT_WIDTH), tile),
                  pl.BlockSpec((2, ATT_HEADS, WINDOW, WINDOW), lambda b, j: (0, 0, 0, 0))],
        out_specs=pl.BlockSpec((ATT_ROWS, ATT_WIDTH), tile),
        out_shape=jax.ShapeDtypeStruct((m, ATT_WIDTH), BF16),
        compiler_params=_params(2),
        name="swa_prompt",
    )(sinks, q, kvpad, kvpad, gate, bias)


def _sample_head_order():
    order = []
    for pair in range(ATT_KV_HEADS // 2):
        base = pair * 2 * ATT_GROUP
        for kv in range(2):
            for half in range(2):
                order += [base + kv * ATT_GROUP + 2 * g + half for g in range(ATT_GROUP // 2)]
    return np.asarray(order, np.int32)


def _swa_sample_pieces(step, q_ref, kvt_ref, ckt_ref, cvt_ref, gate_ref, bias_ref, sink_ref,
                       o_ref, cko_ref, cvo_ref):
    half = ATT_HEAD_DIM
    quarter = ATT_GROUP // 2
    contract_last = (((1,), (1,)), ((), ()))
    newest = lax.broadcasted_iota(jnp.int32, (ATT_KV_WIDTH, WINDOW), 1) == WINDOW - 1
    seq_lane = lax.broadcasted_iota(jnp.int32, (2 * ATT_KV_WIDTH, LANES), 1)
    lo4 = lax.broadcasted_iota(jnp.int32, (quarter, LANES), 1) < half
    new_cache = {}

    def update(i):
        b = step * SAMPLE_ATT_BATCH + i
        new_col = jnp.sum(jnp.where(seq_lane == b, kvt_ref[...], 0.0), axis=1, keepdims=True)
        kb = jnp.where(newest, new_col[:ATT_KV_WIDTH], pltpu.roll(ckt_ref[i], WINDOW - 1, axis=1))
        vb = jnp.where(newest, new_col[ATT_KV_WIDTH:], pltpu.roll(cvt_ref[i], WINDOW - 1, axis=1))
        cko_ref[i] = kb
        cvo_ref[i] = vb
        new_cache[i] = (kb, vb)

    def attend(i, pair):
        b = step * SAMPLE_ATT_BATCH + i
        kb, vb = new_cache[i]
        groups = slice(pair * ATT_GROUP, (pair + 1) * ATT_GROUP)
        hs = slice(pair * 2 * ATT_GROUP, (pair + 1) * 2 * ATT_GROUP)
        feat = slice(pair * LANES, (pair + 1) * LANES)
        g8 = q_ref[b, groups, :]
        g8r = pltpu.roll(g8, half, axis=1)
        qbd = jnp.concatenate(
            [jnp.where(lo4, g8[:quarter], 0.0), jnp.where(lo4, g8r[:quarter], 0.0),
             jnp.where(lo4, 0.0, g8r[quarter:]), jnp.where(lo4, 0.0, g8[quarter:])],
            axis=0)
        l = jnp.dot(qbd, kb[feat, :], preferred_element_type=F32) + bias_ref[hs, :]
        s = sink_ref[hs, 0:1]
        m = jnp.maximum(jnp.max(l, axis=-1, keepdims=True), s)
        pe = jnp.exp2(l - m)
        den = jnp.sum(pe, axis=-1, keepdims=True) + jnp.exp2(s - m)
        o2 = lax.dot_general(pe, vb[feat, :], contract_last,
                             preferred_element_type=F32) / den
        o2r = pltpu.roll(o2, half, axis=1)
        og = jnp.concatenate(
            [jnp.where(lo4, o2[:quarter], o2r[quarter:2 * quarter]),
             jnp.where(lo4, o2r[2 * quarter:3 * quarter], o2[3 * quarter:])], axis=0)
        o_ref[b, groups, :] = og * _silu(gate_ref[b, groups, :])

    pieces = []
    for i in range(SAMPLE_ATT_BATCH):
        pieces.append(functools.partial(update, i))
        pieces += [functools.partial(attend, i, pair) for pair in range(ATT_KV_HEADS // 2)]
    return pieces


def _group_norm_gate(o, gate):
    mu = jnp.mean(o, axis=-1, keepdims=True)
    d = o - mu
    var = jnp.mean(d * d, axis=-1, keepdims=True)
    return d * lax.rsqrt(var + EPS) * _silu(gate)


def _ret_prompt_kernel(g_block_ref, q_ref, k_ref, v_ref, gate_ref, x_ref, wout_ref, gf_ref,
                       y_ref, r_ref, og_ref):
    c = pl.program_id(1)

    @pl.when(c == 0)
    def _():
        r_ref[...] = jnp.zeros_like(r_ref)

    causal = (lax.broadcasted_iota(jnp.int32, (RET_BLOCK, RET_BLOCK), 1)
              <= lax.broadcasted_iota(jnp.int32, (RET_BLOCK, RET_BLOCK), 0))
    contract_last = (((1,), (1,)), ((), ()))
    n_blocks = RET_ROWS // RET_BLOCK
    for blk in range(n_blocks):
        rows = slice(blk * RET_BLOCK, (blk + 1) * RET_BLOCK)
        for h in range(RET_HEADS):
            qcols = slice(h * RET_QK_DIM, (h + 1) * RET_QK_DIM)
            vcols = slice(h * RET_V_DIM, (h + 1) * RET_V_DIM)
            qh = q_ref[rows, qcols]
            kh = k_ref[rows, qcols]
            vh = v_ref[rows, vcols]
            state = r_ref[0, h]
            scores = jnp.where(
                causal, lax.dot_general(qh, kh, contract_last, preferred_element_type=F32), 0.0)
            o = (jnp.dot(scores.astype(BF16), vh, preferred_element_type=F32)
                 + jnp.dot(qh, state.astype(BF16), preferred_element_type=F32))
            r_ref[0, h] = g_block_ref[h] * (
                state + jnp.dot(kh.T, vh, preferred_element_type=F32))
            og_ref[h, rows, :] = _group_norm_gate(
                o, gate_ref[rows, vcols].astype(F32)).astype(og_ref.dtype)
            if blk == n_blocks - 1:
                part = jnp.dot(og_ref[h], wout_ref[vcols, :].astype(BF16),
                               preferred_element_type=F32)
                if h == 0:
                    y_ref[...] = x_ref[...] + part
                elif h < RET_HEADS - 1:
                    y_ref[...] += part
                else:
                    y_ref[...] = _rmsnorm(y_ref[...] + part, gf_ref[...])


def _ret_prompt(q, k, v, gate, x, w_out, final_g, batch, seq):
    m = batch * seq
    nt = seq // RET_ROWS
    g_block = jnp.asarray(_decay_scales_np(RET_BLOCK, RET_BLOCK)[2])
    rows = lambda b, c: (b * nt + c, 0)
    return pl.pallas_call(
        _ret_prompt_kernel,
        grid=(batch, nt),
        in_specs=[pl.BlockSpec(memory_space=pltpu.SMEM),
                  pl.BlockSpec((RET_ROWS, RET_QK_WIDTH), rows),
                  pl.BlockSpec((RET_ROWS, RET_QK_WIDTH), rows),
                  pl.BlockSpec((RET_ROWS, RET_WIDTH), rows),
                  pl.BlockSpec((RET_ROWS, RET_WIDTH), rows),
                  pl.BlockSpec((RET_ROWS, D_MODEL), rows),
                  _resident((RET_WIDTH, D_MODEL), lambda b, c: (0, 0)),
                  pl.BlockSpec((1, D_MODEL), lambda b, c: (0, 0))],
        out_specs=[pl.BlockSpec((RET_ROWS, D_MODEL), rows),
                   pl.BlockSpec((1, RET_HEADS, RET_QK_DIM, RET_V_DIM), lambda b, c: (b, 0, 0, 0))],
        out_shape=[jax.ShapeDtypeStruct((m, D_MODEL), F32),
                   jax.ShapeDtypeStruct((batch, RET_HEADS, RET_QK_DIM, RET_V_DIM), F32)],
        scratch_shapes=[pltpu.VMEM((RET_HEADS, RET_ROWS, RET_V_DIM), BF16)],
        compiler_params=_params(2),
        name="ret_prompt",
    )(g_block, q, k, v, gate, x, w_out, final_g)


def kernel(x_prompt, x_sample, cache_swa_k, cache_swa_v, state_ret, norm_g, final_norm_g, rel_bias,
           w_in_attn, attn_sinks, w_out_attn, w_in_ret, w_out_ret):
    batch, seq, _ = x_prompt.shape
    n_s = x_sample.shape[0]
    xp = x_prompt.reshape(batch * seq, D_MODEL)
    xs = x_sample.reshape(n_s, D_MODEL)
    g0 = norm_g[0].reshape(1, D_MODEL)
    g1 = norm_g[1].reshape(1, D_MODEL)
    gf = final_norm_g.reshape(1, D_MODEL)
    w_in_a = w_in_attn[0].astype(BF16)
    w_out_a = w_out_attn[0]
    w_in_r = w_in_ret[0].astype(BF16)
    w_out_r = w_out_ret[0]
    sinks = attn_sinks[0] * LOG2E

    bias = _bias_table(rel_bias)

    qs, kvs, gates = _attn_inproj(xs, g0, w_in_a, F32)
    order = _sample_head_order()
    bias_s = bias[1, :, WINDOW - 1, :][order]
    sinks_rep = jnp.broadcast_to(sinks[order][:, None], (ATT_HEADS, LANES))
    to_feature_major = lambda c: jnp.transpose(c, (0, 2, 3, 1)).reshape(n_s, ATT_KV_WIDTH, WINDOW)
    to_window_major = lambda c: jnp.transpose(
        c.reshape(n_s, ATT_KV_HEADS, ATT_HEAD_DIM, WINDOW), (0, 3, 1, 2))[None]
    n_groups = ATT_WIDTH // LANES
    q, kv, gate, kvpad, ogs, ckt_new, cvt_new = _attn_inproj(
        xp, g0, w_in_a, BF16,
        sample=(qs.reshape(n_s, n_groups, LANES), kvs,
                to_feature_major(cache_swa_k[0]), to_feature_major(cache_swa_v[0]),
                gates.reshape(n_s, n_groups, LANES), bias_s, sinks_rep))
    og = _swa_prompt(sinks, q, kvpad, gate, bias, batch, seq)
    xp1 = _outproj(og, w_out_a, xp)
    kv_last = kv.reshape(batch, seq, 2 * ATT_KV_WIDTH)[:, seq - WINDOW:]
    k_prompt = kv_last[..., :ATT_KV_WIDTH].reshape(1, batch, WINDOW, ATT_KV_HEADS, ATT_HEAD_DIM)
    v_prompt = kv_last[..., ATT_KV_WIDTH:].reshape(1, batch, WINDOW, ATT_KV_HEADS, ATT_HEAD_DIM)
    xs1 = _outproj(ogs.reshape(n_s, ATT_WIDTH), w_out_a, xs)
    k_sample = to_window_major(ckt_new)
    v_sample = to_window_major(cvt_new)

    q_s, k_s, v_s, gate_s = _ret_inproj(xs1, g1, w_in_r, np.full((n_s,), PAST_LEN), 1, F32)
    q, k, v, gate, og_s, r_sample = _ret_inproj(
        xp1, g1, w_in_r, np.arange(seq), RET_BLOCK, BF16,
        sample=(q_s, k_s, v_s, gate_s, state_ret[0]))
    y_prompt, r_prompt = _ret_prompt(q, k, v, gate, xp1, w_out_r, gf, batch, seq)
    y_prompt = y_prompt.reshape(batch, seq, D_MODEL)
    y_sample = _outproj(og_s, w_out_r, xs1, gf).reshape(n_s, 1, D_MODEL)

    return (y_prompt, y_sample, k_prompt, v_prompt, r_prompt[None],
            k_sample, v_sample, r_sample[None])
```

```python
import functools
import math

import numpy as np
import jax
import jax.numpy as jnp
from jax import lax
from jax.experimental import pallas as pl
from jax.experimental.pallas import tpu as pltpu

F32 = jnp.float32
BF16 = jnp.bfloat16

D_MODEL = 1024
PAST_LEN = 16384
ATT_HEADS = 32
ATT_KV_HEADS = 4
ATT_GROUP = ATT_HEADS // ATT_KV_HEADS
ATT_HEAD_DIM = 64
ATT_WIDTH = ATT_HEADS * ATT_HEAD_DIM
ATT_KV_WIDTH = ATT_KV_HEADS * ATT_HEAD_DIM
ATT_IN = 2 * ATT_WIDTH + 2 * ATT_KV_WIDTH
WINDOW = 128
NUM_BUCKETS = 32
MAX_DISTANCE = 128
RET_HEADS = 4
RET_QK_DIM = 256
RET_V_DIM = 512
RET_QK_WIDTH = RET_HEADS * RET_QK_DIM
RET_WIDTH = RET_HEADS * RET_V_DIM
RET_IN = 2 * RET_QK_WIDTH + 2 * RET_WIDTH
ROPE_BASE = 10000.0
EPS = 1e-6
NEG = -1e30
LOG2E = math.log2(math.e)

LANES = 128
PADDED_KV_VARIANTS = 4
PADDED_KV_WIDTH = ATT_KV_HEADS * PADDED_KV_VARIANTS * LANES
PROJ_ROWS = 512
ATT_PROJ_ROWS = 1024
PROJ_COLS = 512
OUT_ROWS = 1024
ATT_ROWS = 1024
RET_BLOCK = 256
RET_ROWS = 1024
SAMPLE_ATT_BATCH = 4
SAMPLE_RET_BATCH = 2
VMEM_LIMIT = 60 * 1024 * 1024


def _params(n_axes):
    return pltpu.CompilerParams(
        dimension_semantics=("arbitrary",) * n_axes,
        vmem_limit_bytes=VMEM_LIMIT)


def _resident(shape, index_map):
    return pl.BlockSpec(shape, index_map, pipeline_mode=pl.Buffered(1))


def _silu(x):
    hx = 0.5 * x
    return hx + hx * jnp.tanh(hx)


def _rmsnorm(x, g):
    return x * lax.rsqrt(jnp.mean(x * x, axis=-1, keepdims=True) + EPS) * g


def _t5_bucket_np(rel):
    n = np.maximum(rel, 0)
    max_exact = NUM_BUCKETS // 2
    nf = np.maximum(n, 1).astype(np.float64)
    large = max_exact + (np.log(nf / max_exact) / math.log(MAX_DISTANCE / max_exact)
                         * (NUM_BUCKETS - max_exact)).astype(np.int32)
    large = np.minimum(large, NUM_BUCKETS - 1)
    return np.where(n < max_exact, n, large).astype(np.int32)


def _bucket_map_np():
    i = np.arange(WINDOW)[:, None]
    j = np.arange(WINDOW)[None, :]
    rel = np.where(j <= i, i - j, i - j + WINDOW)
    return _t5_bucket_np(rel).astype(np.int32)


def _gammas():
    return [1.0 - 2.0 ** (-5 - h) for h in range(RET_HEADS)]


def _decay_scales_np(rows, block):
    lg = np.log1p(-np.exp2(-5.0 - np.arange(RET_HEADS, dtype=np.float64)))
    i1 = (np.arange(rows) % block + 1.0)[None, :] * lg[:, None]
    q_scale = np.repeat(np.exp(i1)[:, :, None], LANES, axis=2)
    k_scale = np.repeat((np.exp(-i1) * RET_QK_DIM ** -0.5)[:, :, None], LANES, axis=2)
    return (q_scale.astype(np.float32), k_scale.astype(np.float32),
            np.exp(block * lg).astype(np.float32))


def _rope_tables_np(pos):
    half = RET_QK_DIM // 2
    inv = ROPE_BASE ** (-np.arange(half, dtype=np.float64) / half)
    ang = np.asarray(pos, np.float64)[:, None] * inv[None, :]
    return np.cos(ang).astype(np.float32), np.sin(ang).astype(np.float32)


def _bias_table_kernel(rb_ref, bucket_ref, out_ref):
    bk = bucket_ref[...]
    previous_block = (lax.broadcasted_iota(jnp.int32, bk.shape, 1)
                      > lax.broadcasted_iota(jnp.int32, bk.shape, 0))

    def one_head(h, carry):
        acc = jnp.zeros(bk.shape, F32)
        for b in range(NUM_BUCKETS):
            acc = jnp.where(bk == b, rb_ref[b, h] * LOG2E, acc)
        out_ref[1, h] = acc
        out_ref[0, h] = jnp.where(previous_block, NEG, acc)
        return carry

    lax.fori_loop(0, ATT_HEADS, one_head, 0)


def _bias_table(rel_bias):
    bucket = jnp.asarray(_bucket_map_np())
    return pl.pallas_call(
        _bias_table_kernel,
        grid=(1,),
        in_specs=[pl.BlockSpec(memory_space=pltpu.SMEM),
                  pl.BlockSpec((WINDOW, WINDOW), lambda i: (0, 0))],
        out_specs=pl.BlockSpec((2, ATT_HEADS, WINDOW, WINDOW), lambda i: (0, 0, 0, 0)),
        out_shape=jax.ShapeDtypeStruct((2, ATT_HEADS, WINDOW, WINDOW), F32),
        compiler_params=_params(1),
        name="bias_table",
    )(rel_bias, bucket)


def _pad_pair(x, x_rolled, lo, e):
    if e == 0:
        return jnp.where(lo, x, 0.0), jnp.where(lo, 0.0, x_rolled)
    return jnp.where(lo, x_rolled, 0.0), jnp.where(lo, 0.0, x)


def _attn_inproj_kernel(*refs, hosts_sample):
    x_ref, g_ref, w_ref = refs[:3]
    side_work = []
    kvpad_ref = None
    if hosts_sample:
        qs_ref, kvs_ref, ckt_ref, cvt_ref, gates_ref, bias_ref, sink_ref = refs[3:10]
        q_ref, kv_ref, gate_ref, kvpad_ref, ogs_ref, cko_ref, cvo_ref, kvt_ref = refs[10:]
        step = pl.program_id(0)

        @pl.when(step == 0)
        def _():
            kvt_ref[...] = kvs_ref[...].T

        side_work = _swa_sample_pieces(step, qs_ref, kvt_ref, ckt_ref, cvt_ref, gates_ref,
                                       bias_ref, sink_ref, ogs_ref, cko_ref, cvo_ref)
    else:
        q_ref, kv_ref, gate_ref, wcast_ref = refs[3:]

    h = _rmsnorm(x_ref[...], g_ref[...]).astype(BF16)
    for c in range(0, ATT_IN, PROJ_COLS):
        if side_work:
            side_work.pop(0)()
        wc = w_ref[:, c:c + PROJ_COLS]
        if not hosts_sample:
            wc = wc.astype(BF16)
            wcast_ref[:, c:c + PROJ_COLS] = wc
        r = jnp.dot(h, wc, preferred_element_type=F32)
        if c < ATT_WIDTH:
            q_ref[:, c:c + PROJ_COLS] = (r * (ATT_HEAD_DIM ** -0.5 * LOG2E)).astype(q_ref.dtype)
        elif c < ATT_WIDTH + 2 * ATT_KV_WIDTH:
            kv_ref[...] = r
            if kvpad_ref is not None:
                lo = lax.broadcasted_iota(jnp.int32, (r.shape[0], LANES), 1) < ATT_HEAD_DIM
                for pair in range(ATT_KV_HEADS // 2):
                    for is_v in range(2):
                        c1 = is_v * ATT_KV_WIDTH + pair * LANES
                        x2 = r[:, c1:c1 + LANES]
                        x2r = pltpu.roll(x2, ATT_HEAD_DIM, axis=1)
                        for e in range(2):
                            g0 = ((2 * pair + e) * PADDED_KV_VARIANTS + 2 * is_v) * LANES
                            lo_half, hi_half = _pad_pair(x2, x2r, lo, e)
                            kvpad_ref[:, g0:g0 + LANES] = lo_half.astype(kvpad_ref.dtype)
                            kvpad_ref[:, g0 + LANES:g0 + 2 * LANES] = hi_half.astype(kvpad_ref.dtype)
        else:
            o = c - ATT_WIDTH - 2 * ATT_KV_WIDTH
            gate_ref[:, o:o + PROJ_COLS] = r.astype(gate_ref.dtype)
    while side_work:
        side_work.pop(0)()


def _attn_inproj(x, g, w, out_dtype, sample=None):
    m = x.shape[0]
    tm = min(ATT_PROJ_ROWS, m)
    row = lambda i: (i, 0)
    const2 = lambda i: (0, 0)
    const3 = lambda i: (0, 0, 0)
    in_specs = [pl.BlockSpec((tm, D_MODEL), row),
                pl.BlockSpec((1, D_MODEL), const2),
                _resident((D_MODEL, ATT_IN), const2)]
    out_specs = [pl.BlockSpec((tm, ATT_WIDTH), row),
                 pl.BlockSpec((tm, 2 * ATT_KV_WIDTH), row),
                 pl.BlockSpec((tm, ATT_WIDTH), row)]
    out_shape = [jax.ShapeDtypeStruct((m, ATT_WIDTH), out_dtype),
                 jax.ShapeDtypeStruct((m, 2 * ATT_KV_WIDTH), F32),
                 jax.ShapeDtypeStruct((m, ATT_WIDTH), out_dtype)]
    args = [x, g, w]
    scratch_shapes = []
    if sample is not None:
        q_s, kv_s, ckt, cvt, gate_s, bias_s, sinks_rep = sample
        n, n_groups, _ = q_s.shape
        assert n == SAMPLE_ATT_BATCH * (m // tm), (n, m, tm)
        cache_spec = pl.BlockSpec((SAMPLE_ATT_BATCH, ATT_KV_WIDTH, WINDOW), lambda i: (i, 0, 0))
        in_specs += [_resident((n, n_groups, LANES), const3),
                     _resident((n, 2 * ATT_KV_WIDTH), const2),
                     cache_spec, cache_spec,
                     _resident((n, n_groups, LANES), const3),
                     _resident((ATT_HEADS, WINDOW), const2),
                     _resident((ATT_HEADS, LANES), const2)]
        out_specs += [pl.BlockSpec((tm, PADDED_KV_WIDTH), row),
                      pl.BlockSpec((n, n_groups, LANES), const3), cache_spec, cache_spec]
        out_shape += [jax.ShapeDtypeStruct((m, PADDED_KV_WIDTH), BF16),
                      jax.ShapeDtypeStruct((n, n_groups, LANES), F32),
                      jax.ShapeDtypeStruct((n, ATT_KV_WIDTH, WINDOW), F32),
                      jax.ShapeDtypeStruct((n, ATT_KV_WIDTH, WINDOW), F32)]
        args += list(sample)
        scratch_shapes = [pltpu.VMEM((2 * ATT_KV_WIDTH, n), F32)]
    else:
        assert m == tm, (m, tm)
        out_specs.append(pl.BlockSpec((D_MODEL, ATT_IN), const2))
        out_shape.append(jax.ShapeDtypeStruct((D_MODEL, ATT_IN), BF16))
    return pl.pallas_call(
        functools.partial(_attn_inproj_kernel, hosts_sample=sample is not None),
        grid=(m // tm,),
        in_specs=in_specs,
        out_specs=out_specs,
        out_shape=out_shape,
        scratch_shapes=scratch_shapes,
        compiler_params=_params(1),
        name="attn_inproj",
    )(*args)


def _ret_sample_pieces(step, qt_ref, kt_ref, v_ref, gate_ref, st_ref, og_ref, sto_ref):
    lane = lax.broadcasted_iota(jnp.int32, (RET_QK_DIM, LANES), 1)
    gammas = _gammas()

    def piece(i, h):
        b = step * SAMPLE_RET_BATCH + i
        pick = lane == b
        rows = slice(h * RET_QK_DIM, (h + 1) * RET_QK_DIM)
        qc = jnp.sum(jnp.where(pick, qt_ref[rows, :], 0.0), axis=1, keepdims=True)
        kc = jnp.sum(jnp.where(pick, kt_ref[rows, :], 0.0), axis=1, keepdims=True)
        vcols = slice(h * RET_V_DIM, (h + 1) * RET_V_DIM)
        vrow = v_ref[pl.ds(b, 1), vcols]
        state = st_ref[i, h]
        o = (jnp.sum(qc * state, axis=0, keepdims=True)
             + jnp.sum(qc * kc, axis=0, keepdims=True) * vrow)
        sto_ref[i, h] = gammas[h] * (state + kc * vrow)
        og_ref[pl.ds(b, 1), vcols] = _group_norm_gate(o, gate_ref[pl.ds(b, 1), vcols])

    return [functools.partial(piece, i, h)
            for i in range(SAMPLE_RET_BATCH) for h in range(RET_HEADS)]


def _ret_inproj_kernel(*refs, with_sample):
    x_ref, g_ref, w_ref, cos_ref, sin_ref, qsc_ref, ksc_ref = refs[:7]
    side_work = []
    if with_sample:
        qs_ref, ks_ref, vs_ref, gates_ref, st_ref = refs[7:12]
        q_ref, k_ref, v_ref, gate_ref, ogs_ref, sto_ref, qts_ref, kts_ref = refs[12:]
        step = pl.program_id(0)

        @pl.when(step == 0)
        def _():
            qts_ref[...] = qs_ref[...].T
            kts_ref[...] = ks_ref[...].T

        side_work = _ret_sample_pieces(step, qts_ref, kts_ref, vs_ref, gates_ref, st_ref,
                                       ogs_ref, sto_ref)
    else:
        q_ref, k_ref, v_ref, gate_ref, wcast_ref = refs[7:]

    def interleave():
        if side_work:
            side_work.pop(0)()

    def weights(c):
        wc = w_ref[:, c:c + PROJ_COLS]
        if not with_sample:
            wc = wc.astype(BF16)
            wcast_ref[:, c:c + PROJ_COLS] = wc
        return wc

    h = _rmsnorm(x_ref[...], g_ref[...]).astype(BF16)
    half = RET_QK_DIM // 2
    cos = cos_ref[...]
    sin = sin_ref[...]
    for out_ref, scale_ref, w0 in ((q_ref, qsc_ref, 0), (k_ref, ksc_ref, RET_QK_WIDTH)):
        for c in range(0, RET_QK_WIDTH, PROJ_COLS):
            interleave()
            r = jnp.dot(h, weights(w0 + c), preferred_element_type=F32)
            for o in range(0, PROJ_COLS, RET_QK_DIM):
                scale = scale_ref[(c + o) // RET_QK_DIM]
                x1 = r[:, o:o + half]
                x2 = r[:, o + half:o + RET_QK_DIM]
                out_ref[:, c + o:c + o + half] = (
                    (x1 * cos - x2 * sin) * scale).astype(out_ref.dtype)
                out_ref[:, c + o + half:c + o + RET_QK_DIM] = (
                    (x1 * sin + x2 * cos) * scale).astype(out_ref.dtype)
    for c in range(0, RET_WIDTH, PROJ_COLS):
        interleave()
        v_ref[:, c:c + PROJ_COLS] = jnp.dot(
            h, weights(2 * RET_QK_WIDTH + c), preferred_element_type=F32).astype(v_ref.dtype)
    for c in range(0, RET_WIDTH, PROJ_COLS):
        interleave()
        gate_ref[:, c:c + PROJ_COLS] = jnp.dot(
            h, weights(2 * RET_QK_WIDTH + RET_WIDTH + c),
            preferred_element_type=F32).astype(gate_ref.dtype)
    while side_work:
        interleave()


def _ret_inproj(x, g, w, pos, block, out_dtype, sample=None):
    m = x.shape[0]
    tm = min(PROJ_ROWS, m)
    n_pos = len(pos) // tm
    half = RET_QK_DIM // 2
    cos, sin = _rope_tables_np(pos)
    q_scale, k_scale, _ = _decay_scales_np(tm, block)
    row = lambda i: (i, 0)
    const2 = lambda i: (0, 0)
    const3 = lambda i: (0, 0, 0)
    in_specs = [pl.BlockSpec((tm, D_MODEL), row),
                pl.BlockSpec((1, D_MODEL), const2),
                _resident((D_MODEL, RET_IN), const2),
                pl.BlockSpec((tm, half), lambda i: (i % n_pos, 0)),
                pl.BlockSpec((tm, half), lambda i: (i % n_pos, 0)),
                _resident((RET_HEADS, tm, LANES), const3),
                _resident((RET_HEADS, tm, LANES), const3)]
    out_specs = [pl.BlockSpec((tm, RET_QK_WIDTH), row),
                 pl.BlockSpec((tm, RET_QK_WIDTH), row),
                 pl.BlockSpec((tm, RET_WIDTH), row),
                 pl.BlockSpec((tm, RET_WIDTH), row)]
    out_shape = [jax.ShapeDtypeStruct((m, RET_QK_WIDTH), out_dtype),
                 jax.ShapeDtypeStruct((m, RET_QK_WIDTH), out_dtype),
                 jax.ShapeDtypeStruct((m, RET_WIDTH), out_dtype),
                 jax.ShapeDtypeStruct((m, RET_WIDTH), out_dtype)]
    args = [x, g, w, jnp.asarray(cos), jnp.asarray(sin), jnp.asarray(q_scale), jnp.asarray(k_scale)]
    scratch_shapes = []
    if sample is not None:
        q_s, k_s, v_s, gate_s, state = sample
        n = q_s.shape[0]
        assert n == SAMPLE_RET_BATCH * (m // tm), (n, m, tm)
        st_spec = pl.BlockSpec((SAMPLE_RET_BATCH, RET_HEADS, RET_QK_DIM, RET_V_DIM),
                               lambda i: (i, 0, 0, 0))
        in_specs += [_resident((n, RET_QK_WIDTH), const2), _resident((n, RET_QK_WIDTH), const2),
                     _resident((n, RET_WIDTH), const2), _resident((n, RET_WIDTH), const2), st_spec]
        out_specs += [pl.BlockSpec((n, RET_WIDTH), const2), st_spec]
        out_shape += [jax.ShapeDtypeStruct((n, RET_WIDTH), F32),
                      jax.ShapeDtypeStruct(state.shape, F32)]
        args += [q_s, k_s, v_s, gate_s, state]
        scratch_shapes = [pltpu.VMEM((RET_QK_WIDTH, n), F32), pltpu.VMEM((RET_QK_WIDTH, n), F32)]
    else:
        assert m == tm, (m, tm)
        out_specs.append(pl.BlockSpec((D_MODEL, RET_IN), const2))
        out_shape.append(jax.ShapeDtypeStruct((D_MODEL, RET_IN), BF16))
    return pl.pallas_call(
        functools.partial(_ret_inproj_kernel, with_sample=sample is not None),
        grid=(m // tm,),
        in_specs=in_specs,
        out_specs=out_specs,
        out_shape=out_shape,
        scratch_shapes=scratch_shapes,
        compiler_params=_params(1),
        name="ret_inproj",
    )(*args)


def _outproj_kernel(o_ref, w_ref, x_ref, y_ref):
    y_ref[...] = x_ref[...] + jnp.dot(o_ref[...].astype(BF16), w_ref[...].astype(BF16),
                                      preferred_element_type=F32)


def _outproj_norm_kernel(o_ref, w_ref, x_ref, g_ref, y_ref):
    y = x_ref[...] + jnp.dot(o_ref[...].astype(BF16), w_ref[...].astype(BF16),
                             preferred_element_type=F32)
    y_ref[...] = _rmsnorm(y, g_ref[...])


def _outproj(o, w, x, final_g=None):
    m, width = o.shape
    tm = min(OUT_ROWS, m)
    row = lambda i: (i, 0)
    in_specs = [pl.BlockSpec((tm, width), row),
                _resident((width, D_MODEL), lambda i: (0, 0)),
                pl.BlockSpec((tm, D_MODEL), row)]
    args = [o, w, x]
    body = _outproj_kernel
    if final_g is not None:
        in_specs.append(pl.BlockSpec((1, D_MODEL), lambda i: (0, 0)))
        args.append(final_g)
        body = _outproj_norm_kernel
    return pl.pallas_call(
        body,
        grid=(m // tm,),
        in_specs=in_specs,
        out_specs=pl.BlockSpec((tm, D_MODEL), row),
        out_shape=jax.ShapeDtypeStruct((m, D_MODEL), F32),
        compiler_params=_params(1),
        name="outproj",
    )(*args)


def _swa_prompt_kernel(sink_ref, q_ref, kvc_ref, kvp_ref, gate_ref, bias_ref, out_ref):
    j = pl.program_id(1)
    half = ATT_HEAD_DIM
    pairs_per_kv = ATT_GROUP // 2
    kv_group_cols = ATT_GROUP * ATT_HEAD_DIM
    contract_last = (((1,), (1,)), ((), ()))
    lo1 = lax.broadcasted_iota(jnp.int32, (WINDOW, LANES), 1) < half
    own = (lax.broadcasted_iota(jnp.int32, (WINDOW, WINDOW), 1)
           <= lax.broadcasted_iota(jnp.int32, (WINDOW, WINDOW), 0))
    units =[(blk, kvh) for blk in range(ATT_ROWS // WINDOW) for kvh in range(ATT_KV_HEADS)]

    def keys_values(blk, kvh):
        r0 = blk * WINDOW
        out = []
        for variant in range(PADDED_KV_VARIANTS):
            c = (kvh * PADDED_KV_VARIANTS + variant) * LANES
            prev = kvp_ref[:, c:c + LANES] if blk == 0 else kvc_ref[r0 - WINDOW:r0, c:c + LANES]
            out.append(jnp.concatenate([prev, kvc_ref[r0:r0 + WINDOW, c:c + LANES]], axis=0))
        return out

    def scores(blk, kvh):
        rows = slice(blk * WINDOW, (blk + 1) * WINDOW)
        c0 = kvh * kv_group_cols
        k_a, k_b, v_a, v_b = keys_values(blk, kvh)
        qs = jnp.concatenate(
            [q_ref[rows, c0 + p * LANES:c0 + (p + 1) * LANES] for p in range(pairs_per_kv)],
            axis=0)
        logits = (lax.dot_general(qs, k_a, contract_last, preferred_element_type=F32),
                  lax.dot_general(qs, k_b, contract_last, preferred_element_type=F32))
        return logits, v_a, v_b

    def softmax(blk, kvh, logits):
        sel = jnp.where(j == 0, 0, 1) if blk == 0 else 1
        probs = ([], [])
        inv = []
        for p in range(pairs_per_kv):
            top, total, sink = [], [], []
            for st in range(2):
                hd = kvh * ATT_GROUP + 2 * p + st
                lg = logits[st][p * WINDOW:(p + 1) * WINDOW, :]
                l = jnp.where(own, lg[:, WINDOW:], lg[:, :WINDOW]) + bias_ref[sel, hd]
                s = sink_ref[hd]
                m = jnp.maximum(jnp.max(l, axis=-1, keepdims=True), s)
                pe = jnp.exp2(l - m)
                probs[st].append(jnp.concatenate(
                    [jnp.where(own, 0.0, pe).astype(BF16),
                     jnp.where(own, pe, 0.0).astype(BF16)], axis=1))
                top.append(m)
                total.append(jnp.sum(pe, axis=-1, keepdims=True))
                sink.append(s)
            sink_term = jnp.exp2(jnp.where(lo1, sink[0], sink[1]) - jnp.where(lo1, top[0], top[1]))
            inv.append(1.0 / (jnp.where(lo1, total[0], total[1]) + sink_term))
        return jnp.concatenate(probs[0], axis=0), jnp.concatenate(probs[1], axis=0), inv

    def output(blk, kvh, p_a, p_b, inv, v_a, v_b):
        rows = slice(blk * WINDOW, (blk + 1) * WINDOW)
        c0 = kvh * kv_group_cols
        o = (jnp.dot(p_a, v_a, preferred_element_type=F32)
             + jnp.dot(p_b, v_b, preferred_element_type=F32))
        for p in range(pairs_per_kv):
            cols = slice(c0 + p * LANES, c0 + (p + 1) * LANES)
            gt = gate_ref[rows, cols].astype(F32)
            og = o[p * WINDOW:(p + 1) * WINDOW, :] * inv[p] * _silu(gt)
            out_ref[rows, cols] = og.astype(out_ref.dtype)

    for unit in units:
        logits, v_a, v_b = scores(*unit)
        p_a, p_b, inv = softmax(*unit, logits)
        output(*unit, p_a, p_b, inv, v_a, v_b)


def _swa_prompt(sinks, q, kvpad, gate, bias, batch, seq):
    m = batch * seq
    nt = seq // ATT_ROWS
    blocks_per_tile = ATT_ROWS // WINDOW
    blocks_per_seq = seq // WINDOW
    tile = lambda b, j: (b * nt + j, 0)
    prev = lambda b, j: (b * blocks_per_seq + jnp.maximum(j * blocks_per_tile - 1, 0), 0)
    return pl.pallas_call(
        _swa_prompt_kernel,
        grid=(batch, nt),
        in_specs=[pl.BlockSpec(memory_space=pltpu.SMEM),
                  pl.BlockSpec((ATT_ROWS, ATT_WIDTH), tile),
                  pl.BlockSpec((ATT_ROWS, PADDED_KV_WIDTH), tile),
                  pl.BlockSpec((WINDOW, PADDED_KV_WIDTH), prev),
                  pl.BlockSpec((ATT_ROWS, ATT_WIDTH), tile),
                  pl.BlockSpec((2, ATT_HEADS, WINDOW, WINDOW), lambda b, j: (0, 0, 0, 0))],
        out_specs=pl.BlockSpec((ATT_ROWS, ATT_WIDTH), tile),
        out_shape=jax.ShapeDtypeStruct((m, ATT_WIDTH), BF16),
        compiler_params=_params(2),
        name="swa_prompt",
    )(sinks, q, kvpad, kvpad, gate, bias)


def _sample_head_order():
    order = []
    for pair in range(ATT_KV_HEADS // 2):
        base = pair * 2 * ATT_GROUP
        for kv in range(2):
            for half in range(2):
                order += [base + kv * ATT_GROUP + 2 * g + half for g in range(ATT_GROUP // 2)]
    return np.asarray(order, np.int32)


def _swa_sample_pieces(step, q_ref, kvt_ref, ckt_ref, cvt_ref, gate_ref, bias_ref, sink_ref,
                       o_ref, cko_ref, cvo_ref):
    half = ATT_HEAD_DIM
    quarter = ATT_GROUP // 2
    contract_last = (((1,), (1,)), ((), ()))
    newest = lax.broadcasted_iota(jnp.int32, (ATT_KV_WIDTH, WINDOW), 1) == WINDOW - 1
    seq_lane = lax.broadcasted_iota(jnp.int32, (2 * ATT_KV_WIDTH, LANES), 1)
    lo4 = lax.broadcasted_iota(jnp.int32, (quarter, LANES), 1) < half
    new_cache = {}

    def update(i):
        b = step * SAMPLE_ATT_BATCH + i
        new_col = jnp.sum(jnp.where(seq_lane == b, kvt_ref[...], 0.0), axis=1, keepdims=True)
        kb = jnp.where(newest, new_col[:ATT_KV_WIDTH], pltpu.roll(ckt_ref[i], WINDOW - 1, axis=1))
        vb = jnp.where(newest, new_col[ATT_KV_WIDTH:], pltpu.roll(cvt_ref[i], WINDOW - 1, axis=1))
        cko_ref[i] = kb
        cvo_ref[i] = vb
        new_cache[i] = (kb, vb)

    def attend(i, pair):
        b = step * SAMPLE_ATT_BATCH + i
        kb, vb = new_cache[i]
        groups = slice(pair * ATT_GROUP, (pair + 1) * ATT_GROUP)
        hs = slice(pair * 2 * ATT_GROUP, (pair + 1) * 2 * ATT_GROUP)
        feat = slice(pair * LANES, (pair + 1) * LANES)
        g8 = q_ref[b, groups, :]
        g8r = pltpu.roll(g8, half, axis=1)
        qbd = jnp.concatenate(
            [jnp.where(lo4, g8[:quarter], 0.0), jnp.where(lo4, g8r[:quarter], 0.0),
             jnp.where(lo4, 0.0, g8r[quarter:]), jnp.where(lo4, 0.0, g8[quarter:])],
            axis=0)
        l = jnp.dot(qbd, kb[feat, :], preferred_element_type=F32) + bias_ref[hs, :]
        s = sink_ref[hs, 0:1]
        m = jnp.maximum(jnp.max(l, axis=-1, keepdims=True), s)
        pe = jnp.exp2(l - m)
        den = jnp.sum(pe, axis=-1, keepdims=True) + jnp.exp2(s - m)
        o2 = lax.dot_general(pe, vb[feat, :], contract_last,
                             preferred_element_type=F32) / den
        o2r = pltpu.roll(o2, half, axis=1)
        og = jnp.concatenate(
            [jnp.where(lo4, o2[:quarter], o2r[quarter:2 * quarter]),
             jnp.where(lo4, o2r[2 * quarter:3 * quarter], o2[3 * quarter:])], axis=0)
        o_ref[b, groups, :] = og * _silu(gate_ref[b, groups, :])

    pieces = []
    for i in range(SAMPLE_ATT_BATCH):
        pieces.append(functools.partial(update, i))
        pieces += [functools.partial(attend, i, pair) for pair in range(ATT_KV_HEADS // 2)]
    return pieces


def _group_norm_gate(o, gate):
    mu = jnp.mean(o, axis=-1, keepdims=True)
    d = o - mu
    var = jnp.mean(d * d, axis=-1, keepdims=True)
    return d * lax.rsqrt(var + EPS) * _silu(gate)


def _ret_prompt_kernel(g_block_ref, q_ref, k_ref, v_ref, gate_ref, x_ref, wout_ref, gf_ref,
                       y_ref, r_ref, og_ref):
    c = pl.program_id(1)

    @pl.when(c == 0)
    def _():
        r_ref[...] = jnp.zeros_like(r_ref)

    causal = (lax.broadcasted_iota(jnp.int32, (RET_BLOCK, RET_BLOCK), 1)
              <= lax.broadcasted_iota(jnp.int32, (RET_BLOCK, RET_BLOCK), 0))
    contract_last = (((1,), (1,)), ((), ()))
    n_blocks = RET_ROWS // RET_BLOCK
    for blk in range(n_blocks):
        rows = slice(blk * RET_BLOCK, (blk + 1) * RET_BLOCK)
        for h in range(RET_HEADS):
            qcols = slice(h * RET_QK_DIM, (h + 1) * RET_QK_DIM)
            vcols = slice(h * RET_V_DIM, (h + 1) * RET_V_DIM)
            qh = q_ref[rows, qcols]
            kh = k_ref[rows, qcols]
            vh = v_ref[rows, vcols]
            state = r_ref[0, h]
            scores = jnp.where(
                causal, lax.dot_general(qh, kh, contract_last, preferred_element_type=F32), 0.0)
            o = (jnp.dot(scores.astype(BF16), vh, preferred_element_type=F32)
                 + jnp.dot(qh, state.astype(BF16), preferred_element_type=F32))
            r_ref[0, h] = g_block_ref[h] * (
                state + jnp.dot(kh.T, vh, preferred_element_type=F32))
            og_ref[h, rows, :] = _group_norm_gate(
                o, gate_ref[rows, vcols].astype(F32)).astype(og_ref.dtype)
            if blk == n_blocks - 1:
                part = jnp.dot(og_ref[h], wout_ref[vcols, :].astype(BF16),
                               preferred_element_type=F32)
                if h == 0:
                    y_ref[...] = x_ref[...] + part
                elif h < RET_HEADS - 1:
                    y_ref[...] += part
                else:
                    y_ref[...] = _rmsnorm(y_ref[...] + part, gf_ref[...])


def _ret_prompt(q, k, v, gate, x, w_out, final_g, batch, seq):
    m = batch * seq
    nt = seq // RET_ROWS
    g_block = jnp.asarray(_decay_scales_np(RET_BLOCK, RET_BLOCK)[2])
    rows = lambda b, c: (b * nt + c, 0)
    return pl.pallas_call(
        _ret_prompt_kernel,
        grid=(batch, nt),
        in_specs=[pl.BlockSpec(memory_space=pltpu.SMEM),
                  pl.BlockSpec((RET_ROWS, RET_QK_WIDTH), rows),
                  pl.BlockSpec((RET_ROWS, RET_QK_WIDTH), rows),
                  pl.BlockSpec((RET_ROWS, RET_WIDTH), rows),
                  pl.BlockSpec((RET_ROWS, RET_WIDTH), rows),
                  pl.BlockSpec((RET_ROWS, D_MODEL), rows),
                  _resident((RET_WIDTH, D_MODEL), lambda b, c: (0, 0)),
                  pl.BlockSpec((1, D_MODEL), lambda b, c: (0, 0))],
        out_specs=[pl.BlockSpec((RET_ROWS, D_MODEL), rows),
                   pl.BlockSpec((1, RET_HEADS, RET_QK_DIM, RET_V_DIM), lambda b, c: (b, 0, 0, 0))],
        out_shape=[jax.ShapeDtypeStruct((m, D_MODEL), F32),
                   jax.ShapeDtypeStruct((batch, RET_HEADS, RET_QK_DIM, RET_V_DIM), F32)],
        scratch_shapes=[pltpu.VMEM((RET_HEADS, RET_ROWS, RET_V_DIM), BF16)],
        compiler_params=_params(2),
        name="ret_prompt",
    )(g_block, q, k, v, gate, x, w_out, final_g)


def kernel(x_prompt, x_sample, cache_swa_k, cache_swa_v, state_ret, norm_g, final_norm_g, rel_bias,
           w_in_attn, attn_sinks, w_out_attn, w_in_ret, w_out_ret):
    batch, seq, _ = x_prompt.shape
    n_s = x_sample.shape[0]
    xp = x_prompt.reshape(batch * seq, D_MODEL)
    xs = x_sample.reshape(n_s, D_MODEL)
    g0 = norm_g[0].reshape(1, D_MODEL)
    g1 = norm_g[1].reshape(1, D_MODEL)
    gf = final_norm_g.reshape(1, D_MODEL)
    w_out_a = w_out_attn[0]
    w_out_r = w_out_ret[0]
    sinks = attn_sinks[0] * LOG2E

    bias = _bias_table(rel_bias)

    qs, kvs, gates, w_in_a = _attn_inproj(xs, g0, w_in_attn[0], F32)
    order = _sample_head_order()
    bias_s = bias[1, :, WINDOW - 1, :][order]
    sinks_rep = jnp.broadcast_to(sinks[order][:, None], (ATT_HEADS, LANES))
    to_feature_major = lambda c: jnp.transpose(c, (0, 2, 3, 1)).reshape(n_s, ATT_KV_WIDTH, WINDOW)
    to_window_major = lambda c: jnp.transpose(
        c.reshape(n_s, ATT_KV_HEADS, ATT_HEAD_DIM, WINDOW), (0, 3, 1, 2))[None]
    n_groups = ATT_WIDTH // LANES
    q, kv, gate, kvpad, ogs, ckt_new, cvt_new = _attn_inproj(
        xp, g0, w_in_a, BF16,
        sample=(qs.reshape(n_s, n_groups, LANES), kvs,
                to_feature_major(cache_swa_k[0]), to_feature_major(cache_swa_v[0]),
                gates.reshape(n_s, n_groups, LANES), bias_s, sinks_rep))
    og = _swa_prompt(sinks, q, kvpad, gate, bias, batch, seq)
    xp1 = _outproj(og, w_out_a, xp)
    kv_last = kv.reshape(batch, seq, 2 * ATT_KV_WIDTH)[:, seq - WINDOW:]
    k_prompt = kv_last[..., :ATT_KV_WIDTH].reshape(1, batch, WINDOW, ATT_KV_HEADS, ATT_HEAD_DIM)
    v_prompt = kv_last[..., ATT_KV_WIDTH:].reshape(1, batch, WINDOW, ATT_KV_HEADS, ATT_HEAD_DIM)
    xs1 = _outproj(ogs.reshape(n_s, ATT_WIDTH), w_out_a, xs)
    k_sample = to_window_major(ckt_new)
    v_sample = to_window_major(cvt_new)

    q_s, k_s, v_s, gate_s, w_in_r = _ret_inproj(
        xs1, g1, w_in_ret[0], np.full((n_s,), PAST_LEN), 1, F32)
    q, k, v, gate, og_s, r_sample = _ret_inproj(
        xp1, g1, w_in_r, np.arange(seq), RET_BLOCK, BF16,
        sample=(q_s, k_s, v_s, gate_s, state_ret[0]))
    y_prompt, r_prompt = _ret_prompt(q, k, v, gate, xp1, w_out_r, gf, batch, seq)
    y_prompt = y_prompt.reshape(batch, seq, D_MODEL)
    y_sample = _outproj(og_s, w_out_r, xs1, gf).reshape(n_s, 1, D_MODEL)

    return (y_prompt, y_sample, k_prompt, v_prompt, r_prompt[None],
            k_sample, v_sample, r_sample[None])
```

```python
import functools
import math

import numpy as np
import jax
import jax.numpy as jnp
from jax import lax
from jax.experimental import pallas as pl
from jax.experimental.pallas import tpu as pltpu

F32 = jnp.float32
BF16 = jnp.bfloat16

D_MODEL = 1024
PAST_LEN = 16384
ATT_HEADS = 32
ATT_KV_HEADS = 4
ATT_GROUP = ATT_HEADS // ATT_KV_HEADS
ATT_HEAD_DIM = 64
ATT_WIDTH = ATT_HEADS * ATT_HEAD_DIM
ATT_KV_WIDTH = ATT_KV_HEADS * ATT_HEAD_DIM
ATT_IN = 2 * ATT_WIDTH + 2 * ATT_KV_WIDTH
WINDOW = 128
NUM_BUCKETS = 32
MAX_DISTANCE = 128
RET_HEADS = 4
RET_QK_DIM = 256
RET_V_DIM = 512
RET_QK_WIDTH = RET_HEADS * RET_QK_DIM
RET_WIDTH = RET_HEADS * RET_V_DIM
RET_IN = 2 * RET_QK_WIDTH + 2 * RET_WIDTH
ROPE_BASE = 10000.0
EPS = 1e-6
NEG = -1e30
LOG2E = math.log2(math.e)

LANES = 128
PADDED_KV_VARIANTS = 4
PADDED_KV_WIDTH = ATT_KV_HEADS * PADDED_KV_VARIANTS * LANES
PROJ_ROWS = 512
ATT_PROJ_ROWS = 1024
PROJ_COLS = 512
OUT_ROWS = 1024
ATT_ROWS = 1024
RET_BLOCK = 256
RET_ROWS = 1024
SAMPLE_ATT_BATCH = 4
SAMPLE_RET_BATCH = 2
VMEM_LIMIT = 60 * 1024 * 1024


def _params(n_axes):
    return pltpu.CompilerParams(
        dimension_semantics=("arbitrary",) * n_axes,
        vmem_limit_bytes=VMEM_LIMIT)


def _resident(shape, index_map):
    return pl.BlockSpec(shape, index_map, pipeline_mode=pl.Buffered(1))


def _silu(x):
    hx = 0.5 * x
    return hx + hx * jnp.tanh(hx)


def _rmsnorm(x, g):
    return x * lax.rsqrt(jnp.mean(x * x, axis=-1, keepdims=True) + EPS) * g


def _t5_bucket_np(rel):
    n = np.maximum(rel, 0)
    max_exact = NUM_BUCKETS // 2
    nf = np.maximum(n, 1).astype(np.float64)
    large = max_exact + (np.log(nf / max_exact) / math.log(MAX_DISTANCE / max_exact)
                         * (NUM_BUCKETS - max_exact)).astype(np.int32)
    large = np.minimum(large, NUM_BUCKETS - 1)
    return np.where(n < max_exact, n, large).astype(np.int32)


def _bucket_map_np():
    i = np.arange(WINDOW)[:, None]
    j = np.arange(WINDOW)[None, :]
    rel = np.where(j <= i, i - j, i - j + WINDOW)
    return _t5_bucket_np(rel).astype(np.int32)


def _gammas():
    return [1.0 - 2.0 ** (-5 - h) for h in range(RET_HEADS)]


def _decay_scales_np(rows, block):
    lg = np.log1p(-np.exp2(-5.0 - np.arange(RET_HEADS, dtype=np.float64)))
    i1 = (np.arange(rows) % block + 1.0)[None, :] * lg[:, None]
    q_scale = np.repeat(np.exp(i1)[:, :, None], LANES, axis=2)
    k_scale = np.repeat((np.exp(-i1) * RET_QK_DIM ** -0.5)[:, :, None], LANES, axis=2)
    return (q_scale.astype(np.float32), k_scale.astype(np.float32),
            np.exp(block * lg).astype(np.float32))


def _rope_tables_np(pos):
    half = RET_QK_DIM // 2
    inv = ROPE_BASE ** (-np.arange(half, dtype=np.float64) / half)
    ang = np.asarray(pos, np.float64)[:, None] * inv[None, :]
    return np.cos(ang).astype(np.float32), np.sin(ang).astype(np.float32)


def _bias_table_kernel(rb_ref, bucket_ref, out_ref):
    bk = bucket_ref[...]
    previous_block = (lax.broadcasted_iota(jnp.int32, bk.shape, 1)
                      > lax.broadcasted_iota(jnp.int32, bk.shape, 0))

    def one_head(h, carry):
        acc = jnp.zeros(bk.shape, F32)
        for b in range(NUM_BUCKETS):
            acc = jnp.where(bk == b, rb_ref[b, h] * LOG2E, acc)
        out_ref[1, h] = acc
        out_ref[0, h] = jnp.where(previous_block, NEG, acc)
        return carry

    lax.fori_loop(0, ATT_HEADS, one_head, 0)


def _bias_table(rel_bias):
    bucket = jnp.asarray(_bucket_map_np())
    return pl.pallas_call(
        _bias_table_kernel,
        grid=(1,),
        in_specs=[pl.BlockSpec(memory_space=pltpu.SMEM),
                  pl.BlockSpec((WINDOW, WINDOW), lambda i: (0, 0))],
        out_specs=pl.BlockSpec((2, ATT_HEADS, WINDOW, WINDOW), lambda i: (0, 0, 0, 0)),
        out_shape=jax.ShapeDtypeStruct((2, ATT_HEADS, WINDOW, WINDOW), F32),
        compiler_params=_params(1),
        name="bias_table",
    )(rel_bias, bucket)


def _pad_pair(x, x_rolled, lo, e):
    if e == 0:
        return jnp.where(lo, x, 0.0), jnp.where(lo, 0.0, x_rolled)
    return jnp.where(lo, x_rolled, 0.0), jnp.where(lo, 0.0, x)


def _attn_inproj_kernel(*refs, hosts_sample):
    x_ref, g_ref, w_ref = refs[:3]
    side_work = []
    kvpad_ref = None
    if hosts_sample:
        qs_ref, kvs_ref, ckt_ref, cvt_ref, gates_ref, bias_ref, sink_ref = refs[3:10]
        q_ref, kv_ref, gate_ref, kvpad_ref, ogs_ref, cko_ref, cvo_ref, kvt_ref = refs[10:]
        step = pl.program_id(0)

        @pl.when(step == 0)
        def _():
            kvt_ref[...] = kvs_ref[...].T

        side_work = _swa_sample_pieces(step, qs_ref, kvt_ref, ckt_ref, cvt_ref, gates_ref,
                                       bias_ref, sink_ref, ogs_ref, cko_ref, cvo_ref)
    else:
        q_ref, kv_ref, gate_ref, wcast_ref = refs[3:]

    h = _rmsnorm(x_ref[...], g_ref[...]).astype(BF16)
    for c in range(0, ATT_IN, PROJ_COLS):
        if side_work:
            side_work.pop(0)()
        wc = w_ref[:, c:c + PROJ_COLS]
        if not hosts_sample:
            wc = wc.astype(BF16)
            wcast_ref[:, c:c + PROJ_COLS] = wc
        r = jnp.dot(h, wc, preferred_element_type=F32)
        if c < ATT_WIDTH:
            q_ref[:, c:c + PROJ_COLS] = (r * (ATT_HEAD_DIM ** -0.5 * LOG2E)).astype(q_ref.dtype)
        elif c < ATT_WIDTH + 2 * ATT_KV_WIDTH:
            kv_ref[...] = r
            if kvpad_ref is not None:
                lo = lax.broadcasted_iota(jnp.int32, (r.shape[0], LANES), 1) < ATT_HEAD_DIM
                for pair in range(ATT_KV_HEADS // 2):
                    for is_v in range(2):
                        c1 = is_v * ATT_KV_WIDTH + pair * LANES
                        x2 = r[:, c1:c1 + LANES]
                        x2r = pltpu.roll(x2, ATT_HEAD_DIM, axis=1)
                        for e in range(2):
                            g0 = ((2 * pair + e) * PADDED_KV_VARIANTS + 2 * is_v) * LANES
                            lo_half, hi_half = _pad_pair(x2, x2r, lo, e)
                            kvpad_ref[:, g0:g0 + LANES] = lo_half.astype(kvpad_ref.dtype)
                            kvpad_ref[:, g0 + LANES:g0 + 2 * LANES] = hi_half.astype(kvpad_ref.dtype)
        else:
            o = c - ATT_WIDTH - 2 * ATT_KV_WIDTH
            gate_ref[:, o:o + PROJ_COLS] = r.astype(gate_ref.dtype)
    while side_work:
        side_work.pop(0)()


def _attn_inproj(x, g, w, out_dtype, sample=None):
    m = x.shape[0]
    tm = min(ATT_PROJ_ROWS, m)
    row = lambda i: (i, 0)
    const2 = lambda i: (0, 0)
    const3 = lambda i: (0, 0, 0)
    in_specs = [pl.BlockSpec((tm, D_MODEL), row),
                pl.BlockSpec((1, D_MODEL), const2),
                _resident((D_MODEL, ATT_IN), const2)]
    out_specs = [pl.BlockSpec((tm, ATT_WIDTH), row),
                 pl.BlockSpec((tm, 2 * ATT_KV_WIDTH), row),
                 pl.BlockSpec((tm, ATT_WIDTH), row)]
    out_shape = [jax.ShapeDtypeStruct((m, ATT_WIDTH), out_dtype),
                 jax.ShapeDtypeStruct((m, 2 * ATT_KV_WIDTH), F32),
                 jax.ShapeDtypeStruct((m, ATT_WIDTH), out_dtype)]
    args = [x, g, w]
    scratch_shapes = []
    if sample is not None:
        q_s, kv_s, ckt, cvt, gate_s, bias_s, sinks_rep = sample
        n, n_groups, _ = q_s.shape
        assert n == SAMPLE_ATT_BATCH * (m // tm), (n, m, tm)
        cache_spec = pl.BlockSpec((SAMPLE_ATT_BATCH, ATT_KV_WIDTH, WINDOW), lambda i: (i, 0, 0))
        in_specs += [_resident((n, n_groups, LANES), const3),
                     _resident((n, 2 * ATT_KV_WIDTH), const2),
                     cache_spec, cache_spec,
                     _resident((n, n_groups, LANES), const3),
                     _resident((ATT_HEADS, WINDOW), const2),
                     _resident((ATT_HEADS, LANES), const2)]
        out_specs += [pl.BlockSpec((tm, PADDED_KV_WIDTH), row),
                      pl.BlockSpec((n, n_groups, LANES), const3), cache_spec, cache_spec]
        out_shape += [jax.ShapeDtypeStruct((m, PADDED_KV_WIDTH), BF16),
                      jax.ShapeDtypeStruct((n, n_groups, LANES), F32),
                      jax.ShapeDtypeStruct((n, ATT_KV_WIDTH, WINDOW), F32),
                      jax.ShapeDtypeStruct((n, ATT_KV_WIDTH, WINDOW), F32)]
        args += list(sample)
        scratch_shapes = [pltpu.VMEM((2 * ATT_KV_WIDTH, n), F32)]
    else:
        assert m == tm, (m, tm)
        out_specs.append(pl.BlockSpec((D_MODEL, ATT_IN), const2))
        out_shape.append(jax.ShapeDtypeStruct((D_MODEL, ATT_IN), BF16))
    return pl.pallas_call(
        functools.partial(_attn_inproj_kernel, hosts_sample=sample is not None),
        grid=(m // tm,),
        in_specs=in_specs,
        out_specs=out_specs,
        out_shape=out_shape,
        scratch_shapes=scratch_shapes,
        compiler_params=_params(1),
        name="attn_inproj",
    )(*args)


def _ret_sample_pieces(step, qt_ref, kt_ref, v_ref, gate_ref, st_ref, og_ref, sto_ref):
    lane = lax.broadcasted_iota(jnp.int32, (RET_QK_DIM, LANES), 1)
    gammas = _gammas()

    def piece(i, h):
        b = step * SAMPLE_RET_BATCH + i
        pick = lane == b
        rows = slice(h * RET_QK_DIM, (h + 1) * RET_QK_DIM)
        qc = jnp.sum(jnp.where(pick, qt_ref[rows, :], 0.0), axis=1, keepdims=True)
        kc = jnp.sum(jnp.where(pick, kt_ref[rows, :], 0.0), axis=1, keepdims=True)
        vcols = slice(h * RET_V_DIM, (h + 1) * RET_V_DIM)
        vrow = v_ref[pl.ds(b, 1), vcols]
        state = st_ref[i, h]
        o = (jnp.sum(qc * state, axis=0, keepdims=True)
             + jnp.sum(qc * kc, axis=0, keepdims=True) * vrow)
        sto_ref[i, h] = gammas[h] * (state + kc * vrow)
        og_ref[pl.ds(b, 1), vcols] = _group_norm_gate(o, gate_ref[pl.ds(b, 1), vcols])

    return [functools.partial(piece, i, h)
            for i in range(SAMPLE_RET_BATCH) for h in range(RET_HEADS)]


def _ret_inproj_kernel(*refs, with_sample):
    x_ref, g_ref, w_ref, cos_ref, sin_ref, qsc_ref, ksc_ref = refs[:7]
    side_work = []
    if with_sample:
        qs_ref, ks_ref, vs_ref, gates_ref, st_ref = refs[7:12]
        q_ref, k_ref, v_ref, gate_ref, ogs_ref, sto_ref, qts_ref, kts_ref = refs[12:]
        step = pl.program_id(0)

        @pl.when(step == 0)
        def _():
            qts_ref[...] = qs_ref[...].T
            kts_ref[...] = ks_ref[...].T

        side_work = _ret_sample_pieces(step, qts_ref, kts_ref, vs_ref, gates_ref, st_ref,
                                       ogs_ref, sto_ref)
    else:
        q_ref, k_ref, v_ref, gate_ref, wcast_ref = refs[7:]

    def interleave():
        if side_work:
            side_work.pop(0)()

    def weights(c):
        wc = w_ref[:, c:c + PROJ_COLS]
        if not with_sample:
            wc = wc.astype(BF16)
            wcast_ref[:, c:c + PROJ_COLS] = wc
        return wc

    h = _rmsnorm(x_ref[...], g_ref[...]).astype(BF16)
    half = RET_QK_DIM // 2
    cos = cos_ref[...]
    sin = sin_ref[...]
    for out_ref, scale_ref, w0 in ((q_ref, qsc_ref, 0), (k_ref, ksc_ref, RET_QK_WIDTH)):
        for c in range(0, RET_QK_WIDTH, PROJ_COLS):
            interleave()
            r = jnp.dot(h, weights(w0 + c), preferred_element_type=F32)
            for o in range(0, PROJ_COLS, RET_QK_DIM):
                scale = scale_ref[(c + o) // RET_QK_DIM]
                x1 = r[:, o:o + half]
                x2 = r[:, o + half:o + RET_QK_DIM]
                out_ref[:, c + o:c + o + half] = (
                    (x1 * cos - x2 * sin) * scale).astype(out_ref.dtype)
                out_ref[:, c + o + half:c + o + RET_QK_DIM] = (
                    (x1 * sin + x2 * cos) * scale).astype(out_ref.dtype)
    for c in range(0, RET_WIDTH, PROJ_COLS):
        interleave()
        v_ref[:, c:c + PROJ_COLS] = jnp.dot(
            h, weights(2 * RET_QK_WIDTH + c), preferred_element_type=F32).astype(v_ref.dtype)
    for c in range(0, RET_WIDTH, PROJ_COLS):
        interleave()
        gate_ref[:, c:c + PROJ_COLS] = jnp.dot(
            h, weights(2 * RET_QK_WIDTH + RET_WIDTH + c),
            preferred_element_type=F32).astype(gate_ref.dtype)
    while side_work:
        interleave()


def _ret_inproj(x, g, w, pos, block, out_dtype, sample=None):
    m = x.shape[0]
    tm = min(PROJ_ROWS, m)
    n_pos = len(pos) // tm
    half = RET_QK_DIM // 2
    cos, sin = _rope_tables_np(pos)
    q_scale, k_scale, _ = _decay_scales_np(tm, block)
    row = lambda i: (i, 0)
    const2 = lambda i: (0, 0)
    const3 = lambda i: (0, 0, 0)
    in_specs = [pl.BlockSpec((tm, D_MODEL), row),
                pl.BlockSpec((1, D_MODEL), const2),
                _resident((D_MODEL, RET_IN), const2),
                pl.BlockSpec((tm, half), lambda i: (i % n_pos, 0)),
                pl.BlockSpec((tm, half), lambda i: (i % n_pos, 0)),
                _resident((RET_HEADS, tm, LANES), const3),
                _resident((RET_HEADS, tm, LANES), const3)]
    out_specs = [pl.BlockSpec((tm, RET_QK_WIDTH), row),
                 pl.BlockSpec((tm, RET_QK_WIDTH), row),
                 pl.BlockSpec((tm, RET_WIDTH), row),
                 pl.BlockSpec((tm, RET_WIDTH), row)]
    out_shape = [jax.ShapeDtypeStruct((m, RET_QK_WIDTH), out_dtype),
                 jax.ShapeDtypeStruct((m, RET_QK_WIDTH), out_dtype),
                 jax.ShapeDtypeStruct((m, RET_WIDTH), out_dtype),
                 jax.ShapeDtypeStruct((m, RET_WIDTH), out_dtype)]
    args = [x, g, w, jnp.asarray(cos), jnp.asarray(sin), jnp.asarray(q_scale), jnp.asarray(k_scale)]
    scratch_shapes = []
    if sample is not None:
        q_s, k_s, v_s, gate_s, state = sample
        n = q_s.shape[0]
        assert n == SAMPLE_RET_BATCH * (m // tm), (n, m, tm)
        st_spec = pl.BlockSpec((SAMPLE_RET_BATCH, RET_HEADS, RET_QK_DIM, RET_V_DIM),
                               lambda i: (i, 0, 0, 0))
        in_specs += [_resident((n, RET_QK_WIDTH), const2), _resident((n, RET_QK_WIDTH), const2),
                     _resident((n, RET_WIDTH), const2), _resident((n, RET_WIDTH), const2), st_spec]
        out_specs += [pl.BlockSpec((n, RET_WIDTH), const2), st_spec]
        out_shape += [jax.ShapeDtypeStruct((n, RET_WIDTH), F32),
                      jax.ShapeDtypeStruct(state.shape, F32)]
        args += [q_s, k_s, v_s, gate_s, state]
        scratch_shapes = [pltpu.VMEM((RET_QK_WIDTH, n), F32), pltpu.VMEM((RET_QK_WIDTH, n), F32)]
    else:
        assert m == tm, (m, tm)
        out_specs.append(pl.BlockSpec((D_MODEL, RET_IN), const2))
        out_shape.append(jax.ShapeDtypeStruct((D_MODEL, RET_IN), BF16))
    return pl.pallas_call(
        functools.partial(_ret_inproj_kernel, with_sample=sample is not None),
        grid=(m // tm,),
        in_specs=in_specs,
        out_specs=out_specs,
        out_shape=out_shape,
        scratch_shapes=scratch_shapes,
        compiler_params=_params(1),
        name="ret_inproj",
    )(*args)


def _outproj_kernel(o_ref, w_ref, x_ref, y_ref):
    y_ref[...] = x_ref[...] + jnp.dot(o_ref[...].astype(BF16), w_ref[...].astype(BF16),
                                      preferred_element_type=F32)


def _outproj_norm_kernel(o_ref, w_ref, x_ref, g_ref, y_ref):
    y = x_ref[...] + jnp.dot(o_ref[...].astype(BF16), w_ref[...].astype(BF16),
                             preferred_element_type=F32)
    y_ref[...] = _rmsnorm(y, g_ref[...])


def _outproj(o, w, x, final_g=None):
    m, width = o.shape
    tm = min(OUT_ROWS, m)
    row = lambda i: (i, 0)
    in_specs = [pl.BlockSpec((tm, width), row),
                _resident((width, D_MODEL), lambda i: (0, 0)),
                pl.BlockSpec((tm, D_MODEL), row)]
    args = [o, w, x]
    body = _outproj_kernel
    if final_g is not None:
        in_specs.append(pl.BlockSpec((1, D_MODEL), lambda i: (0, 0)))
        args.append(final_g)
        body = _outproj_norm_kernel
    return pl.pallas_call(
        body,
        grid=(m // tm,),
        in_specs=in_specs,
        out_specs=pl.BlockSpec((tm, D_MODEL), row),
        out_shape=jax.ShapeDtypeStruct((m, D_MODEL), F32),
        compiler_params=_params(1),
        name="outproj",
    )(*args)


def _swa_prompt_kernel(sink_ref, q_ref, kvc_ref, kvp_ref, gate_ref, bias_ref, x_ref, wout_ref,
                       y_ref, og_ref):
    j = pl.program_id(1)
    half = ATT_HEAD_DIM
    pairs_per_kv = ATT_GROUP // 2
    kv_group_cols = ATT_GROUP * ATT_HEAD_DIM
    contract_last = (((1,), (1,)), ((), ()))
    lo1 = lax.broadcasted_iota(jnp.int32, (WINDOW, LANES), 1) < half
    own = (lax.broadcasted_iota(jnp.int32, (WINDOW, WINDOW), 1)
           <= lax.broadcasted_iota(jnp.int32, (WINDOW, WINDOW), 0))

    def keys_values(blk, kvh):
        r0 = blk * WINDOW
        out = []
        for variant in range(PADDED_KV_VARIANTS):
            c = (kvh * PADDED_KV_VARIANTS + variant) * LANES
            prev = kvp_ref[:, c:c + LANES] if blk == 0 else kvc_ref[r0 - WINDOW:r0, c:c + LANES]
            out.append(jnp.concatenate([prev, kvc_ref[r0:r0 + WINDOW, c:c + LANES]], axis=0))
        return out

    def scores(blk, kvh):
        rows = slice(blk * WINDOW, (blk + 1) * WINDOW)
        c0 = kvh * kv_group_cols
        k_a, k_b, v_a, v_b = keys_values(blk, kvh)
        qs = jnp.concatenate(
            [q_ref[rows, c0 + p * LANES:c0 + (p + 1) * LANES] for p in range(pairs_per_kv)],
            axis=0)
        logits = (lax.dot_general(qs, k_a, contract_last, preferred_element_type=F32),
                  lax.dot_general(qs, k_b, contract_last, preferred_element_type=F32))
        return logits, v_a, v_b

    def softmax(blk, kvh, logits):
        sel = jnp.where(j == 0, 0, 1) if blk == 0 else 1
        probs = ([], [])
        inv = []
        for p in range(pairs_per_kv):
            top, total, sink = [], [], []
            for st in range(2):
                hd = kvh * ATT_GROUP + 2 * p + st
                lg = logits[st][p * WINDOW:(p + 1) * WINDOW, :]
                l = jnp.where(own, lg[:, WINDOW:], lg[:, :WINDOW]) + bias_ref[sel, hd]
                s = sink_ref[hd]
                m = jnp.maximum(jnp.max(l, axis=-1, keepdims=True), s)
                pe = jnp.exp2(l - m)
                probs[st].append(jnp.concatenate(
                    [jnp.where(own, 0.0, pe).astype(BF16),
                     jnp.where(own, pe, 0.0).astype(BF16)], axis=1))
                top.append(m)
                total.append(jnp.sum(pe, axis=-1, keepdims=True))
                sink.append(s)
            sink_term = jnp.exp2(jnp.where(lo1, sink[0], sink[1]) - jnp.where(lo1, top[0], top[1]))
            inv.append(1.0 / (jnp.where(lo1, total[0], total[1]) + sink_term))
        return jnp.concatenate(probs[0], axis=0), jnp.concatenate(probs[1], axis=0), inv

    def output(blk, kvh, p_a, p_b, inv, v_a, v_b):
        rows = slice(blk * WINDOW, (blk + 1) * WINDOW)
        c0 = kvh * kv_group_cols
        o = (jnp.dot(p_a, v_a, preferred_element_type=F32)
             + jnp.dot(p_b, v_b, preferred_element_type=F32))
        for p in range(pairs_per_kv):
            gt = gate_ref[rows, c0 + p * LANES:c0 + (p + 1) * LANES].astype(F32)
            og = o[p * WINDOW:(p + 1) * WINDOW, :] * inv[p] * _silu(gt)
            og_ref[kvh, rows, p * LANES:(p + 1) * LANES] = og.astype(og_ref.dtype)

    for kvh in range(ATT_KV_HEADS):
        for blk in range(ATT_ROWS // WINDOW):
            logits, v_a, v_b = scores(blk, kvh)
            p_a, p_b, inv = softmax(blk, kvh, logits)
            output(blk, kvh, p_a, p_b, inv, v_a, v_b)
        c0 = kvh * kv_group_cols
        part = jnp.dot(og_ref[kvh], wout_ref[c0:c0 + kv_group_cols, :],
                       preferred_element_type=F32)
        if kvh == 0:
            y_ref[...] = x_ref[...] + part
        else:
            y_ref[...] += part


def _swa_prompt(sinks, q, kvpad, gate, bias, x, w_out, batch, seq):
    m = batch * seq
    nt = seq // ATT_ROWS
    blocks_per_tile = ATT_ROWS // WINDOW
    blocks_per_seq = seq // WINDOW
    tile = lambda b, j: (b * nt + j, 0)
    prev = lambda b, j: (b * blocks_per_seq + jnp.maximum(j * blocks_per_tile - 1, 0), 0)
    return pl.pallas_call(
        _swa_prompt_kernel,
        grid=(batch, nt),
        in_specs=[pl.BlockSpec(memory_space=pltpu.SMEM),
                  pl.BlockSpec((ATT_ROWS, ATT_WIDTH), tile),
                  pl.BlockSpec((ATT_ROWS, PADDED_KV_WIDTH), tile),
                  pl.BlockSpec((WINDOW, PADDED_KV_WIDTH), prev),
                  pl.BlockSpec((ATT_ROWS, ATT_WIDTH), tile),
                  _resident((2, ATT_HEADS, WINDOW, WINDOW), lambda b, j: (0, 0, 0, 0)),
                  pl.BlockSpec((ATT_ROWS, D_MODEL), tile),
                  _resident((ATT_WIDTH, D_MODEL), lambda b, j: (0, 0))],
        out_specs=pl.BlockSpec((ATT_ROWS, D_MODEL), tile),
        out_shape=jax.ShapeDtypeStruct((m, D_MODEL), F32),
        scratch_shapes=[pltpu.VMEM((ATT_KV_HEADS, ATT_ROWS, ATT_GROUP * ATT_HEAD_DIM), BF16)],
        compiler_params=_params(2),
        name="swa_prompt",
    )(sinks, q, kvpad, kvpad, gate, bias, x, w_out)


def _sample_head_order():
    order = []
    for pair in range(ATT_KV_HEADS // 2):
        base = pair * 2 * ATT_GROUP
        for kv in range(2):
            for half in range(2):
                order += [base + kv * ATT_GROUP + 2 * g + half for g in range(ATT_GROUP // 2)]
    return np.asarray(order, np.int32)


def _swa_sample_pieces(step, q_ref, kvt_ref, ckt_ref, cvt_ref, gate_ref, bias_ref, sink_ref,
                       o_ref, cko_ref, cvo_ref):
    half = ATT_HEAD_DIM
    quarter = ATT_GROUP // 2
    contract_last = (((1,), (1,)), ((), ()))
    newest = lax.broadcasted_iota(jnp.int32, (ATT_KV_WIDTH, WINDOW), 1) == WINDOW - 1
    seq_lane = lax.broadcasted_iota(jnp.int32, (2 * ATT_KV_WIDTH, LANES), 1)
    lo4 = lax.broadcasted_iota(jnp.int32, (quarter, LANES), 1) < half
    new_cache = {}

    def update(i):
        b = step * SAMPLE_ATT_BATCH + i
        new_col = jnp.sum(jnp.where(seq_lane == b, kvt_ref[...], 0.0), axis=1, keepdims=True)
        kb = jnp.where(newest, new_col[:ATT_KV_WIDTH], pltpu.roll(ckt_ref[i], WINDOW - 1, axis=1))
        vb = jnp.where(newest, new_col[ATT_KV_WIDTH:], pltpu.roll(cvt_ref[i], WINDOW - 1, axis=1))
        cko_ref[i] = kb
        cvo_ref[i] = vb
        new_cache[i] = (kb, vb)

    def attend(i, pair):
        b = step * SAMPLE_ATT_BATCH + i
        kb, vb = new_cache[i]
        groups = slice(pair * ATT_GROUP, (pair + 1) * ATT_GROUP)
        hs = slice(pair * 2 * ATT_GROUP, (pair + 1) * 2 * ATT_GROUP)
        feat = slice(pair * LANES, (pair + 1) * LANES)
        g8 = q_ref[b, groups, :]
        g8r = pltpu.roll(g8, half, axis=1)
        qbd = jnp.concatenate(
            [jnp.where(lo4, g8[:quarter], 0.0), jnp.where(lo4, g8r[:quarter], 0.0),
             jnp.where(lo4, 0.0, g8r[quarter:]), jnp.where(lo4, 0.0, g8[quarter:])],
            axis=0)
        l = jnp.dot(qbd, kb[feat, :], preferred_element_type=F32) + bias_ref[hs, :]
        s = sink_ref[hs, 0:1]
        m = jnp.maximum(jnp.max(l, axis=-1, keepdims=True), s)
        pe = jnp.exp2(l - m)
        den = jnp.sum(pe, axis=-1, keepdims=True) + jnp.exp2(s - m)
        o2 = lax.dot_general(pe, vb[feat, :], contract_last,
                             preferred_element_type=F32) / den
        o2r = pltpu.roll(o2, half, axis=1)
        og = jnp.concatenate(
            [jnp.where(lo4, o2[:quarter], o2r[quarter:2 * quarter]),
             jnp.where(lo4, o2r[2 * quarter:3 * quarter], o2[3 * quarter:])], axis=0)
        o_ref[b, groups, :] = og * _silu(gate_ref[b, groups, :])

    pieces = []
    for i in range(SAMPLE_ATT_BATCH):
        pieces.append(functools.partial(update, i))
        pieces += [functools.partial(attend, i, pair) for pair in range(ATT_KV_HEADS // 2)]
    return pieces


def _group_norm_gate(o, gate):
    mu = jnp.mean(o, axis=-1, keepdims=True)
    d = o - mu
    var = jnp.mean(d * d, axis=-1, keepdims=True)
    return d * lax.rsqrt(var + EPS) * _silu(gate)


def _ret_prompt_kernel(g_block_ref, q_ref, k_ref, v_ref, gate_ref, x_ref, wout_ref, gf_ref,
                       y_ref, r_ref, og_ref):
    c = pl.program_id(1)

    @pl.when(c == 0)
    def _():
        r_ref[...] = jnp.zeros_like(r_ref)

    causal = (lax.broadcasted_iota(jnp.int32, (RET_BLOCK, RET_BLOCK), 1)
              <= lax.broadcasted_iota(jnp.int32, (RET_BLOCK, RET_BLOCK), 0))
    contract_last = (((1,), (1,)), ((), ()))
    n_blocks = RET_ROWS // RET_BLOCK
    for blk in range(n_blocks):
        rows = slice(blk * RET_BLOCK, (blk + 1) * RET_BLOCK)
        for h in range(RET_HEADS):
            qcols = slice(h * RET_QK_DIM, (h + 1) * RET_QK_DIM)
            vcols = slice(h * RET_V_DIM, (h + 1) * RET_V_DIM)
            qh = q_ref[rows, qcols]
            kh = k_ref[rows, qcols]
            vh = v_ref[rows, vcols]
            state = r_ref[0, h]
            scores = jnp.where(
                causal, lax.dot_general(qh, kh, contract_last, preferred_element_type=F32), 0.0)
            o = (jnp.dot(scores.astype(BF16), vh, preferred_element_type=F32)
                 + jnp.dot(qh, state.astype(BF16), preferred_element_type=F32))
            r_ref[0, h] = g_block_ref[h] * (
                state + jnp.dot(kh.T, vh, preferred_element_type=F32))
            og_ref[h, rows, :] = _group_norm_gate(
                o, gate_ref[rows, vcols].astype(F32)).astype(og_ref.dtype)
            if blk == n_blocks - 1:
                part = jnp.dot(og_ref[h], wout_ref[vcols, :].astype(BF16),
                               preferred_element_type=F32)
                if h == 0:
                    y_ref[...] = x_ref[...] + part
                elif h < RET_HEADS - 1:
                    y_ref[...] += part
                else:
                    y_ref[...] = _rmsnorm(y_ref[...] + part, gf_ref[...])


def _ret_prompt(q, k, v, gate, x, w_out, final_g, batch, seq):
    m = batch * seq
    nt = seq // RET_ROWS
    g_block = jnp.asarray(_decay_scales_np(RET_BLOCK, RET_BLOCK)[2])
    rows = lambda b, c: (b * nt + c, 0)
    return pl.pallas_call(
        _ret_prompt_kernel,
        grid=(batch, nt),
        in_specs=[pl.BlockSpec(memory_space=pltpu.SMEM),
                  pl.BlockSpec((RET_ROWS, RET_QK_WIDTH), rows),
                  pl.BlockSpec((RET_ROWS, RET_QK_WIDTH), rows),
                  pl.BlockSpec((RET_ROWS, RET_WIDTH), rows),
                  pl.BlockSpec((RET_ROWS, RET_WIDTH), rows),
                  pl.BlockSpec((RET_ROWS, D_MODEL), rows),
                  _resident((RET_WIDTH, D_MODEL), lambda b, c: (0, 0)),
                  pl.BlockSpec((1, D_MODEL), lambda b, c: (0, 0))],
        out_specs=[pl.BlockSpec((RET_ROWS, D_MODEL), rows),
                   pl.BlockSpec((1, RET_HEADS, RET_QK_DIM, RET_V_DIM), lambda b, c: (b, 0, 0, 0))],
        out_shape=[jax.ShapeDtypeStruct((m, D_MODEL), F32),
                   jax.ShapeDtypeStruct((batch, RET_HEADS, RET_QK_DIM, RET_V_DIM), F32)],
        scratch_shapes=[pltpu.VMEM((RET_HEADS, RET_ROWS, RET_V_DIM), BF16)],
        compiler_params=_params(2),
        name="ret_prompt",
    )(g_block, q, k, v, gate, x, w_out, final_g)


def kernel(x_prompt, x_sample, cache_swa_k, cache_swa_v, state_ret, norm_g, final_norm_g, rel_bias,
           w_in_attn, attn_sinks, w_out_attn, w_in_ret, w_out_ret):
    batch, seq, _ = x_prompt.shape
    n_s = x_sample.shape[0]
    xp = x_prompt.reshape(batch * seq, D_MODEL)
    xs = x_sample.reshape(n_s, D_MODEL)
    g0 = norm_g[0].reshape(1, D_MODEL)
    g1 = norm_g[1].reshape(1, D_MODEL)
    gf = final_norm_g.reshape(1, D_MODEL)
    w_out_a = w_out_attn[0]
    w_out_r = w_out_ret[0]
    sinks = attn_sinks[0] * LOG2E

    bias = _bias_table(rel_bias)

    qs, kvs, gates, w_in_a = _attn_inproj(xs, g0, w_in_attn[0], F32)
    order = _sample_head_order()
    bias_s = bias[1, :, WINDOW - 1, :][order]
    sinks_rep = jnp.broadcast_to(sinks[order][:, None], (ATT_HEADS, LANES))
    to_feature_major = lambda c: jnp.transpose(c, (0, 2, 3, 1)).reshape(n_s, ATT_KV_WIDTH, WINDOW)
    to_window_major = lambda c: jnp.transpose(
        c.reshape(n_s, ATT_KV_HEADS, ATT_HEAD_DIM, WINDOW), (0, 3, 1, 2))[None]
    n_groups = ATT_WIDTH // LANES
    q, kv, gate, kvpad, ogs, ckt_new, cvt_new = _attn_inproj(
        xp, g0, w_in_a, BF16,
        sample=(qs.reshape(n_s, n_groups, LANES), kvs,
                to_feature_major(cache_swa_k[0]), to_feature_major(cache_swa_v[0]),
                gates.reshape(n_s, n_groups, LANES), bias_s, sinks_rep))
    xp1 = _swa_prompt(sinks, q, kvpad, gate, bias, xp, w_out_a.astype(BF16), batch, seq)
    kv_last = kv.reshape(batch, seq, 2 * ATT_KV_WIDTH)[:, seq - WINDOW:]
    k_prompt = kv_last[..., :ATT_KV_WIDTH].reshape(1, batch, WINDOW, ATT_KV_HEADS, ATT_HEAD_DIM)
    v_prompt = kv_last[..., ATT_KV_WIDTH:].reshape(1, batch, WINDOW, ATT_KV_HEADS, ATT_HEAD_DIM)
    xs1 = _outproj(ogs.reshape(n_s, ATT_WIDTH), w_out_a, xs)
    k_sample = to_window_major(ckt_new)
    v_sample = to_window_major(cvt_new)

    q_s, k_s, v_s, gate_s, w_in_r = _ret_inproj(
        xs1, g1, w_in_ret[0], np.full((n_s,), PAST_LEN), 1, F32)
    q, k, v, gate, og_s, r_sample = _ret_inproj(
        xp1, g1, w_in_r, np.arange(seq), RET_BLOCK, BF16,
        sample=(q_s, k_s, v_s, gate_s, state_ret[0]))
    y_prompt, r_prompt = _ret_prompt(q, k, v, gate, xp1, w_out_r, gf, batch, seq)
    y_prompt = y_prompt.reshape(batch, seq, D_MODEL)
    y_sample = _outproj(og_s, w_out_r, xs1, gf).reshape(n_s, 1, D_MODEL)

    return (y_prompt, y_sample, k_prompt, v_prompt, r_prompt[None],
            k_sample, v_sample, r_sample[None])
```

```python
import functools
import math

import numpy as np
import jax
import jax.numpy as jnp
from jax import lax
from jax.experimental import pallas as pl
from jax.experimental.pallas import tpu as pltpu

F32 = jnp.float32
BF16 = jnp.bfloat16

D_MODEL = 1024
PAST_LEN = 16384
ATT_HEADS = 32
ATT_KV_HEADS = 4
ATT_GROUP = ATT_HEADS // ATT_KV_HEADS
ATT_HEAD_DIM = 64
ATT_WIDTH = ATT_HEADS * ATT_HEAD_DIM
ATT_KV_WIDTH = ATT_KV_HEADS * ATT_HEAD_DIM
ATT_IN = 2 * ATT_WIDTH + 2 * ATT_KV_WIDTH
WINDOW = 128
NUM_BUCKETS = 32
MAX_DISTANCE = 128
RET_HEADS = 4
RET_QK_DIM = 256
RET_V_DIM = 512
RET_QK_WIDTH = RET_HEADS * RET_QK_DIM
RET_WIDTH = RET_HEADS * RET_V_DIM
RET_IN = 2 * RET_QK_WIDTH + 2 * RET_WIDTH
ROPE_BASE = 10000.0
EPS = 1e-6
NEG = -1e30
LOG2E = math.log2(math.e)

LANES = 128
PADDED_KV_VARIANTS = 4
PADDED_KV_WIDTH = ATT_KV_HEADS * PADDED_KV_VARIANTS * LANES
PROJ_ROWS = 512
ATT_PROJ_ROWS = 1024
PROJ_COLS = 512
OUT_ROWS = 1024
ATT_ROWS = 1024
RET_BLOCK = 256
RET_ROWS = 1024
SAMPLE_ATT_BATCH = 4
SAMPLE_RET_BATCH = 2
VMEM_LIMIT = 60 * 1024 * 1024


def _params(n_axes):
    return pltpu.CompilerParams(
        dimension_semantics=("arbitrary",) * n_axes,
        vmem_limit_bytes=VMEM_LIMIT)


def _resident(shape, index_map):
    return pl.BlockSpec(shape, index_map, pipeline_mode=pl.Buffered(1))


def _silu(x):
    hx = 0.5 * x
    return hx + hx * jnp.tanh(hx)


def _rmsnorm(x, g):
    return x * lax.rsqrt(jnp.mean(x * x, axis=-1, keepdims=True) + EPS) * g


def _t5_bucket_np(rel):
    n = np.maximum(rel, 0)
    max_exact = NUM_BUCKETS // 2
    nf = np.maximum(n, 1).astype(np.float64)
    large = max_exact + (np.log(nf / max_exact) / math.log(MAX_DISTANCE / max_exact)
                         * (NUM_BUCKETS - max_exact)).astype(np.int32)
    large = np.minimum(large, NUM_BUCKETS - 1)
    return np.where(n < max_exact, n, large).astype(np.int32)


def _bucket_map_np():
    i = np.arange(WINDOW)[:, None]
    j = np.arange(WINDOW)[None, :]
    rel = np.where(j <= i, i - j, i - j + WINDOW)
    return _t5_bucket_np(rel).astype(np.int32)


def _gammas():
    return [1.0 - 2.0 ** (-5 - h) for h in range(RET_HEADS)]


def _decay_scales_np(rows, block):
    lg = np.log1p(-np.exp2(-5.0 - np.arange(RET_HEADS, dtype=np.float64)))
    i1 = (np.arange(rows) % block + 1.0)[None, :] * lg[:, None]
    q_scale = np.repeat(np.exp(i1)[:, :, None], LANES, axis=2)
    k_scale = np.repeat((np.exp(-i1) * RET_QK_DIM ** -0.5)[:, :, None], LANES, axis=2)
    return (q_scale.astype(np.float32), k_scale.astype(np.float32),
            np.exp(block * lg).astype(np.float32))


def _rope_tables_np(pos):
    half = RET_QK_DIM // 2
    inv = ROPE_BASE ** (-np.arange(half, dtype=np.float64) / half)
    ang = np.asarray(pos, np.float64)[:, None] * inv[None, :]
    return np.cos(ang).astype(np.float32), np.sin(ang).astype(np.float32)


def _bias_table_kernel(rb_ref, bucket_ref, out_ref):
    bk = bucket_ref[...]
    previous_block = (lax.broadcasted_iota(jnp.int32, bk.shape, 1)
                      > lax.broadcasted_iota(jnp.int32, bk.shape, 0))

    def one_head(h, carry):
        acc = jnp.zeros(bk.shape, F32)
        for b in range(NUM_BUCKETS):
            acc = jnp.where(bk == b, rb_ref[b, h] * LOG2E, acc)
        out_ref[1, h] = acc
        out_ref[0, h] = jnp.where(previous_block, NEG, acc)
        return carry

    lax.fori_loop(0, ATT_HEADS, one_head, 0)


def _bias_table(rel_bias):
    bucket = jnp.asarray(_bucket_map_np())
    return pl.pallas_call(
        _bias_table_kernel,
        grid=(1,),
        in_specs=[pl.BlockSpec(memory_space=pltpu.SMEM),
                  pl.BlockSpec((WINDOW, WINDOW), lambda i: (0, 0))],
        out_specs=pl.BlockSpec((2, ATT_HEADS, WINDOW, WINDOW), lambda i: (0, 0, 0, 0)),
        out_shape=jax.ShapeDtypeStruct((2, ATT_HEADS, WINDOW, WINDOW), F32),
        compiler_params=_params(1),
        name="bias_table",
    )(rel_bias, bucket)


def _pad_pair(x, x_rolled, lo, e):
    if e == 0:
        return jnp.where(lo, x, 0.0), jnp.where(lo, 0.0, x_rolled)
    return jnp.where(lo, x_rolled, 0.0), jnp.where(lo, 0.0, x)


def _attn_inproj_kernel(*refs, hosts_sample):
    x_ref, g_ref, w_ref = refs[:3]
    side_work = []
    kvpad_ref = None
    if hosts_sample:
        qs_ref, kvs_ref, ckt_ref, cvt_ref, gates_ref, bias_ref, sink_ref = refs[3:10]
        q_ref, kv_ref, gate_ref, kvpad_ref, ogs_ref, cko_ref, cvo_ref, kvt_ref = refs[10:]
        step = pl.program_id(0)

        @pl.when(step == 0)
        def _():
            kvt_ref[...] = kvs_ref[...].T

        side_work = _swa_sample_pieces(step, qs_ref, kvt_ref, ckt_ref, cvt_ref, gates_ref,
                                       bias_ref, sink_ref, ogs_ref, cko_ref, cvo_ref)
    else:
        q_ref, kv_ref, gate_ref, wcast_ref = refs[3:]

    h = _rmsnorm(x_ref[...], g_ref[...]).astype(BF16)
    for c in range(0, ATT_IN, PROJ_COLS):
        if side_work:
            side_work.pop(0)()
        wc = w_ref[:, c:c + PROJ_COLS]
        if not hosts_sample:
            wc = wc.astype(BF16)
            wcast_ref[:, c:c + PROJ_COLS] = wc
        r = jnp.dot(h, wc, preferred_element_type=F32)
        if c < ATT_WIDTH:
            q_ref[:, c:c + PROJ_COLS] = (r * (ATT_HEAD_DIM ** -0.5 * LOG2E)).astype(q_ref.dtype)
        elif c < ATT_WIDTH + 2 * ATT_KV_WIDTH:
            kv_ref[...] = r
            if kvpad_ref is not None:
                lo = lax.broadcasted_iota(jnp.int32, (r.shape[0], LANES), 1) < ATT_HEAD_DIM
                for pair in range(ATT_KV_HEADS // 2):
                    for is_v in range(2):
                        c1 = is_v * ATT_KV_WIDTH + pair * LANES
                        x2 = r[:, c1:c1 + LANES]
                        x2r = pltpu.roll(x2, ATT_HEAD_DIM, axis=1)
                        for e in range(2):
                            g0 = ((2 * pair + e) * PADDED_KV_VARIANTS + 2 * is_v) * LANES
                            lo_half, hi_half = _pad_pair(x2, x2r, lo, e)
                            kvpad_ref[:, g0:g0 + LANES] = lo_half.astype(kvpad_ref.dtype)
                            kvpad_ref[:, g0 + LANES:g0 + 2 * LANES] = hi_half.astype(kvpad_ref.dtype)
        else:
            o = c - ATT_WIDTH - 2 * ATT_KV_WIDTH
            gate_ref[:, o:o + PROJ_COLS] = _silu(r).astype(gate_ref.dtype)
    while side_work:
        side_work.pop(0)()


def _attn_inproj(x, g, w, out_dtype, sample=None):
    m = x.shape[0]
    tm = min(ATT_PROJ_ROWS, m)
    row = lambda i: (i, 0)
    const2 = lambda i: (0, 0)
    const3 = lambda i: (0, 0, 0)
    in_specs = [pl.BlockSpec((tm, D_MODEL), row),
                pl.BlockSpec((1, D_MODEL), const2),
                _resident((D_MODEL, ATT_IN), const2)]
    out_specs = [pl.BlockSpec((tm, ATT_WIDTH), row),
                 pl.BlockSpec((tm, 2 * ATT_KV_WIDTH), row),
                 pl.BlockSpec((tm, ATT_WIDTH), row)]
    out_shape = [jax.ShapeDtypeStruct((m, ATT_WIDTH), out_dtype),
                 jax.ShapeDtypeStruct((m, 2 * ATT_KV_WIDTH), F32),
                 jax.ShapeDtypeStruct((m, ATT_WIDTH), out_dtype)]
    args = [x, g, w]
    scratch_shapes = []
    if sample is not None:
        q_s, kv_s, ckt, cvt, gate_s, bias_s, sinks_rep = sample
        n, n_groups, _ = q_s.shape
        assert n == SAMPLE_ATT_BATCH * (m // tm), (n, m, tm)
        cache_spec = pl.BlockSpec((SAMPLE_ATT_BATCH, ATT_KV_WIDTH, WINDOW), lambda i: (i, 0, 0))
        in_specs += [_resident((n, n_groups, LANES), const3),
                     _resident((n, 2 * ATT_KV_WIDTH), const2),
                     cache_spec, cache_spec,
                     _resident((n, n_groups, LANES), const3),
                     _resident((ATT_HEADS, WINDOW), const2),
                     _resident((ATT_HEADS, LANES), const2)]
        out_specs += [pl.BlockSpec((tm, PADDED_KV_WIDTH), row),
                      pl.BlockSpec((n, n_groups, LANES), const3), cache_spec, cache_spec]
        out_shape += [jax.ShapeDtypeStruct((m, PADDED_KV_WIDTH), BF16),
                      jax.ShapeDtypeStruct((n, n_groups, LANES), F32),
                      jax.ShapeDtypeStruct((n, ATT_KV_WIDTH, WINDOW), F32),
                      jax.ShapeDtypeStruct((n, ATT_KV_WIDTH, WINDOW), F32)]
        args += list(sample)
        scratch_shapes = [pltpu.VMEM((2 * ATT_KV_WIDTH, n), F32)]
    else:
        assert m == tm, (m, tm)
        out_specs.append(pl.BlockSpec((D_MODEL, ATT_IN), const2))
        out_shape.append(jax.ShapeDtypeStruct((D_MODEL, ATT_IN), BF16))
    return pl.pallas_call(
        functools.partial(_attn_inproj_kernel, hosts_sample=sample is not None),
        grid=(m // tm,),
        in_specs=in_specs,
        out_specs=out_specs,
        out_shape=out_shape,
        scratch_shapes=scratch_shapes,
        compiler_params=_params(1),
        name="attn_inproj",
    )(*args)


def _ret_sample_pieces(step, qt_ref, kt_ref, v_ref, gate_ref, st_ref, og_ref, sto_ref):
    lane = lax.broadcasted_iota(jnp.int32, (RET_QK_DIM, LANES), 1)
    gammas = _gammas()

    def piece(i, h):
        b = step * SAMPLE_RET_BATCH + i
        pick = lane == b
        rows = slice(h * RET_QK_DIM, (h + 1) * RET_QK_DIM)
        qc = jnp.sum(jnp.where(pick, qt_ref[rows, :], 0.0), axis=1, keepdims=True)
        kc = jnp.sum(jnp.where(pick, kt_ref[rows, :], 0.0), axis=1, keepdims=True)
        vcols = slice(h * RET_V_DIM, (h + 1) * RET_V_DIM)
        vrow = v_ref[pl.ds(b, 1), vcols]
        state = st_ref[i, h]
        o = (jnp.sum(qc * state, axis=0, keepdims=True)
             + jnp.sum(qc * kc, axis=0, keepdims=True) * vrow)
        sto_ref[i, h] = gammas[h] * (state + kc * vrow)
        og_ref[pl.ds(b, 1), vcols] = _group_norm_gate(o, gate_ref[pl.ds(b, 1), vcols])

    return [functools.partial(piece, i, h)
            for i in range(SAMPLE_RET_BATCH) for h in range(RET_HEADS)]


def _ret_inproj_kernel(*refs, with_sample):
    x_ref, g_ref, w_ref, cos_ref, sin_ref, qsc_ref, ksc_ref = refs[:7]
    side_work = []
    if with_sample:
        qs_ref, ks_ref, vs_ref, gates_ref, st_ref = refs[7:12]
        q_ref, k_ref, v_ref, gate_ref, ogs_ref, sto_ref, qts_ref, kts_ref = refs[12:]
        step = pl.program_id(0)

        @pl.when(step == 0)
        def _():
            qts_ref[...] = qs_ref[...].T
            kts_ref[...] = ks_ref[...].T

        side_work = _ret_sample_pieces(step, qts_ref, kts_ref, vs_ref, gates_ref, st_ref,
                                       ogs_ref, sto_ref)
    else:
        q_ref, k_ref, v_ref, gate_ref, wcast_ref = refs[7:]

    def interleave():
        if side_work:
            side_work.pop(0)()

    def weights(c):
        wc = w_ref[:, c:c + PROJ_COLS]
        if not with_sample:
            wc = wc.astype(BF16)
            wcast_ref[:, c:c + PROJ_COLS] = wc
        return wc

    h = _rmsnorm(x_ref[...], g_ref[...]).astype(BF16)
    half = RET_QK_DIM // 2
    cos = cos_ref[...]
    sin = sin_ref[...]
    for out_ref, scale_ref, w0 in ((q_ref, qsc_ref, 0), (k_ref, ksc_ref, RET_QK_WIDTH)):
        for c in range(0, RET_QK_WIDTH, PROJ_COLS):
            interleave()
            r = jnp.dot(h, weights(w0 + c), preferred_element_type=F32)
            for o in range(0, PROJ_COLS, RET_QK_DIM):
                scale = scale_ref[(c + o) // RET_QK_DIM]
                x1 = r[:, o:o + half]
                x2 = r[:, o + half:o + RET_QK_DIM]
                out_ref[:, c + o:c + o + half] = (
                    (x1 * cos - x2 * sin) * scale).astype(out_ref.dtype)
                out_ref[:, c + o + half:c + o + RET_QK_DIM] = (
                    (x1 * sin + x2 * cos) * scale).astype(out_ref.dtype)
    for c in range(0, RET_WIDTH, PROJ_COLS):
        interleave()
        v_ref[:, c:c + PROJ_COLS] = jnp.dot(
            h, weights(2 * RET_QK_WIDTH + c), preferred_element_type=F32).astype(v_ref.dtype)
    for c in range(0, RET_WIDTH, PROJ_COLS):
        interleave()
        gate_ref[:, c:c + PROJ_COLS] = jnp.dot(
            h, weights(2 * RET_QK_WIDTH + RET_WIDTH + c),
            preferred_element_type=F32).astype(gate_ref.dtype)
    while side_work:
        interleave()


def _ret_inproj(x, g, w, pos, block, out_dtype, sample=None):
    m = x.shape[0]
    tm = min(PROJ_ROWS, m)
    n_pos = len(pos) // tm
    half = RET_QK_DIM // 2
    cos, sin = _rope_tables_np(pos)
    q_scale, k_scale, _ = _decay_scales_np(tm, block)
    row = lambda i: (i, 0)
    const2 = lambda i: (0, 0)
    const3 = lambda i: (0, 0, 0)
    in_specs = [pl.BlockSpec((tm, D_MODEL), row),
                pl.BlockSpec((1, D_MODEL), const2),
                _resident((D_MODEL, RET_IN), const2),
                pl.BlockSpec((tm, half), lambda i: (i % n_pos, 0)),
                pl.BlockSpec((tm, half), lambda i: (i % n_pos, 0)),
                _resident((RET_HEADS, tm, LANES), const3),
                _resident((RET_HEADS, tm, LANES), const3)]
    out_specs = [pl.BlockSpec((tm, RET_QK_WIDTH), row),
                 pl.BlockSpec((tm, RET_QK_WIDTH), row),
                 pl.BlockSpec((tm, RET_WIDTH), row),
                 pl.BlockSpec((tm, RET_WIDTH), row)]
    out_shape = [jax.ShapeDtypeStruct((m, RET_QK_WIDTH), out_dtype),
                 jax.ShapeDtypeStruct((m, RET_QK_WIDTH), out_dtype),
                 jax.ShapeDtypeStruct((m, RET_WIDTH), out_dtype),
                 jax.ShapeDtypeStruct((m, RET_WIDTH), out_dtype)]
    args = [x, g, w, jnp.asarray(cos), jnp.asarray(sin), jnp.asarray(q_scale), jnp.asarray(k_scale)]
    scratch_shapes = []
    if sample is not None:
        q_s, k_s, v_s, gate_s, state = sample
        n = q_s.shape[0]
        assert n == SAMPLE_RET_BATCH * (m // tm), (n, m, tm)
        st_spec = pl.BlockSpec((SAMPLE_RET_BATCH, RET_HEADS, RET_QK_DIM, RET_V_DIM),
                               lambda i: (i, 0, 0, 0))
        in_specs += [_resident((n, RET_QK_WIDTH), const2), _resident((n, RET_QK_WIDTH), const2),
                     _resident((n, RET_WIDTH), const2), _resident((n, RET_WIDTH), const2), st_spec]
        out_specs += [pl.BlockSpec((n, RET_WIDTH), const2), st_spec]
        out_shape += [jax.ShapeDtypeStruct((n, RET_WIDTH), F32),
                      jax.ShapeDtypeStruct(state.shape, F32)]
        args += [q_s, k_s, v_s, gate_s, state]
        scratch_shapes = [pltpu.VMEM((RET_QK_WIDTH, n), F32), pltpu.VMEM((RET_QK_WIDTH, n), F32)]
    else:
        assert m == tm, (m, tm)
        out_specs.append(pl.BlockSpec((D_MODEL, RET_IN), const2))
        out_shape.append(jax.ShapeDtypeStruct((D_MODEL, RET_IN), BF16))
    return pl.pallas_call(
        functools.partial(_ret_inproj_kernel, with_sample=sample is not None),
        grid=(m // tm,),
        in_specs=in_specs,
        out_specs=out_specs,
        out_shape=out_shape,
        scratch_shapes=scratch_shapes,
        compiler_params=_params(1),
        name="ret_inproj",
    )(*args)


def _outproj_kernel(o_ref, w_ref, x_ref, y_ref):
    y_ref[...] = x_ref[...] + jnp.dot(o_ref[...].astype(BF16), w_ref[...].astype(BF16),
                                      preferred_element_type=F32)


def _outproj_norm_kernel(o_ref, w_ref, x_ref, g_ref, y_ref):
    y = x_ref[...] + jnp.dot(o_ref[...].astype(BF16), w_ref[...].astype(BF16),
                             preferred_element_type=F32)
    y_ref[...] = _rmsnorm(y, g_ref[...])


def _outproj(o, w, x, final_g=None):
    m, width = o.shape
    tm = min(OUT_ROWS, m)
    row = lambda i: (i, 0)
    in_specs = [pl.BlockSpec((tm, width), row),
                _resident((width, D_MODEL), lambda i: (0, 0)),
                pl.BlockSpec((tm, D_MODEL), row)]
    args = [o, w, x]
    body = _outproj_kernel
    if final_g is not None:
        in_specs.append(pl.BlockSpec((1, D_MODEL), lambda i: (0, 0)))
        args.append(final_g)
        body = _outproj_norm_kernel
    return pl.pallas_call(
        body,
        grid=(m // tm,),
        in_specs=in_specs,
        out_specs=pl.BlockSpec((tm, D_MODEL), row),
        out_shape=jax.ShapeDtypeStruct((m, D_MODEL), F32),
        compiler_params=_params(1),
        name="outproj",
    )(*args)


def _swa_prompt_kernel(sink_ref, q_ref, kvc_ref, kvp_ref, gate_ref, bias_ref, out_ref):
    j = pl.program_id(1)
    half = ATT_HEAD_DIM
    pairs_per_kv = ATT_GROUP // 2
    kv_group_cols = ATT_GROUP * ATT_HEAD_DIM
    contract_last = (((1,), (1,)), ((), ()))
    lo1 = lax.broadcasted_iota(jnp.int32, (WINDOW, LANES), 1) < half
    own = (lax.broadcasted_iota(jnp.int32, (WINDOW, WINDOW), 1)
           <= lax.broadcasted_iota(jnp.int32, (WINDOW, WINDOW), 0))
    units =[(blk, kvh) for blk in range(ATT_ROWS // WINDOW) for kvh in range(ATT_KV_HEADS)]

    def keys_values(blk, kvh):
        r0 = blk * WINDOW
        out = []
        for variant in range(PADDED_KV_VARIANTS):
            c = (kvh * PADDED_KV_VARIANTS + variant) * LANES
            prev = kvp_ref[:, c:c + LANES] if blk == 0 else kvc_ref[r0 - WINDOW:r0, c:c + LANES]
            out.append(jnp.concatenate([prev, kvc_ref[r0:r0 + WINDOW, c:c + LANES]], axis=0))
        return out

    def scores(blk, kvh):
        rows = slice(blk * WINDOW, (blk + 1) * WINDOW)
        c0 = kvh * kv_group_cols
        k_a, k_b, v_a, v_b = keys_values(blk, kvh)
        qs = jnp.concatenate(
            [q_ref[rows, c0 + p * LANES:c0 + (p + 1) * LANES] for p in range(pairs_per_kv)],
            axis=0)
        logits = (lax.dot_general(qs, k_a, contract_last, preferred_element_type=F32),
                  lax.dot_general(qs, k_b, contract_last, preferred_element_type=F32))
        return logits, v_a, v_b

    def softmax(blk, kvh, logits):
        sel = jnp.where(j == 0, 0, 1) if blk == 0 else 1
        probs = ([], [])
        inv = []
        for p in range(pairs_per_kv):
            top, total, sink = [], [], []
            for st in range(2):
                hd = kvh * ATT_GROUP + 2 * p + st
                lg = logits[st][p * WINDOW:(p + 1) * WINDOW, :]
                l = jnp.where(own, lg[:, WINDOW:], lg[:, :WINDOW]) + bias_ref[sel, hd]
                s = sink_ref[hd]
                m = jnp.maximum(jnp.max(l, axis=-1, keepdims=True), s)
                pe = jnp.exp2(l - m)
                probs[st].append(jnp.concatenate(
                    [jnp.where(own, 0.0, pe).astype(BF16),
                     jnp.where(own, pe, 0.0).astype(BF16)], axis=1))
                top.append(m)
                total.append(jnp.sum(pe, axis=-1, keepdims=True))
                sink.append(s)
            sink_term = jnp.exp2(jnp.where(lo1, sink[0], sink[1]) - jnp.where(lo1, top[0], top[1]))
            inv.append(1.0 / (jnp.where(lo1, total[0], total[1]) + sink_term))
        return jnp.concatenate(probs[0], axis=0), jnp.concatenate(probs[1], axis=0), inv

    def output(blk, kvh, p_a, p_b, inv, v_a, v_b):
        rows = slice(blk * WINDOW, (blk + 1) * WINDOW)
        c0 = kvh * kv_group_cols
        o = (jnp.dot(p_a, v_a, preferred_element_type=F32)
             + jnp.dot(p_b, v_b, preferred_element_type=F32))
        for p in range(pairs_per_kv):
            cols = slice(c0 + p * LANES, c0 + (p + 1) * LANES)
            gt = gate_ref[rows, cols].astype(F32)
            og = o[p * WINDOW:(p + 1) * WINDOW, :] * inv[p] * gt
            out_ref[rows, cols] = og.astype(out_ref.dtype)

    for unit in units:
        logits, v_a, v_b = scores(*unit)
        p_a, p_b, inv = softmax(*unit, logits)
        output(*unit, p_a, p_b, inv, v_a, v_b)


def _swa_prompt(sinks, q, kvpad, gate, bias, batch, seq):
    m = batch * seq
    nt = seq // ATT_ROWS
    blocks_per_tile = ATT_ROWS // WINDOW
    blocks_per_seq = seq // WINDOW
    tile = lambda b, j: (b * nt + j, 0)
    prev = lambda b, j: (b * blocks_per_seq + jnp.maximum(j * blocks_per_tile - 1, 0), 0)
    return pl.pallas_call(
        _swa_prompt_kernel,
        grid=(batch, nt),
        in_specs=[pl.BlockSpec(memory_space=pltpu.SMEM),
                  pl.BlockSpec((ATT_ROWS, ATT_WIDTH), tile),
                  pl.BlockSpec((ATT_ROWS, PADDED_KV_WIDTH), tile),
                  pl.BlockSpec((WINDOW, PADDED_KV_WIDTH), prev),
                  pl.BlockSpec((ATT_ROWS, ATT_WIDTH), tile),
                  pl.BlockSpec((2, ATT_HEADS, WINDOW, WINDOW), lambda b, j: (0, 0, 0, 0))],
        out_specs=pl.BlockSpec((ATT_ROWS, ATT_WIDTH), tile),
        out_shape=jax.ShapeDtypeStruct((m, ATT_WIDTH), BF16),
        compiler_params=_params(2),
        name="swa_prompt",
    )(sinks, q, kvpad, kvpad, gate, bias)


def _sample_head_order():
    order = []
    for pair in range(ATT_KV_HEADS // 2):
        base = pair * 2 * ATT_GROUP
        for kv in range(2):
            for half in range(2):
                order += [base + kv * ATT_GROUP + 2 * g + half for g in range(ATT_GROUP // 2)]
    return np.asarray(order, np.int32)


def _swa_sample_pieces(step, q_ref, kvt_ref, ckt_ref, cvt_ref, gate_ref, bias_ref, sink_ref,
                       o_ref, cko_ref, cvo_ref):
    half = ATT_HEAD_DIM
    quarter = ATT_GROUP // 2
    contract_last = (((1,), (1,)), ((), ()))
    newest = lax.broadcasted_iota(jnp.int32, (ATT_KV_WIDTH, WINDOW), 1) == WINDOW - 1
    seq_lane = lax.broadcasted_iota(jnp.int32, (2 * ATT_KV_WIDTH, LANES), 1)
    lo4 = lax.broadcasted_iota(jnp.int32, (quarter, LANES), 1) < half
    new_cache = {}

    def update(i):
        b = step * SAMPLE_ATT_BATCH + i
        new_col = jnp.sum(jnp.where(seq_lane == b, kvt_ref[...], 0.0), axis=1, keepdims=True)
        kb = jnp.where(newest, new_col[:ATT_KV_WIDTH], pltpu.roll(ckt_ref[i], WINDOW - 1, axis=1))
        vb = jnp.where(newest, new_col[ATT_KV_WIDTH:], pltpu.roll(cvt_ref[i], WINDOW - 1, axis=1))
        cko_ref[i] = kb
        cvo_ref[i] = vb
        new_cache[i] = (kb, vb)

    def attend(i, pair):
        b = step * SAMPLE_ATT_BATCH + i
        kb, vb = new_cache[i]
        groups = slice(pair * ATT_GROUP, (pair + 1) * ATT_GROUP)
        hs = slice(pair * 2 * ATT_GROUP, (pair + 1) * 2 * ATT_GROUP)
        feat = slice(pair * LANES, (pair + 1) * LANES)
        g8 = q_ref[b, groups, :]
        g8r = pltpu.roll(g8, half, axis=1)
        qbd = jnp.concatenate(
            [jnp.where(lo4, g8[:quarter], 0.0), jnp.where(lo4, g8r[:quarter], 0.0),
             jnp.where(lo4, 0.0, g8r[quarter:]), jnp.where(lo4, 0.0, g8[quarter:])],
            axis=0)
        l = jnp.dot(qbd, kb[feat, :], preferred_element_type=F32) + bias_ref[hs, :]
        s = sink_ref[hs, 0:1]
        m = jnp.maximum(jnp.max(l, axis=-1, keepdims=True), s)
        pe = jnp.exp2(l - m)
        den = jnp.sum(pe, axis=-1, keepdims=True) + jnp.exp2(s - m)
        o2 = lax.dot_general(pe, vb[feat, :], contract_last,
                             preferred_element_type=F32) / den
        o2r = pltpu.roll(o2, half, axis=1)
        og = jnp.concatenate(
            [jnp.where(lo4, o2[:quarter], o2r[quarter:2 * quarter]),
             jnp.where(lo4, o2r[2 * quarter:3 * quarter], o2[3 * quarter:])], axis=0)
        o_ref[b, groups, :] = og * gate_ref[b, groups, :]

    pieces = []
    for i in range(SAMPLE_ATT_BATCH):
        pieces.append(functools.partial(update, i))
        pieces += [functools.partial(attend, i, pair) for pair in range(ATT_KV_HEADS // 2)]
    return pieces


def _group_norm_gate(o, gate):
    mu = jnp.mean(o, axis=-1, keepdims=True)
    d = o - mu
    var = jnp.mean(d * d, axis=-1, keepdims=True)
    return d * lax.rsqrt(var + EPS) * _silu(gate)


def _ret_prompt_kernel(g_block_ref, q_ref, k_ref, v_ref, gate_ref, x_ref, wout_ref, gf_ref,
                       y_ref, r_ref, og_ref):
    c = pl.program_id(1)

    @pl.when(c == 0)
    def _():
        r_ref[...] = jnp.zeros_like(r_ref)

    causal = (lax.broadcasted_iota(jnp.int32, (RET_BLOCK, RET_BLOCK), 1)
              <= lax.broadcasted_iota(jnp.int32, (RET_BLOCK, RET_BLOCK), 0))
    contract_last = (((1,), (1,)), ((), ()))
    n_blocks = RET_ROWS // RET_BLOCK
    for blk in range(n_blocks):
        rows = slice(blk * RET_BLOCK, (blk + 1) * RET_BLOCK)
        for h in range(RET_HEADS):
            qcols = slice(h * RET_QK_DIM, (h + 1) * RET_QK_DIM)
            vcols = slice(h * RET_V_DIM, (h + 1) * RET_V_DIM)
            qh = q_ref[rows, qcols]
            kh = k_ref[rows, qcols]
            vh = v_ref[rows, vcols]
            state = r_ref[0, h]
            scores = jnp.where(
                causal, lax.dot_general(qh, kh, contract_last, preferred_element_type=F32), 0.0)
            o = (jnp.dot(scores.astype(BF16), vh, preferred_element_type=F32)
                 + jnp.dot(qh, state.astype(BF16), preferred_element_type=F32))
            r_ref[0, h] = g_block_ref[h] * (
                state + jnp.dot(kh.T, vh, preferred_element_type=F32))
            og_ref[h, rows, :] = _group_norm_gate(
                o, gate_ref[rows, vcols].astype(F32)).astype(og_ref.dtype)
            if blk == n_blocks - 1:
                part = jnp.dot(og_ref[h], wout_ref[vcols, :].astype(BF16),
                               preferred_element_type=F32)
                if h == 0:
                    y_ref[...] = x_ref[...] + part
                elif h < RET_HEADS - 1:
                    y_ref[...] += part
                else:
                    y_ref[...] = _rmsnorm(y_ref[...] + part, gf_ref[...])


def _ret_prompt(q, k, v, gate, x, w_out, final_g, batch, seq):
    m = batch * seq
    nt = seq // RET_ROWS
    g_block = jnp.asarray(_decay_scales_np(RET_BLOCK, RET_BLOCK)[2])
    rows = lambda b, c: (b * nt + c, 0)
    return pl.pallas_call(
        _ret_prompt_kernel,
        grid=(batch, nt),
        in_specs=[pl.BlockSpec(memory_space=pltpu.SMEM),
                  pl.BlockSpec((RET_ROWS, RET_QK_WIDTH), rows),
                  pl.BlockSpec((RET_ROWS, RET_QK_WIDTH), rows),
                  pl.BlockSpec((RET_ROWS, RET_WIDTH), rows),
                  pl.BlockSpec((RET_ROWS, RET_WIDTH), rows),
                  pl.BlockSpec((RET_ROWS, D_MODEL), rows),
                  _resident((RET_WIDTH, D_MODEL), lambda b, c: (0, 0)),
                  pl.BlockSpec((1, D_MODEL), lambda b, c: (0, 0))],
        out_specs=[pl.BlockSpec((RET_ROWS, D_MODEL), rows),
                   pl.BlockSpec((1, RET_HEADS, RET_QK_DIM, RET_V_DIM), lambda b, c: (b, 0, 0, 0))],
        out_shape=[jax.ShapeDtypeStruct((m, D_MODEL), F32),
                   jax.ShapeDtypeStruct((batch, RET_HEADS, RET_QK_DIM, RET_V_DIM), F32)],
        scratch_shapes=[pltpu.VMEM((RET_HEADS, RET_ROWS, RET_V_DIM), BF16)],
        compiler_params=_params(2),
        name="ret_prompt",
    )(g_block, q, k, v, gate, x, w_out, final_g)


def kernel(x_prompt, x_sample, cache_swa_k, cache_swa_v, state_ret, norm_g, final_norm_g, rel_bias,
           w_in_attn, attn_sinks, w_out_attn, w_in_ret, w_out_ret):
    batch, seq, _ = x_prompt.shape
    n_s = x_sample.shape[0]
    xp = x_prompt.reshape(batch * seq, D_MODEL)
    xs = x_sample.reshape(n_s, D_MODEL)
    g0 = norm_g[0].reshape(1, D_MODEL)
    g1 = norm_g[1].reshape(1, D_MODEL)
    gf = final_norm_g.reshape(1, D_MODEL)
    w_out_a = w_out_attn[0]
    w_out_r = w_out_ret[0]
    sinks = attn_sinks[0] * LOG2E

    bias = _bias_table(rel_bias)

    qs, kvs, gates, w_in_a = _attn_inproj(xs, g0, w_in_attn[0], F32)
    order = _sample_head_order()
    bias_s = bias[1, :, WINDOW - 1, :][order]
    sinks_rep = jnp.broadcast_to(sinks[order][:, None], (ATT_HEADS, LANES))
    to_feature_major = lambda c: jnp.transpose(c, (0, 2, 3, 1)).reshape(n_s, ATT_KV_WIDTH, WINDOW)
    to_window_major = lambda c: jnp.transpose(
        c.reshape(n_s, ATT_KV_HEADS, ATT_HEAD_DIM, WINDOW), (0, 3, 1, 2))[None]
    n_groups = ATT_WIDTH // LANES
    q, kv, gate, kvpad, ogs, ckt_new, cvt_new = _attn_inproj(
        xp, g0, w_in_a, BF16,
        sample=(qs.reshape(n_s, n_groups, LANES), kvs,
                to_feature_major(cache_swa_k[0]), to_feature_major(cache_swa_v[0]),
                gates.reshape(n_s, n_groups, LANES), bias_s, sinks_rep))
    og = _swa_prompt(sinks, q, kvpad, gate, bias, batch, seq)
    xp1 = _outproj(og, w_out_a, xp)
    kv_last = kv.reshape(batch, seq, 2 * ATT_KV_WIDTH)[:, seq - WINDOW:]
    k_prompt = kv_last[..., :ATT_KV_WIDTH].reshape(1, batch, WINDOW, ATT_KV_HEADS, ATT_HEAD_DIM)
    v_prompt = kv_last[..., ATT_KV_WIDTH:].reshape(1, batch, WINDOW, ATT_KV_HEADS, ATT_HEAD_DIM)
    xs1 = _outproj(ogs.reshape(n_s, ATT_WIDTH), w_out_a, xs)
    k_sample = to_window_major(ckt_new)
    v_sample = to_window_major(cvt_new)

    q_s, k_s, v_s, gate_s, w_in_r = _ret_inproj(
        xs1, g1, w_in_ret[0], np.full((n_s,), PAST_LEN), 1, F32)
    q, k, v, gate, og_s, r_sample = _ret_inproj(
        xp1, g1, w_in_r, np.arange(seq), RET_BLOCK, BF16,
        sample=(q_s, k_s, v_s, gate_s, state_ret[0]))
    y_prompt, r_prompt = _ret_prompt(q, k, v, gate, xp1, w_out_r, gf, batch, seq)
    y_prompt = y_prompt.reshape(batch, seq, D_MODEL)
    y_sample = _outproj(og_s, w_out_r, xs1, gf).reshape(n_s, 1, D_MODEL)

    return (y_prompt, y_sample, k_prompt, v_prompt, r_prompt[None],
            k_sample, v_sample, r_sample[None])
```

```python
import functools
import math

import numpy as np
import jax
import jax.numpy as jnp
from jax import lax
from jax.experimental import pallas as pl
from jax.experimental.pallas import tpu as pltpu

F32 = jnp.float32
BF16 = jnp.bfloat16

D_MODEL = 1024
PAST_LEN = 16384
ATT_HEADS = 32
ATT_KV_HEADS = 4
ATT_GROUP = ATT_HEADS // ATT_KV_HEADS
ATT_HEAD_DIM = 64
ATT_WIDTH = ATT_HEADS * ATT_HEAD_DIM
ATT_KV_WIDTH = ATT_KV_HEADS * ATT_HEAD_DIM
ATT_IN = 2 * ATT_WIDTH + 2 * ATT_KV_WIDTH
WINDOW = 128
NUM_BUCKETS = 32
MAX_DISTANCE = 128
RET_HEADS = 4
RET_QK_DIM = 256
RET_V_DIM = 512
RET_QK_WIDTH = RET_HEADS * RET_QK_DIM
RET_WIDTH = RET_HEADS * RET_V_DIM
RET_IN = 2 * RET_QK_WIDTH + 2 * RET_WIDTH
ROPE_BASE = 10000.0
EPS = 1e-6
NEG = -1e30
LOG2E = math.log2(math.e)

LANES = 128
PADDED_KV_VARIANTS = 4
PADDED_KV_WIDTH = ATT_KV_HEADS * PADDED_KV_VARIANTS * LANES
PROJ_ROWS = 512
ATT_PROJ_ROWS = 1024
PROJ_COLS = 512
OUT_ROWS = 1024
ATT_ROWS = 1024
PAIRS_PER_DOT = 2
RET_BLOCK = 256
RET_ROWS = 1024
SAMPLE_ATT_BATCH = 4
SAMPLE_RET_BATCH = 2
VMEM_LIMIT = 60 * 1024 * 1024


def _params(n_axes):
    return pltpu.CompilerParams(
        dimension_semantics=("arbitrary",) * n_axes,
        vmem_limit_bytes=VMEM_LIMIT)


def _resident(shape, index_map):
    return pl.BlockSpec(shape, index_map, pipeline_mode=pl.Buffered(1))


def _silu(x):
    hx = 0.5 * x
    return hx + hx * jnp.tanh(hx)


def _rmsnorm(x, g):
    return x * lax.rsqrt(jnp.mean(x * x, axis=-1, keepdims=True) + EPS) * g


def _t5_bucket_np(rel):
    n = np.maximum(rel, 0)
    max_exact = NUM_BUCKETS // 2
    nf = np.maximum(n, 1).astype(np.float64)
    large = max_exact + (np.log(nf / max_exact) / math.log(MAX_DISTANCE / max_exact)
                         * (NUM_BUCKETS - max_exact)).astype(np.int32)
    large = np.minimum(large, NUM_BUCKETS - 1)
    return np.where(n < max_exact, n, large).astype(np.int32)


def _bucket_map_np():
    i = np.arange(WINDOW)[:, None]
    j = np.arange(WINDOW)[None, :]
    rel = np.where(j <= i, i - j, i - j + WINDOW)
    return _t5_bucket_np(rel).astype(np.int32)


def _gammas():
    return [1.0 - 2.0 ** (-5 - h) for h in range(RET_HEADS)]


def _decay_scales_np(rows, block):
    lg = np.log1p(-np.exp2(-5.0 - np.arange(RET_HEADS, dtype=np.float64)))
    i1 = (np.arange(rows) % block + 1.0)[None, :] * lg[:, None]
    q_scale = np.repeat(np.exp(i1)[:, :, None], LANES, axis=2)
    k_scale = np.repeat((np.exp(-i1) * RET_QK_DIM ** -0.5)[:, :, None], LANES, axis=2)
    return (q_scale.astype(np.float32), k_scale.astype(np.float32),
            np.exp(block * lg).astype(np.float32))


def _rope_tables_np(pos):
    half = RET_QK_DIM // 2
    inv = ROPE_BASE ** (-np.arange(half, dtype=np.float64) / half)
    ang = np.asarray(pos, np.float64)[:, None] * inv[None, :]
    return np.cos(ang).astype(np.float32), np.sin(ang).astype(np.float32)


def _bias_table_kernel(rb_ref, bucket_ref, out_ref):
    bk = bucket_ref[...]
    previous_block = (lax.broadcasted_iota(jnp.int32, bk.shape, 1)
                      > lax.broadcasted_iota(jnp.int32, bk.shape, 0))

    def one_head(h, carry):
        acc = jnp.zeros(bk.shape, F32)
        for b in range(NUM_BUCKETS):
            acc = jnp.where(bk == b, rb_ref[b, h] * LOG2E, acc)
        out_ref[1, h] = acc
        out_ref[0, h] = jnp.where(previous_block, NEG, acc)
        return carry

    lax.fori_loop(0, ATT_HEADS, one_head, 0)


def _bias_table(rel_bias):
    bucket = jnp.asarray(_bucket_map_np())
    return pl.pallas_call(
        _bias_table_kernel,
        grid=(1,),
        in_specs=[pl.BlockSpec(memory_space=pltpu.SMEM),
                  pl.BlockSpec((WINDOW, WINDOW), lambda i: (0, 0))],
        out_specs=pl.BlockSpec((2, ATT_HEADS, WINDOW, WINDOW), lambda i: (0, 0, 0, 0)),
        out_shape=jax.ShapeDtypeStruct((2, ATT_HEADS, WINDOW, WINDOW), F32),
        compiler_params=_params(1),
        name="bias_table",
    )(rel_bias, bucket)


def _pad_pair(x, x_rolled, lo, e):
    if e == 0:
        return jnp.where(lo, x, 0.0), jnp.where(lo, 0.0, x_rolled)
    return jnp.where(lo, x_rolled, 0.0), jnp.where(lo, 0.0, x)


def _attn_inproj_kernel(*refs, hosts_sample):
    x_ref, g_ref, w_ref = refs[:3]
    side_work = []
    kvpad_ref = None
    if hosts_sample:
        qs_ref, kvs_ref, ckt_ref, cvt_ref, gates_ref, bias_ref, sink_ref = refs[3:10]
        q_ref, kv_ref, gate_ref, kvpad_ref, ogs_ref, cko_ref, cvo_ref, kvt_ref = refs[10:]
        step = pl.program_id(0)

        @pl.when(step == 0)
        def _():
            kvt_ref[...] = kvs_ref[...].T

        side_work = _swa_sample_pieces(step, qs_ref, kvt_ref, ckt_ref, cvt_ref, gates_ref,
                                       bias_ref, sink_ref, ogs_ref, cko_ref, cvo_ref)
    else:
        q_ref, kv_ref, gate_ref, wcast_ref = refs[3:]

    h = _rmsnorm(x_ref[...], g_ref[...]).astype(BF16)
    for c in range(0, ATT_IN, PROJ_COLS):
        if side_work:
            side_work.pop(0)()
        wc = w_ref[:, c:c + PROJ_COLS]
        if not hosts_sample:
            wc = wc.astype(BF16)
            wcast_ref[:, c:c + PROJ_COLS] = wc
        r = jnp.dot(h, wc, preferred_element_type=F32)
        if c < ATT_WIDTH:
            q_ref[:, c:c + PROJ_COLS] = (r * (ATT_HEAD_DIM ** -0.5 * LOG2E)).astype(q_ref.dtype)
        elif c < ATT_WIDTH + 2 * ATT_KV_WIDTH:
            kv_ref[...] = r
            if kvpad_ref is not None:
                lo = lax.broadcasted_iota(jnp.int32, (r.shape[0], LANES), 1) < ATT_HEAD_DIM
                for pair in range(ATT_KV_HEADS // 2):
                    for is_v in range(2):
                        c1 = is_v * ATT_KV_WIDTH + pair * LANES
                        x2 = r[:, c1:c1 + LANES]
                        x2r = pltpu.roll(x2, ATT_HEAD_DIM, axis=1)
                        for e in range(2):
                            g0 = ((2 * pair + e) * PADDED_KV_VARIANTS + 2 * is_v) * LANES
                            lo_half, hi_half = _pad_pair(x2, x2r, lo, e)
                            kvpad_ref[:, g0:g0 + LANES] = lo_half.astype(kvpad_ref.dtype)
                            kvpad_ref[:, g0 + LANES:g0 + 2 * LANES] = hi_half.astype(kvpad_ref.dtype)
        else:
            o = c - ATT_WIDTH - 2 * ATT_KV_WIDTH
            gate_ref[:, o:o + PROJ_COLS] = _silu(r).astype(gate_ref.dtype)
    while side_work:
        side_work.pop(0)()


def _attn_inproj(x, g, w, out_dtype, sample=None):
    m = x.shape[0]
    tm = min(ATT_PROJ_ROWS, m)
    row = lambda i: (i, 0)
    const2 = lambda i: (0, 0)
    const3 = lambda i: (0, 0, 0)
    in_specs = [pl.BlockSpec((tm, D_MODEL), row),
                pl.BlockSpec((1, D_MODEL), const2),
                _resident((D_MODEL, ATT_IN), const2)]
    out_specs = [pl.BlockSpec((tm, ATT_WIDTH), row),
                 pl.BlockSpec((tm, 2 * ATT_KV_WIDTH), row),
                 pl.BlockSpec((tm, ATT_WIDTH), row)]
    out_shape = [jax.ShapeDtypeStruct((m, ATT_WIDTH), out_dtype),
                 jax.ShapeDtypeStruct((m, 2 * ATT_KV_WIDTH), F32),
                 jax.ShapeDtypeStruct((m, ATT_WIDTH), out_dtype)]
    args = [x, g, w]
    scratch_shapes = []
    if sample is not None:
        q_s, kv_s, ckt, cvt, gate_s, bias_s, sinks_rep = sample
        n, n_groups, _ = q_s.shape
        assert n == SAMPLE_ATT_BATCH * (m // tm), (n, m, tm)
        cache_spec = pl.BlockSpec((SAMPLE_ATT_BATCH, ATT_KV_WIDTH, WINDOW), lambda i: (i, 0, 0))
        in_specs += [_resident((n, n_groups, LANES), const3),
                     _resident((n, 2 * ATT_KV_WIDTH), const2),
                     cache_spec, cache_spec,
                     _resident((n, n_groups, LANES), const3),
                     _resident((ATT_HEADS, WINDOW), const2),
                     _resident((ATT_HEADS, LANES), const2)]
        out_specs += [pl.BlockSpec((tm, PADDED_KV_WIDTH), row),
                      pl.BlockSpec((n, n_groups, LANES), const3), cache_spec, cache_spec]
        out_shape += [jax.ShapeDtypeStruct((m, PADDED_KV_WIDTH), BF16),
                      jax.ShapeDtypeStruct((n, n_groups, LANES), F32),
                      jax.ShapeDtypeStruct((n, ATT_KV_WIDTH, WINDOW), F32),
                      jax.ShapeDtypeStruct((n, ATT_KV_WIDTH, WINDOW), F32)]
        args += list(sample)
        scratch_shapes = [pltpu.VMEM((2 * ATT_KV_WIDTH, n), F32)]
    else:
        assert m == tm, (m, tm)
        out_specs.append(pl.BlockSpec((D_MODEL, ATT_IN), const2))
        out_shape.append(jax.ShapeDtypeStruct((D_MODEL, ATT_IN), BF16))
    return pl.pallas_call(
        functools.partial(_attn_inproj_kernel, hosts_sample=sample is not None),
        grid=(m // tm,),
        in_specs=in_specs,
        out_specs=out_specs,
        out_shape=out_shape,
        scratch_shapes=scratch_shapes,
        compiler_params=_params(1),
        name="attn_inproj",
    )(*args)


def _ret_sample_pieces(step, qt_ref, kt_ref, v_ref, gate_ref, st_ref, og_ref, sto_ref):
    lane = lax.broadcasted_iota(jnp.int32, (RET_QK_DIM, LANES), 1)
    gammas = _gammas()

    def piece(i, h):
        b = step * SAMPLE_RET_BATCH + i
        pick = lane == b
        rows = slice(h * RET_QK_DIM, (h + 1) * RET_QK_DIM)
        qc = jnp.sum(jnp.where(pick, qt_ref[rows, :], 0.0), axis=1, keepdims=True)
        kc = jnp.sum(jnp.where(pick, kt_ref[rows, :], 0.0), axis=1, keepdims=True)
        vcols = slice(h * RET_V_DIM, (h + 1) * RET_V_DIM)
        vrow = v_ref[pl.ds(b, 1), vcols]
        state = st_ref[i, h]
        o = (jnp.sum(qc * state, axis=0, keepdims=True)
             + jnp.sum(qc * kc, axis=0, keepdims=True) * vrow)
        sto_ref[i, h] = gammas[h] * (state + kc * vrow)
        og_ref[pl.ds(b, 1), vcols] = _group_norm_gate(o, gate_ref[pl.ds(b, 1), vcols])

    return [functools.partial(piece, i, h)
            for i in range(SAMPLE_RET_BATCH) for h in range(RET_HEADS)]


def _ret_inproj_kernel(*refs, with_sample):
    x_ref, g_ref, w_ref, cos_ref, sin_ref, qsc_ref, ksc_ref = refs[:7]
    side_work = []
    if with_sample:
        qs_ref, ks_ref, vs_ref, gates_ref, st_ref = refs[7:12]
        q_ref, k_ref, v_ref, gate_ref, ogs_ref, sto_ref, qts_ref, kts_ref = refs[12:]
        step = pl.program_id(0)

        @pl.when(step == 0)
        def _():
            qts_ref[...] = qs_ref[...].T
            kts_ref[...] = ks_ref[...].T

        side_work = _ret_sample_pieces(step, qts_ref, kts_ref, vs_ref, gates_ref, st_ref,
                                       ogs_ref, sto_ref)
    else:
        q_ref, k_ref, v_ref, gate_ref, wcast_ref = refs[7:]

    def interleave():
        if side_work:
            side_work.pop(0)()

    def weights(c):
        wc = w_ref[:, c:c + PROJ_COLS]
        if not with_sample:
            wc = wc.astype(BF16)
            wcast_ref[:, c:c + PROJ_COLS] = wc
        return wc

    h = _rmsnorm(x_ref[...], g_ref[...]).astype(BF16)
    half = RET_QK_DIM // 2
    cos = cos_ref[...]
    sin = sin_ref[...]
    for out_ref, scale_ref, w0 in ((q_ref, qsc_ref, 0), (k_ref, ksc_ref, RET_QK_WIDTH)):
        for c in range(0, RET_QK_WIDTH, PROJ_COLS):
            interleave()
            r = jnp.dot(h, weights(w0 + c), preferred_element_type=F32)
            for o in range(0, PROJ_COLS, RET_QK_DIM):
                scale = scale_ref[(c + o) // RET_QK_DIM]
                x1 = r[:, o:o + half]
                x2 = r[:, o + half:o + RET_QK_DIM]
                out_ref[:, c + o:c + o + half] = (
                    (x1 * cos - x2 * sin) * scale).astype(out_ref.dtype)
                out_ref[:, c + o + half:c + o + RET_QK_DIM] = (
                    (x1 * sin + x2 * cos) * scale).astype(out_ref.dtype)
    for c in range(0, RET_WIDTH, PROJ_COLS):
        interleave()
        v_ref[:, c:c + PROJ_COLS] = jnp.dot(
            h, weights(2 * RET_QK_WIDTH + c), preferred_element_type=F32).astype(v_ref.dtype)
    for c in range(0, RET_WIDTH, PROJ_COLS):
        interleave()
        gate_ref[:, c:c + PROJ_COLS] = jnp.dot(
            h, weights(2 * RET_QK_WIDTH + RET_WIDTH + c),
            preferred_element_type=F32).astype(gate_ref.dtype)
    while side_work:
        interleave()


def _ret_inproj(x, g, w, pos, block, out_dtype, sample=None):
    m = x.shape[0]
    tm = min(PROJ_ROWS, m)
    n_pos = len(pos) // tm
    half = RET_QK_DIM // 2
    cos, sin = _rope_tables_np(pos)
    q_scale, k_scale, _ = _decay_scales_np(tm, block)
    row = lambda i: (i, 0)
    const2 = lambda i: (0, 0)
    const3 = lambda i: (0, 0, 0)
    in_specs = [pl.BlockSpec((tm, D_MODEL), row),
                pl.BlockSpec((1, D_MODEL), const2),
                _resident((D_MODEL, RET_IN), const2),
                pl.BlockSpec((tm, half), lambda i: (i % n_pos, 0)),
                pl.BlockSpec((tm, half), lambda i: (i % n_pos, 0)),
                _resident((RET_HEADS, tm, LANES), const3),
                _resident((RET_HEADS, tm, LANES), const3)]
    out_specs = [pl.BlockSpec((tm, RET_QK_WIDTH), row),
                 pl.BlockSpec((tm, RET_QK_WIDTH), row),
                 pl.BlockSpec((tm, RET_WIDTH), row),
                 pl.BlockSpec((tm, RET_WIDTH), row)]
    out_shape = [jax.ShapeDtypeStruct((m, RET_QK_WIDTH), out_dtype),
                 jax.ShapeDtypeStruct((m, RET_QK_WIDTH), out_dtype),
                 jax.ShapeDtypeStruct((m, RET_WIDTH), out_dtype),
                 jax.ShapeDtypeStruct((m, RET_WIDTH), out_dtype)]
    args = [x, g, w, jnp.asarray(cos), jnp.asarray(sin), jnp.asarray(q_scale), jnp.asarray(k_scale)]
    scratch_shapes = []
    if sample is not None:
        q_s, k_s, v_s, gate_s, state = sample
        n = q_s.shape[0]
        assert n == SAMPLE_RET_BATCH * (m // tm), (n, m, tm)
        st_spec = pl.BlockSpec((SAMPLE_RET_BATCH, RET_HEADS, RET_QK_DIM, RET_V_DIM),
                               lambda i: (i, 0, 0, 0))
        in_specs += [_resident((n, RET_QK_WIDTH), const2), _resident((n, RET_QK_WIDTH), const2),
                     _resident((n, RET_WIDTH), const2), _resident((n, RET_WIDTH), const2), st_spec]
        out_specs += [pl.BlockSpec((n, RET_WIDTH), const2), st_spec]
        out_shape += [jax.ShapeDtypeStruct((n, RET_WIDTH), F32),
                      jax.ShapeDtypeStruct(state.shape, F32)]
        args += [q_s, k_s, v_s, gate_s, state]
        scratch_shapes = [pltpu.VMEM((RET_QK_WIDTH, n), F32), pltpu.VMEM((RET_QK_WIDTH, n), F32)]
    else:
        assert m == tm, (m, tm)
        out_specs.append(pl.BlockSpec((D_MODEL, RET_IN), const2))
        out_shape.append(jax.ShapeDtypeStruct((D_MODEL, RET_IN), BF16))
    return pl.pallas_call(
        functools.partial(_ret_inproj_kernel, with_sample=sample is not None),
        grid=(m // tm,),
        in_specs=in_specs,
        out_specs=out_specs,
        out_shape=out_shape,
        scratch_shapes=scratch_shapes,
        compiler_params=_params(1),
        name="ret_inproj",
    )(*args)


def _outproj_kernel(o_ref, w_ref, x_ref, y_ref):
    y_ref[...] = x_ref[...] + jnp.dot(o_ref[...].astype(BF16), w_ref[...].astype(BF16),
                                      preferred_element_type=F32)


def _outproj_norm_kernel(o_ref, w_ref, x_ref, g_ref, y_ref):
    y = x_ref[...] + jnp.dot(o_ref[...].astype(BF16), w_ref[...].astype(BF16),
                             preferred_element_type=F32)
    y_ref[...] = _rmsnorm(y, g_ref[...])


def _outproj(o, w, x, final_g=None):
    m, width = o.shape
    tm = min(OUT_ROWS, m)
    row = lambda i: (i, 0)
    in_specs = [pl.BlockSpec((tm, width), row),
                _resident((width, D_MODEL), lambda i: (0, 0)),
                pl.BlockSpec((tm, D_MODEL), row)]
    args = [o, w, x]
    body = _outproj_kernel
    if final_g is not None:
        in_specs.append(pl.BlockSpec((1, D_MODEL), lambda i: (0, 0)))
        args.append(final_g)
        body = _outproj_norm_kernel
    return pl.pallas_call(
        body,
        grid=(m // tm,),
        in_specs=in_specs,
        out_specs=pl.BlockSpec((tm, D_MODEL), row),
        out_shape=jax.ShapeDtypeStruct((m, D_MODEL), F32),
        compiler_params=_params(1),
        name="outproj",
    )(*args)


def _swa_prompt_kernel(sink_ref, q_ref, kvc_ref, kvp_ref, gate_ref, bias_ref, out_ref):
    j = pl.program_id(1)
    half = ATT_HEAD_DIM
    pairs_per_kv = ATT_GROUP // 2
    kv_group_cols = ATT_GROUP * ATT_HEAD_DIM
    contract_last = (((1,), (1,)), ((), ()))
    lo1 = lax.broadcasted_iota(jnp.int32, (WINDOW, LANES), 1) < half
    own = (lax.broadcasted_iota(jnp.int32, (WINDOW, WINDOW), 1)
           <= lax.broadcasted_iota(jnp.int32, (WINDOW, WINDOW), 0))
    units = [(blk, kvh, p0) for blk in range(ATT_ROWS // WINDOW) for kvh in range(ATT_KV_HEADS)
             for p0 in range(0, pairs_per_kv, PAIRS_PER_DOT)]

    def keys_values(blk, kvh):
        r0 = blk * WINDOW
        out = []
        for variant in range(PADDED_KV_VARIANTS):
            c = (kvh * PADDED_KV_VARIANTS + variant) * LANES
            prev = kvp_ref[:, c:c + LANES] if blk == 0 else kvc_ref[r0 - WINDOW:r0, c:c + LANES]
            out.append(jnp.concatenate([prev, kvc_ref[r0:r0 + WINDOW, c:c + LANES]], axis=0))
        return out

    def scores(blk, kvh, p0):
        rows = slice(blk * WINDOW, (blk + 1) * WINDOW)
        c0 = kvh * kv_group_cols
        k_a, k_b, v_a, v_b = keys_values(blk, kvh)
        qs = jnp.concatenate(
            [q_ref[rows, c0 + p * LANES:c0 + (p + 1) * LANES]
             for p in range(p0, p0 + PAIRS_PER_DOT)],
            axis=0)
        logits = (lax.dot_general(qs, k_a, contract_last, preferred_element_type=F32),
                  lax.dot_general(qs, k_b, contract_last, preferred_element_type=F32))
        return logits, v_a, v_b

    def softmax(blk, kvh, p0, logits):
        sel = jnp.where(j == 0, 0, 1) if blk == 0 else 1
        probs = ([], [])
        inv = []
        for p in range(p0, p0 + PAIRS_PER_DOT):
            top, total, sink = [], [], []
            for st in range(2):
                hd = kvh * ATT_GROUP + 2 * p + st
                lg = logits[st][(p - p0) * WINDOW:(p - p0 + 1) * WINDOW, :]
                l = jnp.where(own, lg[:, WINDOW:], lg[:, :WINDOW]) + bias_ref[sel, hd]
                s = sink_ref[hd]
                m = jnp.maximum(jnp.max(l, axis=-1, keepdims=True), s)
                pe = jnp.exp2(l - m)
                probs[st].append(jnp.concatenate(
                    [jnp.where(own, 0.0, pe).astype(BF16),
                     jnp.where(own, pe, 0.0).astype(BF16)], axis=1))
                top.append(m)
                total.append(jnp.sum(pe, axis=-1, keepdims=True))
                sink.append(s)
            sink_term = jnp.exp2(jnp.where(lo1, sink[0], sink[1]) - jnp.where(lo1, top[0], top[1]))
            inv.append(1.0 / (jnp.where(lo1, total[0], total[1]) + sink_term))
        return jnp.concatenate(probs[0], axis=0), jnp.concatenate(probs[1], axis=0), inv

    def output(blk, kvh, p0, p_a, p_b, inv, v_a, v_b):
        rows = slice(blk * WINDOW, (blk + 1) * WINDOW)
        c0 = kvh * kv_group_cols
        o = (jnp.dot(p_a, v_a, preferred_element_type=F32)
             + jnp.dot(p_b, v_b, preferred_element_type=F32))
        for p in range(p0, p0 + PAIRS_PER_DOT):
            cols = slice(c0 + p * LANES, c0 + (p + 1) * LANES)
            gt = gate_ref[rows, cols].astype(F32)
            og = o[(p - p0) * WINDOW:(p - p0 + 1) * WINDOW, :] * inv[p - p0] * gt
            out_ref[rows, cols] = og.astype(out_ref.dtype)

    for unit in units:
        logits, v_a, v_b = scores(*unit)
        p_a, p_b, inv = softmax(*unit, logits)
        output(*unit, p_a, p_b, inv, v_a, v_b)


def _swa_prompt(sinks, q, kvpad, gate, bias, batch, seq):
    m = batch * seq
    nt = seq // ATT_ROWS
    blocks_per_tile = ATT_ROWS // WINDOW
    blocks_per_seq = seq // WINDOW
    tile = lambda b, j: (b * nt + j, 0)
    prev = lambda b, j: (b * blocks_per_seq + jnp.maximum(j * blocks_per_tile - 1, 0), 0)
    return pl.pallas_call(
        _swa_prompt_kernel,
        grid=(batch, nt),
        in_specs=[pl.BlockSpec(memory_space=pltpu.SMEM),
                  pl.BlockSpec((ATT_ROWS, ATT_WIDTH), tile),
                  pl.BlockSpec((ATT_ROWS, PADDED_KV_WIDTH), tile),
                  pl.BlockSpec((WINDOW, PADDED_KV_WIDTH), prev),
                  pl.BlockSpec((ATT_ROWS, ATT_WIDTH), tile),
                  pl.BlockSpec((2, ATT_HEADS, WINDOW, WINDOW), lambda b, j: (0, 0, 0, 0))],
        out_specs=pl.BlockSpec((ATT_ROWS, ATT_WIDTH), tile),
        out_shape=jax.ShapeDtypeStruct((m, ATT_WIDTH), BF16),
        compiler_params=_params(2),
        name="swa_prompt",
    )(sinks, q, kvpad, kvpad, gate, bias)


def _sample_head_order():
    order = []
    for pair in range(ATT_KV_HEADS // 2):
        base = pair * 2 * ATT_GROUP
        for kv in range(2):
            for half in range(2):
                order += [base + kv * ATT_GROUP + 2 * g + half for g in range(ATT_GROUP // 2)]
    return np.asarray(order, np.int32)


def _swa_sample_pieces(step, q_ref, kvt_ref, ckt_ref, cvt_ref, gate_ref, bias_ref, sink_ref,
                       o_ref, cko_ref, cvo_ref):
    half = ATT_HEAD_DIM
    quarter = ATT_GROUP // 2
    contract_last = (((1,), (1,)), ((), ()))
    newest = lax.broadcasted_iota(jnp.int32, (ATT_KV_WIDTH, WINDOW), 1) == WINDOW - 1
    seq_lane = lax.broadcasted_iota(jnp.int32, (2 * ATT_KV_WIDTH, LANES), 1)
    lo4 = lax.broadcasted_iota(jnp.int32, (quarter, LANES), 1) < half
    new_cache = {}

    def update(i):
        b = step * SAMPLE_ATT_BATCH + i
        new_col = jnp.sum(jnp.where(seq_lane == b, kvt_ref[...], 0.0), axis=1, keepdims=True)
        kb = jnp.where(newest, new_col[:ATT_KV_WIDTH], pltpu.roll(ckt_ref[i], WINDOW - 1, axis=1))
        vb = jnp.where(newest, new_col[ATT_KV_WIDTH:], pltpu.roll(cvt_ref[i], WINDOW - 1, axis=1))
        cko_ref[i] = kb
        cvo_ref[i] = vb
        new_cache[i] = (kb, vb)

    def attend(i, pair):
        b = step * SAMPLE_ATT_BATCH + i
        kb, vb = new_cache[i]
        groups = slice(pair * ATT_GROUP, (pair + 1) * ATT_GROUP)
        hs = slice(pair * 2 * ATT_GROUP, (pair + 1) * 2 * ATT_GROUP)
        feat = slice(pair * LANES, (pair + 1) * LANES)
        g8 = q_ref[b, groups, :]
        g8r = pltpu.roll(g8, half, axis=1)
        qbd = jnp.concatenate(
            [jnp.where(lo4, g8[:quarter], 0.0), jnp.where(lo4, g8r[:quarter], 0.0),
             jnp.where(lo4, 0.0, g8r[quarter:]), jnp.where(lo4, 0.0, g8[quarter:])],
            axis=0)
        l = jnp.dot(qbd, kb[feat, :], preferred_element_type=F32) + bias_ref[hs, :]
        s = sink_ref[hs, 0:1]
        m = jnp.maximum(jnp.max(l, axis=-1, keepdims=True), s)
        pe = jnp.exp2(l - m)
        den = jnp.sum(pe, axis=-1, keepdims=True) + jnp.exp2(s - m)
        o2 = lax.dot_general(pe, vb[feat, :], contract_last,
                             preferred_element_type=F32) / den
        o2r = pltpu.roll(o2, half, axis=1)
        og = jnp.concatenate(
            [jnp.where(lo4, o2[:quarter], o2r[quarter:2 * quarter]),
             jnp.where(lo4, o2r[2 * quarter:3 * quarter], o2[3 * quarter:])], axis=0)
        o_ref[b, groups, :] = og * gate_ref[b, groups, :]

    pieces = []
    for i in range(SAMPLE_ATT_BATCH):
        pieces.append(functools.partial(update, i))
        pieces += [functools.partial(attend, i, pair) for pair in range(ATT_KV_HEADS // 2)]
    return pieces


def _group_norm_gate(o, gate):
    mu = jnp.mean(o, axis=-1, keepdims=True)
    d = o - mu
    var = jnp.mean(d * d, axis=-1, keepdims=True)
    return d * lax.rsqrt(var + EPS) * _silu(gate)


def _ret_prompt_kernel(g_block_ref, q_ref, k_ref, v_ref, gate_ref, x_ref, wout_ref, gf_ref,
                       y_ref, r_ref, og_ref):
    c = pl.program_id(1)

    @pl.when(c == 0)
    def _():
        r_ref[...] = jnp.zeros_like(r_ref)

    causal = (lax.broadcasted_iota(jnp.int32, (RET_BLOCK, RET_BLOCK), 1)
              <= lax.broadcasted_iota(jnp.int32, (RET_BLOCK, RET_BLOCK), 0))
    contract_last = (((1,), (1,)), ((), ()))
    n_blocks = RET_ROWS // RET_BLOCK
    for blk in range(n_blocks):
        rows = slice(blk * RET_BLOCK, (blk + 1) * RET_BLOCK)
        for h in range(RET_HEADS):
            qcols = slice(h * RET_QK_DIM, (h + 1) * RET_QK_DIM)
            vcols = slice(h * RET_V_DIM, (h + 1) * RET_V_DIM)
            qh = q_ref[rows, qcols]
            kh = k_ref[rows, qcols]
            vh = v_ref[rows, vcols]
            state = r_ref[0, h]
            scores = jnp.where(
                causal, lax.dot_general(qh, kh, contract_last, preferred_element_type=F32), 0.0)
            o = (jnp.dot(scores.astype(BF16), vh, preferred_element_type=F32)
                 + jnp.dot(qh, state.astype(BF16), preferred_element_type=F32))
            r_ref[0, h] = g_block_ref[h] * (
                state + jnp.dot(kh.T, vh, preferred_element_type=F32))
            og_ref[h, rows, :] = _group_norm_gate(
                o, gate_ref[rows, vcols].astype(F32)).astype(og_ref.dtype)
            if blk == n_blocks - 1:
                part = jnp.dot(og_ref[h], wout_ref[vcols, :].astype(BF16),
                               preferred_element_type=F32)
                if h == 0:
                    y_ref[...] = x_ref[...] + part
                elif h < RET_HEADS - 1:
                    y_ref[...] += part
                else:
                    y_ref[...] = _rmsnorm(y_ref[...] + part, gf_ref[...])


def _ret_prompt(q, k, v, gate, x, w_out, final_g, batch, seq):
    m = batch * seq
    nt = seq // RET_ROWS
    g_block = jnp.asarray(_decay_scales_np(RET_BLOCK, RET_BLOCK)[2])
    rows = lambda b, c: (b * nt + c, 0)
    return pl.pallas_call(
        _ret_prompt_kernel,
        grid=(batch, nt),
        in_specs=[pl.BlockSpec(memory_space=pltpu.SMEM),
                  pl.BlockSpec((RET_ROWS, RET_QK_WIDTH), rows),
                  pl.BlockSpec((RET_ROWS, RET_QK_WIDTH), rows),
                  pl.BlockSpec((RET_ROWS, RET_WIDTH), rows),
                  pl.BlockSpec((RET_ROWS, RET_WIDTH), rows),
                  pl.BlockSpec((RET_ROWS, D_MODEL), rows),
                  _resident((RET_WIDTH, D_MODEL), lambda b, c: (0, 0)),
                  pl.BlockSpec((1, D_MODEL), lambda b, c: (0, 0))],
        out_specs=[pl.BlockSpec((RET_ROWS, D_MODEL), rows),
                   pl.BlockSpec((1, RET_HEADS, RET_QK_DIM, RET_V_DIM), lambda b, c: (b, 0, 0, 0))],
        out_shape=[jax.ShapeDtypeStruct((m, D_MODEL), F32),
                   jax.ShapeDtypeStruct((batch, RET_HEADS, RET_QK_DIM, RET_V_DIM), F32)],
        scratch_shapes=[pltpu.VMEM((RET_HEADS, RET_ROWS, RET_V_DIM), BF16)],
        compiler_params=_params(2),
        name="ret_prompt",
    )(g_block, q, k, v, gate, x, w_out, final_g)


def kernel(x_prompt, x_sample, cache_swa_k, cache_swa_v, state_ret, norm_g, final_norm_g, rel_bias,
           w_in_attn, attn_sinks, w_out_attn, w_in_ret, w_out_ret):
    batch, seq, _ = x_prompt.shape
    n_s = x_sample.shape[0]
    xp = x_prompt.reshape(batch * seq, D_MODEL)
    xs = x_sample.reshape(n_s, D_MODEL)
    g0 = norm_g[0].reshape(1, D_MODEL)
    g1 = norm_g[1].reshape(1, D_MODEL)
    gf = final_norm_g.reshape(1, D_MODEL)
    w_out_a = w_out_attn[0]
    w_out_r = w_out_ret[0]
    sinks = attn_sinks[0] * LOG2E

    bias = _bias_table(rel_bias)

    qs, kvs, gates, w_in_a = _attn_inproj(xs, g0, w_in_attn[0], F32)
    order = _sample_head_order()
    bias_s = bias[1, :, WINDOW - 1, :][order]
    sinks_rep = jnp.broadcast_to(sinks[order][:, None], (ATT_HEADS, LANES))
    to_feature_major = lambda c: jnp.transpose(c, (0, 2, 3, 1)).reshape(n_s, ATT_KV_WIDTH, WINDOW)
    to_window_major = lambda c: jnp.transpose(
        c.reshape(n_s, ATT_KV_HEADS, ATT_HEAD_DIM, WINDOW), (0, 3, 1, 2))[None]
    n_groups = ATT_WIDTH // LANES
    q, kv, gate, kvpad, ogs, ckt_new, cvt_new = _attn_inproj(
        xp, g0, w_in_a, BF16,
        sample=(qs.reshape(n_s, n_groups, LANES), kvs,
                to_feature_major(cache_swa_k[0]), to_feature_major(cache_swa_v[0]),
                gates.reshape(n_s, n_groups, LANES), bias_s, sinks_rep))
    og = _swa_prompt(sinks, q, kvpad, gate, bias, batch, seq)
    xp1 = _outproj(og, w_out_a, xp)
    kv_last = kv.reshape(batch, seq, 2 * ATT_KV_WIDTH)[:, seq - WINDOW:]
    k_prompt = kv_last[..., :ATT_KV_WIDTH].reshape(1, batch, WINDOW, ATT_KV_HEADS, ATT_HEAD_DIM)
    v_prompt = kv_last[..., ATT_KV_WIDTH:].reshape(1, batch, WINDOW, ATT_KV_HEADS, ATT_HEAD_DIM)
    xs1 = _outproj(ogs.reshape(n_s, ATT_WIDTH), w_out_a, xs)
    k_sample = to_window_major(ckt_new)
    v_sample = to_window_major(cvt_new)

    q_s, k_s, v_s, gate_s, w_in_r = _ret_inproj(
        xs1, g1, w_in_ret[0], np.full((n_s,), PAST_LEN), 1, F32)
    q, k, v, gate, og_s, r_sample = _ret_inproj(
        xp1, g1, w_in_r, np.arange(seq), RET_BLOCK, BF16,
        sample=(q_s, k_s, v_s, gate_s, state_ret[0]))
    y_prompt, r_prompt = _ret_prompt(q, k, v, gate, xp1, w_out_r, gf, batch, seq)
    y_prompt = y_prompt.reshape(batch, seq, D_MODEL)
    y_sample = _outproj(og_s, w_out_r, xs1, gf).reshape(n_s, 1, D_MODEL)

    return (y_prompt, y_sample, k_prompt, v_prompt, r_prompt[None],
            k_sample, v_sample, r_sample[None])
```
